```python
import jax, jax.numpy as jnp
from jax import lax
import numpy as np

D_MODEL = 1024
BATCH = 2
SEQ = 8192
DEPTH = 1
DEC_BATCH = 32
DEC_SEQ = 8
PAST_LEN = 8192
PAGE_SIZE = 128

D_MIX = D_MODEL
GLA_HEADS = 4
GLA_DK = D_MIX // 16
GLA_DV = D_MIX // 8
GLA_RANK = 16
GLA_TAU = 16.0
GLA_CHUNK = 64
SWA_HEADS = 8
SWA_HD = D_MIX // 16
DILATED_PATTERNS = ((128, 1), (512, 4), (2048, 16))
SWA_WINDOW = 2048
SWA_QBLOCK = 128
MEM_LEN = 256
X_HEADS = 4
X_HD = D_MODEL // 8
N_EXPERTS = 32
TOP_K = 4
D_EXPERT = D_MODEL
SWIGLU_ALPHA = 1.702
SWIGLU_LIMIT = 7.0
EPS = 1e-6

GLA_QK_W = GLA_HEADS * GLA_DK
GLA_V_W = GLA_HEADS * GLA_DV
SWA_W = SWA_HEADS * SWA_HD
IN_WIDTHS = (GLA_QK_W, GLA_QK_W, GLA_V_W, GLA_V_W, GLA_RANK, SWA_W, SWA_W, SWA_W)
D_IN = sum(IN_WIDTHS)
IN_SPLITS = [int(s) for s in np.cumsum(IN_WIDTHS)[:-1]]

kernel_name = "hymba_gla_dilated_swa_moe_step"


def rmsnorm(x, g):
    xf = x.astype(jnp.float32)
    y = xf * lax.rsqrt(jnp.mean(xf * xf, axis=-1, keepdims=True) + EPS)
    return (y * g.astype(jnp.float32)).astype(x.dtype)


def gla_chunked(q, k, v, log_a, s0):
    B, T, H, DK = q.shape
    DV = v.shape[-1]
    C = min(GLA_CHUNK, T)
    n = T // C
    f32 = jnp.float32

    def to_chunks(t):
        return jnp.moveaxis(t.astype(f32).reshape(B, n, C, *t.shape[2:]), 1, 0)

    causal = jnp.tril(jnp.ones((C, C), dtype=bool))[None, :, :, None, None]

    def step(S, inp):
        qc, kc, vc, gc = inp
        b = jnp.cumsum(gc, axis=1)
        diff = b[:, :, None] - b[:, None, :]
        decay = jnp.exp(jnp.where(causal, diff, -jnp.inf))
        scores = jnp.einsum('bthk,bshk,btshk->bhts', qc, kc, decay)
        o = (jnp.einsum('bhts,bshv->bthv', scores, vc)
             + jnp.einsum('bthk,bhkv->bthv', qc * jnp.exp(b), S))
        b_end = b[:, -1]
        k_dec = kc * jnp.exp(b_end[:, None] - b)
        S_new = jnp.exp(b_end)[..., None] * S + jnp.einsum('bshk,bshv->bhkv', k_dec, vc)
        return S_new, o

    S_fin, o = lax.scan(step, s0.astype(f32), (to_chunks(q), to_chunks(k), to_chunks(v), to_chunks(log_a)))
    o = jnp.moveaxis(o, 0, 1).reshape(B, T, H, DV)
    return o.astype(q.dtype), S_fin.astype(s0.dtype)


def dilated_attention(q, k_all, v_all, q_off):
    B, Tq, H, D = q.shape
    qb_len = min(SWA_QBLOCK, Tq)
    n_blocks = Tq // qb_len
    slopes = 2.0 ** (-8.0 * jnp.arange(1, H + 1, dtype=jnp.float32) / H)
    scale = D ** -0.5

    def one_block(i0):
        qb = lax.dynamic_slice_in_dim(q, i0, qb_len, axis=1)
        qpos = q_off + i0 + jnp.arange(qb_len)
        outs, logw = [], []
        for window, dil in DILATED_PATTERNS:
            dist = dil * jnp.arange(window // dil + 1)
            idx = qpos[:, None] - dist[None, :]
            valid = idx >= 0
            idx = jnp.maximum(idx, 0)
            kg = k_all[:, idx]
            vg = v_all[:, idx]
            s = jnp.einsum('bqhd,bqnhd->bhqn', qb, kg).astype(jnp.float32) * scale
            s = s - slopes[:, None, None] * dist.astype(jnp.float32)
            s = jnp.where(valid[None, None], s, -jnp.inf)
            m = jnp.max(s, axis=-1, keepdims=True)
            p = jnp.exp(s - m)
            den = jnp.sum(p, axis=-1)
            o = jnp.einsum('bhqn,bqnhd->bqhd', p, vg.astype(jnp.float32))
            outs.append(o / jnp.moveaxis(den, 1, 2)[..., None])
            logw.append(jnp.log(den) + m[..., 0])
        w = jax.nn.softmax(jnp.stack(logw, 0), axis=0)
        return jnp.einsum('pbhq,pbqhd->bqhd', w, jnp.stack(outs, 0)).astype(q.dtype)

    o = lax.map(one_block, jnp.arange(n_blocks) * qb_len)
    return jnp.moveaxis(o, 0, 1).reshape(B, Tq, H, D)


def hybrid_mixer(h, gla_s0, k_past, v_past, w_in, w_gla_a2, b_gla_a, g_gla_out, g_swa_q, g_swa_k, w_out):
    B, T, _ = h.shape
    z = h @ w_in
    gq, gk, gv, gr, ga, sq, sk, sv = jnp.split(z, IN_SPLITS, axis=-1)
    gq = gq.reshape(B, T, GLA_HEADS, GLA_DK) * (GLA_DK ** -0.5)
    gk = gk.reshape(B, T, GLA_HEADS, GLA_DK)
    gv = gv.reshape(B, T, GLA_HEADS, GLA_DV)
    log_a = jax.nn.log_sigmoid((ga @ w_gla_a2 + b_gla_a).astype(jnp.float32)) / GLA_TAU
    log_a = log_a.reshape(B, T, GLA_HEADS, GLA_DK)
    if gla_s0 is None:
        gla_s0 = jnp.zeros((B, GLA_HEADS, GLA_DK, GLA_DV), h.dtype)
    o_gla, s_new = gla_chunked(gq, gk, gv, log_a, gla_s0)
    o_gla = rmsnorm(o_gla, g_gla_out).reshape(B, T, GLA_V_W) * jax.nn.silu(gr)
    q = rmsnorm(sq.reshape(B, T, SWA_HEADS, SWA_HD), g_swa_q)
    k = rmsnorm(sk.reshape(B, T, SWA_HEADS, SWA_HD), g_swa_k)
    v = sv.reshape(B, T, SWA_HEADS, SWA_HD)
    if k_past is None:
        k_all, v_all, q_off, w_buf = k, v, 0, min(SWA_WINDOW, T)
    else:
        k_all = jnp.concatenate([k_past, k], axis=1)
        v_all = jnp.concatenate([v_past, v], axis=1)
        q_off, w_buf = k_past.shape[1], k_past.shape[1]
    o_swa = dilated_attention(q, k_all, v_all, q_off).reshape(B, T, SWA_W)
    y = jnp.concatenate([o_gla, o_swa], axis=-1) @ w_out
    return y, s_new, k_all[:, -w_buf:], v_all[:, -w_buf:]


def memory_kv(mem, g_mem, w_mk, w_mv, g_xk):
    B, M, _ = mem.shape
    hm = rmsnorm(mem, g_mem)
    mk = rmsnorm((hm @ w_mk).reshape(B, M, X_HEADS, X_HD), g_xk)
    mv = (hm @ w_mv).reshape(B, M, X_HEADS, X_HD)
    return mk, mv


def cross_attention(h, mk, mv, w_xq, g_xq, w_xo):
    B, T, _ = h.shape
    q = rmsnorm((h @ w_xq).reshape(B, T, X_HEADS, X_HD), g_xq)
    s = jnp.einsum('bthd,bmhd->bhtm', q, mk).astype(jnp.float32) * (X_HD ** -0.5)
    p = jax.nn.softmax(s, axis=-1).astype(mv.dtype)
    o = jnp.einsum('bhtm,bmhd->bthd', p, mv).reshape(B, T, X_HEADS * X_HD)
    return o @ w_xo


def moe(h, w_router, b_router, w_gate, b_gate, w_up, b_up, w_down, b_down):
    B, T, D = h.shape
    hf = h.reshape(B * T, D)
    logits = (hf @ w_router + b_router).astype(jnp.float32)
    top_v, top_i = lax.top_k(logits, TOP_K)
    gates = jax.nn.softmax(top_v, axis=-1)
    combine = jnp.einsum('nk,nke->ne', gates, jax.nn.one_hot(top_i, N_EXPERTS, dtype=jnp.float32))
    combine = combine.astype(h.dtype)
    out = jnp.zeros_like(hf)
    for e in range(N_EXPERTS):
        g = jnp.minimum(hf @ w_gate[e] + b_gate[e], SWIGLU_LIMIT)
        u = jnp.clip(hf @ w_up[e] + b_up[e], -SWIGLU_LIMIT, SWIGLU_LIMIT)
        a = g * jax.nn.sigmoid(SWIGLU_ALPHA * g) * (u + 1.0)
        out = out + combine[:, e:e + 1] * (a @ w_down[e] + b_down[e])
    return out.reshape(B, T, D)


def trunk_layer(x, gla_s0, k_past, v_past, mk, mv,
                g_mix, w_in, w_gla_a2, b_gla_a, g_gla_out, g_swa_q, g_swa_k, w_out,
                g_xattn, w_xq, g_xq, w_xo,
                g_ffn, w_router, b_router, w_gate, b_gate, w_up, b_up, w_down, b_down):
    y, s_new, kb, vb = hybrid_mixer(rmsnorm(x, g_mix), gla_s0, k_past, v_past,
                                    w_in, w_gla_a2, b_gla_a, g_gla_out, g_swa_q, g_swa_k, w_out)
    x = x + y
    x = x + cross_attention(rmsnorm(x, g_xattn), mk, mv, w_xq, g_xq, w_xo)
    x = x + moe(rmsnorm(x, g_ffn), w_router, b_router, w_gate, b_gate, w_up, b_up, w_down, b_down)
    return x, s_new, kb, vb


def setup_inputs(seed: int = 0) -> dict:
    key = jax.random.key(seed)
    ks = iter(jax.random.split(key, 64))
    L = DEPTH
    w_buf = min(SWA_WINDOW, PAST_LEN)

    def nrm(shape, scale):
        return jax.random.normal(next(ks), shape, jnp.float32) * scale

    def gain(shape):
        return 1.0 + 0.01 * jax.random.normal(next(ks), shape, jnp.float32)

    return {
        "x_prompt": nrm((BATCH, SEQ, D_MODEL), 1.0),
        "x_sample": nrm((DEC_BATCH, DEC_SEQ, D_MODEL), 1.0),
        "mem_prompt": nrm((BATCH, MEM_LEN, D_MODEL), 1.0),
        "state_gla": nrm((L, DEC_BATCH, GLA_HEADS, GLA_DK, GLA_DV), 0.5),
        "cache_swa_k": nrm((L, DEC_BATCH, w_buf, SWA_HEADS, SWA_HD), 1.0),
        "cache_swa_v": nrm((L, DEC_BATCH, w_buf, SWA_HEADS, SWA_HD), 1.0),
        "cache_mem_k": nrm((L, DEC_BATCH, MEM_LEN, X_HEADS, X_HD), 1.0),
        "cache_mem_v": nrm((L, DEC_BATCH, MEM_LEN, X_HEADS, X_HD), 1.0),
        "g_mix": gain((L, D_MODEL)),
        "w_in": nrm((L, D_MODEL, D_IN), D_MODEL ** -0.5),
        "w_gla_a2": nrm((L, GLA_RANK, GLA_QK_W), GLA_RANK ** -0.5),
        "b_gla_a": nrm((L, GLA_QK_W), 0.1),
        "g_gla_out": gain((L, GLA_DV)),
        "g_swa_q": gain((L, SWA_HD)),
        "g_swa_k": gain((L, SWA_HD)),
        "w_out": nrm((L, D_MIX, D_MODEL), D_MIX ** -0.5),
        "g_mem": gain((L, D_MODEL)),
        "w_mk": nrm((L, D_MODEL, X_HEADS * X_HD), D_MODEL ** -0.5),
        "w_mv": nrm((L, D_MODEL, X_HEADS * X_HD), D_MODEL ** -0.5),
        "g_xk": gain((L, X_HD)),
        "g_xattn": gain((L, D_MODEL)),
        "w_xq": nrm((L, D_MODEL, X_HEADS * X_HD), D_MODEL ** -0.5),
        "g_xq": gain((L, X_HD)),
        "w_xo": nrm((L, X_HEADS * X_HD, D_MODEL), (X_HEADS * X_HD) ** -0.5),
        "g_ffn": gain((L, D_MODEL)),
        "w_router": nrm((L, D_MODEL, N_EXPERTS), D_MODEL ** -0.5),
        "b_router": nrm((L, N_EXPERTS), 0.01),
        "w_gate": nrm((L, N_EXPERTS, D_MODEL, D_EXPERT), D_MODEL ** -0.5),
        "b_gate": nrm((L, N_EXPERTS, D_EXPERT), 0.01),
        "w_up": nrm((L, N_EXPERTS, D_MODEL, D_EXPERT), D_MODEL ** -0.5),
        "b_up": nrm((L, N_EXPERTS, D_EXPERT), 0.01),
        "w_down": nrm((L, N_EXPERTS, D_EXPERT, D_MODEL), D_EXPERT ** -0.5),
        "b_down": nrm((L, N_EXPERTS, D_MODEL), 0.01),
    }


def reference(x_prompt, x_sample, mem_prompt, state_gla, cache_swa_k, cache_swa_v, cache_mem_k, cache_mem_v,
              g_mix, w_in, w_gla_a2, b_gla_a, g_gla_out, g_swa_q, g_swa_k, w_out,
              g_mem, w_mk, w_mv, g_xk, g_xattn, w_xq, g_xq, w_xo,
              g_ffn, w_router, b_router, w_gate, b_gate, w_up, b_up, w_down, b_down):
    y_p, y_s = x_prompt, x_sample
    gla_p, swk_p, swv_p, mk_p_all, mv_p_all = [], [], [], [], []
    gla_s, swk_s, swv_s = [], [], []
    for l in range(DEPTH):
        lw = (g_mix[l], w_in[l], w_gla_a2[l], b_gla_a[l], g_gla_out[l], g_swa_q[l], g_swa_k[l], w_out[l],
              g_xattn[l], w_xq[l], g_xq[l], w_xo[l],
              g_ffn[l], w_router[l], b_router[l], w_gate[l], b_gate[l], w_up[l], b_up[l], w_down[l], b_down[l])
        mk_p, mv_p = memory_kv(mem_prompt, g_mem[l], w_mk[l], w_mv[l], g_xk[l])
        y_p, s_p, kb_p, vb_p = trunk_layer(y_p, None, None, None, mk_p, mv_p, *lw)
        y_s, s_s, kb_s, vb_s = trunk_layer(y_s, state_gla[l], cache_swa_k[l], cache_swa_v[l],
                                           cache_mem_k[l], cache_mem_v[l], *lw)
        gla_p.append(s_p); swk_p.append(kb_p); swv_p.append(vb_p)
        mk_p_all.append(mk_p); mv_p_all.append(mv_p)
        gla_s.append(s_s); swk_s.append(kb_s); swv_s.append(vb_s)
    return (y_p, y_s,
            jnp.stack(gla_p, 0), jnp.stack(swk_p, 0), jnp.stack(swv_p, 0),
            jnp.stack(mk_p_all, 0), jnp.stack(mv_p_all, 0),
            jnp.stack(gla_s, 0), jnp.stack(swk_s, 0), jnp.stack(swv_s, 0))
```

```python
import functools

import numpy as np
import jax
import jax.numpy as jnp
from jax import lax
from jax.experimental import pallas as pl
from jax.experimental.pallas import tpu as pltpu

F32 = jnp.float32
BF16 = jnp.bfloat16

D_MODEL = 1024
GLA_HEADS = 4
GLA_DK = 64
GLA_DV = 128
GLA_RANK = 16
GLA_TAU = 16.0
GLA_CHUNK = 64
GLA_SUB = 16
SWA_HEADS = 8
SWA_HD = 64
DILATED_PATTERNS = ((128, 1), (512, 4), (2048, 16))
BAND = 128
SWA_WINDOW = 2048
MEM_LEN = 256
X_HEADS = 4
X_HD = 128
N_EXPERTS = 32
TOP_K = 4
SWIGLU_ALPHA = 1.702
SWIGLU_LIMIT = 7.0
EPS = 1e-6

GLA_QK_W = GLA_HEADS * GLA_DK
GLA_V_W = GLA_HEADS * GLA_DV
SWA_W = SWA_HEADS * SWA_HD
X_W = X_HEADS * X_HD
LANES = 128
NEG = -1e30
VMEM_LIMIT = 56 * 1024 * 1024

_C_GQ, _C_GK, _C_GV, _C_GR, _C_SQ, _C_SK, _C_SV, _C_GA = 0, 256, 512, 1024, 1536, 2048, 2560, 3072
D_IN_PAD = 3200

MOE_TM = 256


def _cparams(*sem):
    return pltpu.CompilerParams(dimension_semantics=sem, vmem_limit_bytes=VMEM_LIMIT)


def _dot(a, b):
    return jnp.dot(a, b, preferred_element_type=F32)


def _dot_nt(a, b):
    return lax.dot_general(a, b, (((1,), (1,)), ((), ())), preferred_element_type=F32)


def _rms_rows(x, g):
    return x * lax.rsqrt(jnp.mean(x * x, axis=-1, keepdims=True) + EPS) * g


def _split_bf16(x, n):
    out = []
    for _ in range(n - 1):
        hi = x.astype(BF16)
        out.append(hi)
        x = x - hi.astype(F32)
    out.append(x.astype(BF16))
    return out


def _group_rms(z, g, ones_bd, group):
    hi, lo = _split_bf16(z * z, 2)
    ss = _dot(hi, ones_bd) + _dot(lo, ones_bd)
    return z * lax.rsqrt(ss * (1.0 / group) + EPS) * g


def _log_sigmoid(x):
    return jnp.minimum(x, 0.0) - jnp.log1p(jnp.exp(-jnp.abs(x)))


def _sigmoid(x):
    return 1.0 / (1.0 + jnp.exp(-x))


def _in_proj_kernel(x_ref, gmix_ref, w_ref, wa2_ref, ba_ref, gsq_ref, gsk_ref, bd_ref,
                    oq_ref, ok_ref, ov_ref, or_ref, ola_ref, osq_ref, osk_ref, osv_ref):
    h = _rms_rows(x_ref[...], gmix_ref[...]).astype(BF16)

    def proj(lo, width):
        return _dot(h, w_ref[:, lo:lo + width])

    oq_ref[...] = proj(_C_GQ, GLA_QK_W) * (GLA_DK ** -0.5)
    ok_ref[...] = proj(_C_GK, GLA_QK_W)
    ov_ref[...] = proj(_C_GV, GLA_V_W)
    or_ref[...] = proj(_C_GR, GLA_V_W)
    osv_ref[...] = proj(_C_SV, SWA_W)
    ga = proj(_C_GA, LANES)
    xa = _dot(ga.astype(BF16), wa2_ref[...]) + ba_ref[...]
    ola_ref[...] = _log_sigmoid(xa) * (1.0 / GLA_TAU)
    bd = bd_ref[...]
    osq_ref[...] = _group_rms(proj(_C_SQ, SWA_W), gsq_ref[...], bd, SWA_HD)
    osk_ref[...] = _group_rms(proj(_C_SK, SWA_W), gsk_ref[...], bd, SWA_HD)


def _in_proj(x, gmix, w_in_p, wa2_p, ba, gsq, gsk, bd, tm):
    n = x.shape[0]
    row = lambda w: pl.BlockSpec((tm, w), lambda i: (i, 0))
    full = lambda a: pl.BlockSpec(a.shape, lambda i: (0,) * a.ndim)
    widths = (GLA_QK_W, GLA_QK_W, GLA_V_W, GLA_V_W, GLA_QK_W, SWA_W, SWA_W, SWA_W)
    return pl.pallas_call(
        _in_proj_kernel,
        grid=(n // tm,),
        in_specs=[row(D_MODEL), full(gmix), full(w_in_p), full(wa2_p), full(ba), full(gsq), full(gsk), full(bd)],
        out_specs=[row(w) for w in widths],
        out_shape=[jax.ShapeDtypeStruct((n, w), F32) for w in widths],
        compiler_params=_cparams("parallel"),
        name="in_proj",
    )(x, gmix, w_in_p, wa2_p, ba, gsq, gsk, bd)


def _gla_kernel(q_ref, k_ref, v_ref, la_ref, r_ref, s0t_ref, gout_ref, lcat_ref, qmask_ref,
                o_ref, st_ref, st_scr, *, n_chunks):
    tb = pl.program_id(1)
    C, S, NS = GLA_CHUNK, GLA_SUB, GLA_CHUNK // GLA_SUB

    @pl.when(tb == 0)
    def _():
        st_scr[...] = s0t_ref[...]

    lcat = lcat_ref[...]
    qmask = qmask_ref[...]
    gout = gout_ref[...]
    row = lax.broadcasted_iota(jnp.int32, (C, GLA_DK), 0)
    tril = lax.broadcasted_iota(jnp.int32, (C, C), 0) >= lax.broadcasted_iota(jnp.int32, (C, C), 1)

    def chunk(c, carry):
        r0 = pl.multiple_of(c * C, C)
        rows = pl.ds(r0, C)
        g3 = _split_bf16(la_ref[rows, :], 3)
        b_r = _dot(lcat, g3[0]) + _dot(lcat, g3[1]) + _dot(lcat, g3[2])
        b = b_r[:C]
        ref = b_r[C:]
        b_end = b[C - 1:C, :]
        q = q_ref[rows, :]
        k = k_ref[rows, :]
        q_hat = q * jnp.exp(b - ref)
        q_til = q * jnp.exp(b)
        k_dec = k * jnp.exp(b_end - b)
        a_end = jnp.exp(b_end)
        for h in range(GLA_HEADS):
            sl = slice(h * GLA_DK, (h + 1) * GLA_DK)
            vl = slice(h * GLA_DV, (h + 1) * GLA_DV)
            b_h, ref_h, k_h = b[:, sl], ref[:, sl], k[:, sl]
            pieces = []
            for i in range(NS):
                e = ref_h[S * i:S * i + 1, :] - b_h
                e = jnp.where(row < S * (i + 1), e, NEG)
                pieces.append(k_h * jnp.exp(e))
            k_big = jnp.concatenate(pieces, axis=1).astype(BF16)
            q_big = (jnp.concatenate([q_hat[:, sl]] * NS, axis=1) * qmask).astype(BF16)
            sc = jnp.where(tril, _dot_nt(q_big, k_big), 0.0)
            v_h = v_ref[rows, vl]
            st = st_scr[h]
            o = _dot(sc.astype(BF16), v_h.astype(BF16)) + _dot_nt(q_til[:, sl].astype(BF16), st.astype(BF16))
            st_scr[h] = st * a_end[:, sl] + _dot(v_h.T.astype(BF16), k_dec[:, sl].astype(BF16))
            gate = r_ref[rows, vl]
            o_ref[rows, vl] = (_rms_rows(o, gout) * (gate * _sigmoid(gate))).astype(o_ref.dtype)
        return carry

    lax.fori_loop(0, n_chunks, chunk, 0)

    @pl.when(tb == pl.num_programs(1) - 1)
    def _():
        st_ref[...] = st_scr[...]


def _gla_consts():
    C, S = GLA_CHUNK, GLA_SUB
    t = np.arange(C)
    incl = (t[None, :] <= t[:, None]).astype(np.float32)
    upto = (t[None, :] < (t[:, None] // S) * S).astype(np.float32)
    lcat = np.concatenate([incl, upto], axis=0)
    col = np.arange((C // S) * GLA_DK)
    qmask = ((t[:, None] // S) == (col[None, :] // GLA_DK)).astype(np.float32)
    return jnp.asarray(lcat, BF16), jnp.asarray(qmask, F32)


def _gla(q, k, v, la, r, s0t, gout, tb):
    B, T, _ = q.shape
    lcat, qmask = _gla_consts()
    blk = lambda w: pl.BlockSpec((None, tb, w), lambda b, t: (b, t, 0))
    full = lambda a: pl.BlockSpec(a.shape, lambda b, t: (0,) * a.ndim)
    st_spec = pl.BlockSpec((None, GLA_HEADS, GLA_DV, GLA_DK), lambda b, t: (b, 0, 0, 0))
    return pl.pallas_call(
        functools.partial(_gla_kernel, n_chunks=tb // GLA_CHUNK),
        grid=(B, T // tb),
        in_specs=[blk(GLA_QK_W), blk(GLA_QK_W), blk(GLA_V_W), blk(GLA_QK_W), blk(GLA_V_W), st_spec,
                  full(gout), full(lcat), full(qmask)],
        out_specs=[blk(GLA_V_W), st_spec],
        out_shape=[jax.ShapeDtypeStruct((B, T, GLA_V_W), BF16),
                   jax.ShapeDtypeStruct((B, GLA_HEADS, GLA_DV, GLA_DK), F32)],
        scratch_shapes=[pltpu.VMEM((GLA_HEADS, GLA_DV, GLA_DK), F32)],
        compiler_params=_cparams("parallel", "arbitrary"),
        name="gla",
    )(q, k, v, la, r, s0t, gout, lcat, qmask)


def _band_attn_kernel(q_ref, kp_ref, kc_ref, vp_ref, vc_ref, o_ref, lse_ref, *, dil):
    i = pl.program_id(1)
    t = lax.broadcasted_iota(jnp.int32, (BAND, 2 * BAND), 0)
    c = lax.broadcasted_iota(jnp.int32, (BAND, 2 * BAND), 1)
    dist = BAND + t - c
    valid = (dist >= 0) & (dist <= BAND) & ((c >= BAND) | (i > 0))
    distf = (dist * dil).astype(F32)
    q = q_ref[...].astype(BF16)
    k = jnp.concatenate([kp_ref[...], kc_ref[...]], axis=0).astype(BF16)
    v = jnp.concatenate([vp_ref[...], vc_ref[...]], axis=0).astype(BF16)
    outs, lses = [], []
    for h in range(SWA_HEADS):
        sl = slice(h * SWA_HD, (h + 1) * SWA_HD)
        s = _dot_nt(q[:, sl], k[:, sl]) * (SWA_HD ** -0.5) - (2.0 ** -(h + 1)) * distf
        s = jnp.where(valid, s, NEG)
        m = jnp.max(s, axis=-1, keepdims=True)
        p = jnp.exp(s - m)
        den = jnp.sum(p, axis=-1, keepdims=True)
        outs.append(_dot(p.astype(BF16), v[:, sl]) / den)
        lses.append(jnp.broadcast_to(m + jnp.log(den), (BAND, SWA_HD)))
    o_ref[...] = jnp.concatenate(outs, axis=1)
    lse_ref[...] = jnp.concatenate(lses, axis=1)


def _band_attn(q, k, v, dil):
    S, Tc, W = q.shape
    cur = pl.BlockSpec((None, BAND, W), lambda s, i: (s, i, 0))
    prev = pl.BlockSpec((None, BAND, W), lambda s, i: (s, jnp.maximum(i - 1, 0), 0))
    return pl.pallas_call(
        functools.partial(_band_attn_kernel, dil=dil),
        grid=(S, Tc // BAND),
        in_specs=[cur, prev, cur, prev, cur],
        out_specs=[cur, cur],
        out_shape=[jax.ShapeDtypeStruct((S, Tc, W), F32)] * 2,
        compiler_params=_cparams("parallel", "parallel"),
        name="band_attn_d%d" % dil,
    )(q, k, k, v, v)


def _to_classes(x, dil):
    B, T, W = x.shape
    if dil == 1:
        return x
    return x.reshape(B, T // dil, dil, W).transpose(0, 2, 1, 3).reshape(B * dil, T // dil, W)


def _from_classes(x, dil, B):
    if dil == 1:
        return x
    S, Tc, W = x.shape
    return x.reshape(B, dil, Tc, W).transpose(0, 2, 1, 3).reshape(B, Tc * dil, W)


def _pattern_count(dist):
    cnt = jnp.zeros(dist.shape, F32)
    for window, dil in DILATED_PATTERNS:
        hit = (dist >= 0) & (dist <= window)
        if dil > 1:
            hit = hit & ((dist & (dil - 1)) == 0)
        cnt = cnt + hit.astype(F32)
    return cnt


def _samp_attn_kernel(q_ref, kn_ref, vn_ref, kp_ref, vp_ref, o_ref, ko_ref, vo_ref):
    P, Tn = kp_ref.shape[0], kn_ref.shape[0]
    ko_ref[0:P - Tn, :] = kp_ref[Tn:P, :]
    ko_ref[P - Tn:P, :] = kn_ref[...]
    vo_ref[0:P - Tn, :] = vp_ref[Tn:P, :]
    vo_ref[P - Tn:P, :] = vn_ref[...]

    qi = lax.broadcasted_iota(jnp.int32, (Tn, P), 0)
    dist_p = P + qi - lax.broadcasted_iota(jnp.int32, (Tn, P), 1)
    dist_n = lax.broadcasted_iota(jnp.int32, (Tn, Tn), 0) - lax.broadcasted_iota(jnp.int32, (Tn, Tn), 1)
    cnt_p, cnt_n = _pattern_count(dist_p), _pattern_count(dist_n)
    dpf, dnf = dist_p.astype(F32), dist_n.astype(F32)
    q = q_ref[...].astype(BF16)
    kp, vp = kp_ref[...].astype(BF16), vp_ref[...].astype(BF16)
    kn, vn = kn_ref[...].astype(BF16), vn_ref[...].astype(BF16)
    outs = []
    for h in range(SWA_HEADS):
        sl = slice(h * SWA_HD, (h + 1) * SWA_HD)
        slope = 2.0 ** -(h + 1)
        lp = jnp.where(cnt_p > 0, _dot_nt(q[:, sl], kp[:, sl]) * (SWA_HD ** -0.5) - slope * dpf, NEG)
        ln = jnp.where(cnt_n > 0, _dot_nt(q[:, sl], kn[:, sl]) * (SWA_HD ** -0.5) - slope * dnf, NEG)
        m = jnp.maximum(jnp.max(lp, axis=-1, keepdims=True), jnp.max(ln, axis=-1, keepdims=True))
        pp = cnt_p * jnp.exp(lp - m)
        pn = cnt_n * jnp.exp(ln - m)
        den = jnp.sum(pp, axis=-1, keepdims=True) + jnp.sum(pn, axis=-1, keepdims=True)
        outs.append((_dot(pp.astype(BF16), vp[:, sl]) + _dot(pn.astype(BF16), vn[:, sl])) / den)
    o_ref[...] = jnp.concatenate(outs, axis=1)


def _samp_attn(q, kn, vn, kp, vp):
    B, Tn, W = q.shape
    P = kp.shape[1]
    new = pl.BlockSpec((None, Tn, W), lambda b: (b, 0, 0))
    past = pl.BlockSpec((None, P, W), lambda b: (b, 0, 0))
    return pl.pallas_call(
        _samp_attn_kernel,
        grid=(B,),
        in_specs=[new, new, new, past, past],
        out_specs=[new, past, past],
        out_shape=[jax.ShapeDtypeStruct((B, Tn, W), F32), jax.ShapeDtypeStruct((B, P, W), F32),
                   jax.ShapeDtypeStruct((B, P, W), F32)],
        compiler_params=_cparams("parallel"),
        name="samp_attn",
    )(q, kn, vn, kp, vp)


def _mem_kv_kernel(mem_ref, gmem_ref, wk_ref, wv_ref, gxk_ref, mk_ref, mv_ref):
    hm = _rms_rows(mem_ref[...], gmem_ref[...]).astype(BF16)
    mk = _dot(hm, wk_ref[...])
    gxk = gxk_ref[...]
    for h in range(X_HEADS):
        sl = slice(h * X_HD, (h + 1) * X_HD)
        mk_ref[:, sl] = _rms_rows(mk[:, sl], gxk)
    mv_ref[...] = _dot(hm, wv_ref[...])


def _mem_kv(mem, gmem, wk, wv, gxk):
    n = mem.shape[0]
    return pl.pallas_call(
        _mem_kv_kernel,
        out_shape=[jax.ShapeDtypeStruct((n, X_W), F32)] * 2,
        compiler_params=pltpu.CompilerParams(vmem_limit_bytes=VMEM_LIMIT),
        name="mem_kv",
    )(mem, gmem, wk, wv, gxk)


def _post_kernel(*refs, n_pat, n_seg):
    n_swa = 2 * n_pat if n_pat > 1 else 1
    x_ref, og_ref = refs[0], refs[1]
    swa_refs = refs[2:2 + n_swa]
    (wout_ref, gx_ref, wxq_ref, gxq_ref, mk_ref, mv_ref, wxo_ref, gffn_ref, wr_ref, br_ref,
     x2_ref, h3_ref, ti_ref, tg_ref, ox_scr) = refs[2 + n_swa:]

    if n_pat > 1:
        o_p = [r[...] for r in swa_refs[:n_pat]]
        l_p = [r[...] for r in swa_refs[n_pat:]]
        lmax = functools.reduce(jnp.maximum, l_p)
        w_p = [jnp.exp(l - lmax) for l in l_p]
        o_swa = sum(w * o for w, o in zip(w_p, o_p)) / sum(w_p)
    else:
        o_swa = swa_refs[0][...]

    x1 = x_ref[...] + _dot(og_ref[...], wout_ref[0:GLA_V_W, :]) + _dot(o_swa.astype(BF16), wout_ref[GLA_V_W:, :])

    q = _dot(_rms_rows(x1, gx_ref[...]).astype(BF16), wxq_ref[...])
    gxq = gxq_ref[...]
    seg = x1.shape[0] // n_seg
    for h in range(X_HEADS):
        sl = slice(h * X_HD, (h + 1) * X_HD)
        qn = _rms_rows(q[:, sl], gxq).astype(BF16)
        for s in range(n_seg):
            rows = slice(s * seg, (s + 1) * seg)
            sc = _dot_nt(qn[rows], mk_ref[s, :, sl].astype(BF16)) * (X_HD ** -0.5)
            p = jnp.exp(sc - jnp.max(sc, axis=-1, keepdims=True))
            p = p / jnp.sum(p, axis=-1, keepdims=True)
            ox_scr[rows, sl] = _dot(p.astype(BF16), mv_ref[s, :, sl].astype(BF16))
    x2 = x1 + _dot(ox_scr[...].astype(BF16), wxo_ref[...])
    x2_ref[...] = x2

    h3 = _rms_rows(x2, gffn_ref[...]).astype(BF16)
    h3_ref[...] = h3
    work = _dot(h3, wr_ref[...]) + br_ref[...]
    lane = lax.broadcasted_iota(jnp.int32, work.shape, 1)
    vals, idxs = [], []
    for _ in range(TOP_K):
        m = jnp.max(work, axis=-1, keepdims=True)
        idx = jnp.min(jnp.where(work == m, lane, LANES), axis=-1, keepdims=True)
        vals.append(m)
        idxs.append(idx)
        work = jnp.where(lane == idx, -jnp.inf, work)
    es = [jnp.exp(v - vals[0]) for v in vals]
    den = sum(es)
    ti = jnp.zeros(work.shape, jnp.int32)
    tg = jnp.zeros(work.shape, F32)
    for j in range(TOP_K):
        ti = jnp.where(lane == j, idxs[j], ti)
        tg = jnp.where(lane == j, es[j] / den, tg)
    ti_ref[...] = ti
    tg_ref[...] = tg


def _post(x, og, swa, wout, gx, wxq, gxq, mk, mv, wxo, gffn, wr, br, tm, n_seg, rows_per_mem):
    n = x.shape[0]
    n_pat = len(swa) // 2 if len(swa) > 1 else 1
    row = lambda w: pl.BlockSpec((tm, w), lambda i: (i, 0))
    full = lambda a: pl.BlockSpec(a.shape, lambda i: (0,) * a.ndim)
    mem = pl.BlockSpec((n_seg, MEM_LEN, X_W), lambda i: ((i * tm) // (rows_per_mem * n_seg), 0, 0))
    return pl.pallas_call(
        functools.partial(_post_kernel, n_pat=n_pat, n_seg=n_seg),
        grid=(n // tm,),
        in_specs=[row(D_MODEL), row(GLA_V_W)] + [row(SWA_W)] * len(swa)
        + [full(wout), full(gx), full(wxq), full(gxq), mem, mem, full(wxo), full(gffn), full(wr), full(br)],
        out_specs=[row(D_MODEL), row(D_MODEL), row(LANES), row(LANES)],
        out_shape=[jax.ShapeDtypeStruct((n, D_MODEL), F32), jax.ShapeDtypeStruct((n, D_MODEL), BF16),
                   jax.ShapeDtypeStruct((n, LANES), jnp.int32), jax.ShapeDtypeStruct((n, LANES), F32)],
        scratch_shapes=[pltpu.VMEM((tm, X_W), F32)],
        compiler_params=_cparams("parallel"),
        name="post",
    )(x, og, *swa, wout, gx, wxq, gxq, mk, mv, wxo, gffn, wr, br)


def _moe_kernel(te_ref, nu_ref, x_ref, g_ref, wg_ref, bg_ref, wu_ref, bu_ref, wd_ref, bd_ref, y_ref):
    @pl.when(pl.program_id(0) < nu_ref[0])
    def _():
        x = x_ref[...]
        g = jnp.minimum(_dot(x, wg_ref[...]) + bg_ref[...], SWIGLU_LIMIT)
        u = jnp.clip(_dot(x, wu_ref[...]) + bu_ref[...], -SWIGLU_LIMIT, SWIGLU_LIMIT)
        a = g * _sigmoid(SWIGLU_ALPHA * g) * (u + 1.0)
        y_ref[...] = g_ref[...] * (_dot(a.astype(BF16), wd_ref[...]) + bd_ref[...])


def _moe(tile_expert, n_used, xs, slot_gate, wg, bg, wu, bu, wd, bd):
    P = xs.shape[0]
    tm = MOE_TM
    w_spec = pl.BlockSpec((None, D_MODEL, D_MODEL), lambda i, te, nu: (te[i], 0, 0))
    b_spec = pl.BlockSpec((None, 1, D_MODEL), lambda i, te, nu: (te[i], 0, 0))
    return pl.pallas_call(
        _moe_kernel,
        grid_spec=pltpu.PrefetchScalarGridSpec(
            num_scalar_prefetch=2,
            grid=(P // tm,),
            in_specs=[pl.BlockSpec((tm, D_MODEL), lambda i, te, nu: (i, 0)),
                      pl.BlockSpec((tm, 1), lambda i, te, nu: (i, 0)),
                      w_spec, b_spec, w_spec, b_spec, w_spec, b_spec],
            out_specs=pl.BlockSpec((tm, D_MODEL), lambda i, te, nu: (i, 0)),
        ),
        out_shape=jax.ShapeDtypeStruct((P, D_MODEL), F32),
        compiler_params=_cparams("arbitrary"),
        name="moe",
    )(tile_expert, n_used, xs, slot_gate, wg, bg, wu, bu, wd, bd)


def _route(top_i, top_g, tm):
    n = top_i.shape[0]
    a = n * TOP_K
    n_tiles = a // tm + N_EXPERTS
    flat_e = top_i.reshape(a)
    onehot = (flat_e[:, None] == jnp.arange(N_EXPERTS, dtype=jnp.int32)[None, :]).astype(jnp.int32)
    pos = jnp.sum((jnp.cumsum(onehot, axis=0) - 1) * onehot, axis=1)
    counts = jnp.sum(onehot, axis=0)
    tiles_e = (counts + tm - 1) // tm
    tile_end = jnp.cumsum(tiles_e)
    dest = (tile_end - tiles_e)[flat_e] * tm + pos
    n_used = tile_end[-1:].astype(jnp.int32)
    tile_expert = jnp.minimum(jnp.searchsorted(tile_end, jnp.arange(n_tiles, dtype=jnp.int32), side="right"),
                              N_EXPERTS - 1).astype(jnp.int32)
    src_tok = jnp.zeros((n_tiles * tm,), jnp.int32).at[dest].set(jnp.arange(a, dtype=jnp.int32) // TOP_K)
    slot_gate = jnp.zeros((n_tiles * tm,), F32).at[dest].set(top_g.reshape(a))
    return tile_expert, n_used, src_tok, slot_gate[:, None], dest.reshape(n, TOP_K)


def _mixer_inputs(x2d, w, tm):
    return _in_proj(x2d, w["gmix"], w["w_in_p"], w["wa2_p"], w["ba"], w["gsq"], w["gsk"], w["bd"], tm)


def kernel(x_prompt, x_sample, mem_prompt, state_gla, cache_swa_k, cache_swa_v, cache_mem_k, cache_mem_v, g_mix, w_in, w_gla_a2, b_gla_a, g_gla_out, g_swa_q, g_swa_k, w_out, g_mem, w_mk, w_mv, g_xk, g_xattn, w_xq, g_xq, w_xo, g_ffn, w_router, b_router, w_gate, b_gate, w_up, b_up, w_down, b_down):
    B, T, D = x_prompt.shape
    Bs, Ts, _ = x_sample.shape
    P = cache_swa_k.shape[2]
    l = 0

    wi = w_in[l]
    segs = np.cumsum((0, GLA_QK_W, GLA_QK_W, GLA_V_W, GLA_V_W, GLA_RANK, SWA_W, SWA_W, SWA_W))
    gq_c, gk_c, gv_c, gr_c, ga_c, sq_c, sk_c, sv_c = [wi[:, segs[j]:segs[j + 1]] for j in range(8)]
    w_in_p = jnp.concatenate(
        [gq_c, gk_c, gv_c, gr_c, sq_c, sk_c, sv_c, ga_c, jnp.zeros((D, D_IN_PAD - _C_GA - GLA_RANK), F32)],
        axis=1).astype(BF16)
    heads = np.arange(SWA_W) // SWA_HD
    w = dict(
        gmix=g_mix[l][None], w_in_p=w_in_p,
        wa2_p=jnp.concatenate([w_gla_a2[l], jnp.zeros((LANES - GLA_RANK, GLA_QK_W), F32)], axis=0).astype(BF16),
        ba=b_gla_a[l][None],
        gsq=jnp.tile(g_swa_q[l], SWA_HEADS)[None], gsk=jnp.tile(g_swa_k[l], SWA_HEADS)[None],
        bd=jnp.asarray(heads[:, None] == heads[None, :], BF16),
    )
    gout = g_gla_out[l][None]
    wout = w_out[l].astype(BF16)
    wxq, wxo = w_xq[l].astype(BF16), w_xo[l].astype(BF16)
    wr = jnp.concatenate([w_router[l], jnp.zeros((D, LANES - N_EXPERTS), F32)], axis=1).astype(BF16)
    br = jnp.concatenate([b_router[l], jnp.full((LANES - N_EXPERTS,), NEG, F32)])[None]
    post_w = (wout, g_xattn[l][None], wxq, g_xq[l][None])
    post_w2 = (wxo, g_ffn[l][None], wr, br)

    gq, gk, gv, gr, la, sq, sk, sv = _mixer_inputs(x_prompt.reshape(B * T, D), w, 512)
    r3 = lambda a: a.reshape(B, T, a.shape[-1])
    og_p, st_p = _gla(r3(gq), r3(gk), r3(gv), r3(la), r3(gr),
                      jnp.zeros((B, GLA_HEADS, GLA_DV, GLA_DK), F32), gout, 512)
    o_pat, l_pat = [], []
    for _, dil in DILATED_PATTERNS:
        o_c, l_c = _band_attn(_to_classes(r3(sq), dil), _to_classes(r3(sk), dil), _to_classes(r3(sv), dil), dil)
        o_pat.append(_from_classes(o_c, dil, B).reshape(B * T, SWA_W))
        l_pat.append(_from_classes(l_c, dil, B).reshape(B * T, SWA_W))
    sk_p, sv_p = sk, sv
    mk_p, mv_p = _mem_kv(mem_prompt.reshape(B * MEM_LEN, D), g_mem[l][None], w_mk[l].astype(BF16),
                         w_mv[l].astype(BF16), g_xk[l][None])
    x2_p, h3_p, ti_p, tg_p = _post(x_prompt.reshape(B * T, D), og_p.reshape(B * T, GLA_V_W), o_pat + l_pat,
                                   *post_w, mk_p.reshape(B, MEM_LEN, X_W), mv_p.reshape(B, MEM_LEN, X_W),
                                   *post_w2, tm=256, n_seg=1, rows_per_mem=T)

    gq, gk, gv, gr, la, sq, sk, sv = _mixer_inputs(x_sample.reshape(Bs * Ts, D), w, Bs * Ts)
    pad = lambda a: jnp.pad(a.reshape(Bs, Ts, a.shape[-1]), ((0, 0), (0, GLA_CHUNK - Ts), (0, 0)))
    og_s, st_s = _gla(pad(gq), pad(gk), pad(gv), pad(la), pad(gr),
                      jnp.swapaxes(state_gla[l], -1, -2), gout, GLA_CHUNK)
    og_s = og_s[:, :Ts].reshape(Bs * Ts, GLA_V_W)
    s3 = lambda a: a.reshape(Bs, Ts, SWA_W)
    o_swa_s, kb_s, vb_s = _samp_attn(s3(sq), s3(sk), s3(sv), cache_swa_k[l].reshape(Bs, P, SWA_W),
                                     cache_swa_v[l].reshape(Bs, P, SWA_W))
    seqs = 8
    x2_s, h3_s, ti_s, tg_s = _post(x_sample.reshape(Bs * Ts, D), og_s, [o_swa_s.reshape(Bs * Ts, SWA_W)],
                                   *post_w, cache_mem_k[l].reshape(Bs, MEM_LEN, X_W),
                                   cache_mem_v[l].reshape(Bs, MEM_LEN, X_W),
                                   *post_w2, tm=seqs * Ts, n_seg=seqs, rows_per_mem=Ts)

    n_p = B * T
    h3 = jnp.concatenate([h3_p, h3_s], axis=0)
    top_i = jnp.concatenate([ti_p[:, :TOP_K], ti_s[:, :TOP_K]], axis=0)
    top_g = jnp.concatenate([tg_p[:, :TOP_K], tg_s[:, :TOP_K]], axis=0)
    tile_expert, n_used, src_tok, slot_gate, dest = _route(top_i, top_g, MOE_TM)
    ys = _moe(tile_expert, n_used, h3[src_tok], slot_gate,
              w_gate[l].astype(BF16), b_gate[l][:, None, :], w_up[l].astype(BF16), b_up[l][:, None, :],
              w_down[l].astype(BF16), b_down[l][:, None, :])
    y = jnp.concatenate([x2_p, x2_s], axis=0) + jnp.sum(ys[dest], axis=1)

    w_buf = min(SWA_WINDOW, T)
    kv_p = lambda a: a.reshape(B, T, SWA_HEADS, SWA_HD)[:, T - w_buf:][None]
    return (y[:n_p].reshape(B, T, D), y[n_p:].reshape(Bs, Ts, D),
            jnp.swapaxes(st_p, -1, -2)[None], kv_p(sk_p), kv_p(sv_p),
            mk_p.reshape(1, B, MEM_LEN, X_HEADS, X_HD), mv_p.reshape(1, B, MEM_LEN, X_HEADS, X_HD),
            jnp.swapaxes(st_s, -1, -2)[None],
            kb_s.reshape(1, Bs, P, SWA_HEADS, SWA_HD), vb_s.reshape(1, Bs, P, SWA_HEADS, SWA_HD))
```

```python
import functools

import numpy as np
import jax
import jax.numpy as jnp
from jax import lax
from jax.experimental import pallas as pl
from jax.experimental.pallas import tpu as pltpu

F32 = jnp.float32
BF16 = jnp.bfloat16

D_MODEL = 1024
GLA_HEADS = 4
GLA_DK = 64
GLA_DV = 128
GLA_RANK = 16
GLA_TAU = 16.0
GLA_CHUNK = 64
GLA_SUB = 16
SWA_HEADS = 8
SWA_HD = 64
DILATED_PATTERNS = ((128, 1), (512, 4), (2048, 16))
BAND = 128
SWA_WINDOW = 2048
MEM_LEN = 256
X_HEADS = 4
X_HD = 128
N_EXPERTS = 32
TOP_K = 4
SWIGLU_ALPHA = 1.702
SWIGLU_LIMIT = 7.0
EPS = 1e-6

GLA_QK_W = GLA_HEADS * GLA_DK
GLA_V_W = GLA_HEADS * GLA_DV
SWA_W = SWA_HEADS * SWA_HD
X_W = X_HEADS * X_HD
LANES = 128
NEG = -1e30
VMEM_LIMIT = 56 * 1024 * 1024

_C_GQ, _C_GK, _C_GV, _C_GR, _C_SQ, _C_SK, _C_SV, _C_GA = 0, 256, 512, 1024, 1536, 2048, 2560, 3072
D_IN_PAD = 3200

MOE_TM = 256


def _cparams(*sem):
    return pltpu.CompilerParams(dimension_semantics=sem, vmem_limit_bytes=VMEM_LIMIT)


def _dot(a, b):
    return jnp.dot(a, b, preferred_element_type=F32)


def _dot_nt(a, b):
    return lax.dot_general(a, b, (((1,), (1,)), ((), ())), preferred_element_type=F32)


def _rms_rows(x, g):
    return x * lax.rsqrt(jnp.mean(x * x, axis=-1, keepdims=True) + EPS) * g


def _split_bf16(x, n):
    out = []
    for _ in range(n - 1):
        hi = x.astype(BF16)
        out.append(hi)
        x = x - hi.astype(F32)
    out.append(x.astype(BF16))
    return out


def _group_rms(z, g, ones_bd, group):
    hi, lo = _split_bf16(z * z, 2)
    ss = _dot(hi, ones_bd) + _dot(lo, ones_bd)
    return z * lax.rsqrt(ss * (1.0 / group) + EPS) * g


def _log_sigmoid(x):
    return jnp.minimum(x, 0.0) - jnp.log1p(jnp.exp(-jnp.abs(x)))


def _sigmoid(x):
    return 1.0 / (1.0 + jnp.exp(-x))


def _in_proj_kernel(x_ref, gmix_ref, w_ref, wa2_ref, ba_ref, gsq_ref, gsk_ref, bd_ref,
                    oq_ref, ok_ref, ov_ref, or_ref, ola_ref, osq_ref, osk_ref, osv_ref):
    h = _rms_rows(x_ref[...], gmix_ref[...]).astype(BF16)

    def proj(lo, width):
        return _dot(h, w_ref[:, lo:lo + width])

    oq_ref[...] = proj(_C_GQ, GLA_QK_W) * (GLA_DK ** -0.5)
    ok_ref[...] = proj(_C_GK, GLA_QK_W)
    ov_ref[...] = proj(_C_GV, GLA_V_W)
    or_ref[...] = proj(_C_GR, GLA_V_W)
    osv_ref[...] = proj(_C_SV, SWA_W)
    ga = proj(_C_GA, LANES)
    xa = _dot(ga.astype(BF16), wa2_ref[...]) + ba_ref[...]
    ola_ref[...] = _log_sigmoid(xa) * (1.0 / GLA_TAU)
    bd = bd_ref[...]
    osq_ref[...] = _group_rms(proj(_C_SQ, SWA_W), gsq_ref[...], bd, SWA_HD)
    osk_ref[...] = _group_rms(proj(_C_SK, SWA_W), gsk_ref[...], bd, SWA_HD)


def _in_proj(x, gmix, w_in_p, wa2_p, ba, gsq, gsk, bd, tm):
    n = x.shape[0]
    row = lambda w: pl.BlockSpec((tm, w), lambda i: (i, 0))
    full = lambda a: pl.BlockSpec(a.shape, lambda i: (0,) * a.ndim)
    widths = (GLA_QK_W, GLA_QK_W, GLA_V_W, GLA_V_W, GLA_QK_W, SWA_W, SWA_W, SWA_W)
    return pl.pallas_call(
        _in_proj_kernel,
        grid=(n // tm,),
        in_specs=[row(D_MODEL), full(gmix), full(w_in_p), full(wa2_p), full(ba), full(gsq), full(gsk), full(bd)],
        out_specs=[row(w) for w in widths],
        out_shape=[jax.ShapeDtypeStruct((n, w), F32) for w in widths],
        compiler_params=_cparams("parallel"),
        name="in_proj",
    )(x, gmix, w_in_p, wa2_p, ba, gsq, gsk, bd)


def _gla_kernel(q_ref, k_ref, v_ref, la_ref, r_ref, s0t_ref, gout_ref, lcat_ref, qmask_ref,
                o_ref, st_ref, st_scr, *, n_chunks):
    tb = pl.program_id(1)
    C, S, NS = GLA_CHUNK, GLA_SUB, GLA_CHUNK // GLA_SUB

    @pl.when(tb == 0)
    def _():
        st_scr[...] = s0t_ref[...]

    lcat = lcat_ref[...]
    qmask = qmask_ref[...]
    gout = gout_ref[...]
    row = lax.broadcasted_iota(jnp.int32, (C, GLA_DK), 0)
    tril = lax.broadcasted_iota(jnp.int32, (C, C), 0) >= lax.broadcasted_iota(jnp.int32, (C, C), 1)

    def chunk(c, carry):
        r0 = pl.multiple_of(c * C, C)
        rows = pl.ds(r0, C)
        g3 = _split_bf16(la_ref[rows, :], 3)
        b_r = _dot(lcat, g3[0]) + _dot(lcat, g3[1]) + _dot(lcat, g3[2])
        b = b_r[:C]
        ref = b_r[C:]
        b_end = b[C - 1:C, :]
        q = q_ref[rows, :]
        k = k_ref[rows, :]
        q_hat = q * jnp.exp(b - ref)
        q_til = q * jnp.exp(b)
        k_dec = k * jnp.exp(b_end - b)
        a_end = jnp.exp(b_end)
        for h in range(GLA_HEADS):
            sl = slice(h * GLA_DK, (h + 1) * GLA_DK)
            vl = slice(h * GLA_DV, (h + 1) * GLA_DV)
            b_h, ref_h, k_h = b[:, sl], ref[:, sl], k[:, sl]
            pieces = []
            for i in range(NS):
                e = ref_h[S * i:S * i + 1, :] - b_h
                e = jnp.where(row < S * (i + 1), e, NEG)
                pieces.append(k_h * jnp.exp(e))
            k_big = jnp.concatenate(pieces, axis=1).astype(BF16)
            q_big = (jnp.concatenate([q_hat[:, sl]] * NS, axis=1) * qmask).astype(BF16)
            sc = jnp.where(tril, _dot_nt(q_big, k_big), 0.0)
            v_h = v_ref[rows, vl]
            st = st_scr[h]
            o = _dot(sc.astype(BF16), v_h.astype(BF16)) + _dot_nt(q_til[:, sl].astype(BF16), st.astype(BF16))
            st_scr[h] = st * a_end[:, sl] + _dot(v_h.T.astype(BF16), k_dec[:, sl].astype(BF16))
            gate = r_ref[rows, vl]
            o_ref[rows, vl] = (_rms_rows(o, gout) * (gate * _sigmoid(gate))).astype(o_ref.dtype)
        return carry

    lax.fori_loop(0, n_chunks, chunk, 0)

    @pl.when(tb == pl.num_programs(1) - 1)
    def _():
        st_ref[...] = st_scr[...]


def _gla_consts():
    C, S = GLA_CHUNK, GLA_SUB
    t = np.arange(C)
    incl = (t[None, :] <= t[:, None]).astype(np.float32)
    upto = (t[None, :] < (t[:, None] // S) * S).astype(np.float32)
    lcat = np.concatenate([incl, upto], axis=0)
    col = np.arange((C // S) * GLA_DK)
    qmask = ((t[:, None] // S) == (col[None, :] // GLA_DK)).astype(np.float32)
    return jnp.asarray(lcat, BF16), jnp.asarray(qmask, F32)


def _gla(q, k, v, la, r, s0t, gout, tb):
    B, T, _ = q.shape
    lcat, qmask = _gla_consts()
    blk = lambda w: pl.BlockSpec((None, tb, w), lambda b, t: (b, t, 0))
    full = lambda a: pl.BlockSpec(a.shape, lambda b, t: (0,) * a.ndim)
    st_spec = pl.BlockSpec((None, GLA_HEADS, GLA_DV, GLA_DK), lambda b, t: (b, 0, 0, 0))
    return pl.pallas_call(
        functools.partial(_gla_kernel, n_chunks=tb // GLA_CHUNK),
        grid=(B, T // tb),
        in_specs=[blk(GLA_QK_W), blk(GLA_QK_W), blk(GLA_V_W), blk(GLA_QK_W), blk(GLA_V_W), st_spec,
                  full(gout), full(lcat), full(qmask)],
        out_specs=[blk(GLA_V_W), st_spec],
        out_shape=[jax.ShapeDtypeStruct((B, T, GLA_V_W), BF16),
                   jax.ShapeDtypeStruct((B, GLA_HEADS, GLA_DV, GLA_DK), F32)],
        scratch_shapes=[pltpu.VMEM((GLA_HEADS, GLA_DV, GLA_DK), F32)],
        compiler_params=_cparams("parallel", "arbitrary"),
        name="gla",
    )(q, k, v, la, r, s0t, gout, lcat, qmask)


def _band_attn_kernel(q_ref, kp_ref, kc_ref, vp_ref, vc_ref, o_ref, lse_ref, k_scr, v_scr, *, dil, n_sub):
    i = pl.program_id(2)
    k_scr[0:BAND, :] = kp_ref[...].astype(BF16)
    k_scr[BAND:, :] = kc_ref[...].astype(BF16)
    v_scr[0:BAND, :] = vp_ref[...].astype(BF16)
    v_scr[BAND:, :] = vc_ref[...].astype(BF16)
    t = lax.broadcasted_iota(jnp.int32, (BAND, 2 * BAND), 0)
    c = lax.broadcasted_iota(jnp.int32, (BAND, 2 * BAND), 1)
    dist = BAND + t - c
    in_band = (dist >= 0) & (dist <= BAND)
    distf = (dist * dil).astype(F32)

    def sub_block(j, carry):
        r0 = pl.multiple_of(j * BAND, BAND)
        rows = pl.ds(r0, BAND)
        valid = in_band & ((c >= BAND) | (i > 0) | (j > 0))
        q = (q_ref[rows, :] * (SWA_HD ** -0.5)).astype(BF16)
        k = k_scr[pl.ds(r0, 2 * BAND), :]
        v = v_scr[pl.ds(r0, 2 * BAND), :]
        outs, lses = [], []
        for h in range(SWA_HEADS):
            sl = slice(h * SWA_HD, (h + 1) * SWA_HD)
            s = _dot_nt(q[:, sl], k[:, sl]) - (2.0 ** -(h + 1)) * distf
            s = jnp.where(valid, s, NEG)
            m = jnp.max(s, axis=-1, keepdims=True)
            p = jnp.exp(s - m)
            den = jnp.sum(p, axis=-1, keepdims=True)
            outs.append(_dot(p.astype(BF16), v[:, sl]) / den)
            lses.append(jnp.broadcast_to(m + jnp.log(den), (BAND, SWA_HD)))
        o_ref[rows, :] = jnp.concatenate(outs, axis=1)
        lse_ref[rows, :] = jnp.concatenate(lses, axis=1)
        return carry

    lax.fori_loop(0, n_sub, sub_block, 0)


def _band_attn(q, k, v, dil):
    B, T, W = q.shape
    Tc = T // dil
    qb = min(4 * BAND, Tc)
    n_sub = qb // BAND
    view = lambda a: a.reshape(B, Tc, dil * W)
    cur = pl.BlockSpec((None, qb, W), lambda b, r, i: (b, i, r))
    prev = pl.BlockSpec((None, BAND, W), lambda b, r, i: (b, jnp.maximum(i * n_sub - 1, 0), r))
    o, lse = pl.pallas_call(
        functools.partial(_band_attn_kernel, dil=dil, n_sub=n_sub),
        grid=(B, dil, Tc // qb),
        in_specs=[cur, prev, cur, prev, cur],
        out_specs=[cur, cur],
        out_shape=[jax.ShapeDtypeStruct((B, Tc, dil * W), F32)] * 2,
        scratch_shapes=[pltpu.VMEM((BAND + qb, W), BF16)] * 2,
        compiler_params=_cparams("parallel", "parallel", "parallel"),
        name="band_attn_d%d" % dil,
    )(view(q), view(k), view(k), view(v), view(v))
    return o.reshape(B * T, W), lse.reshape(B * T, W)


def _pattern_count(dist):
    cnt = jnp.zeros(dist.shape, F32)
    for window, dil in DILATED_PATTERNS:
        hit = (dist >= 0) & (dist <= window)
        if dil > 1:
            hit = hit & ((dist & (dil - 1)) == 0)
        cnt = cnt + hit.astype(F32)
    return cnt


def _samp_attn_kernel(q_ref, kn_ref, vn_ref, kp_ref, vp_ref, o_ref, ko_ref, vo_ref):
    P, Tn = kp_ref.shape[0], kn_ref.shape[0]
    ko_ref[0:P - Tn, :] = kp_ref[Tn:P, :]
    ko_ref[P - Tn:P, :] = kn_ref[...]
    vo_ref[0:P - Tn, :] = vp_ref[Tn:P, :]
    vo_ref[P - Tn:P, :] = vn_ref[...]

    qi = lax.broadcasted_iota(jnp.int32, (Tn, P), 0)
    dist_p = P + qi - lax.broadcasted_iota(jnp.int32, (Tn, P), 1)
    dist_n = lax.broadcasted_iota(jnp.int32, (Tn, Tn), 0) - lax.broadcasted_iota(jnp.int32, (Tn, Tn), 1)
    cnt_p, cnt_n = _pattern_count(dist_p), _pattern_count(dist_n)
    dpf, dnf = dist_p.astype(F32), dist_n.astype(F32)
    q = q_ref[...].astype(BF16)
    kp, vp = kp_ref[...].astype(BF16), vp_ref[...].astype(BF16)
    kn, vn = kn_ref[...].astype(BF16), vn_ref[...].astype(BF16)
    outs = []
    for h in range(SWA_HEADS):
        sl = slice(h * SWA_HD, (h + 1) * SWA_HD)
        slope = 2.0 ** -(h + 1)
        lp = jnp.where(cnt_p > 0, _dot_nt(q[:, sl], kp[:, sl]) * (SWA_HD ** -0.5) - slope * dpf, NEG)
        ln = jnp.where(cnt_n > 0, _dot_nt(q[:, sl], kn[:, sl]) * (SWA_HD ** -0.5) - slope * dnf, NEG)
        m = jnp.maximum(jnp.max(lp, axis=-1, keepdims=True), jnp.max(ln, axis=-1, keepdims=True))
        pp = cnt_p * jnp.exp(lp - m)
        pn = cnt_n * jnp.exp(ln - m)
        den = jnp.sum(pp, axis=-1, keepdims=True) + jnp.sum(pn, axis=-1, keepdims=True)
        outs.append((_dot(pp.astype(BF16), vp[:, sl]) + _dot(pn.astype(BF16), vn[:, sl])) / den)
    o_ref[...] = jnp.concatenate(outs, axis=1)


def _samp_attn(q, kn, vn, kp, vp):
    B, Tn, W = q.shape
    P = kp.shape[1]
    new = pl.BlockSpec((None, Tn, W), lambda b: (b, 0, 0))
    past = pl.BlockSpec((None, P, W), lambda b: (b, 0, 0))
    return pl.pallas_call(
        _samp_attn_kernel,
        grid=(B,),
        in_specs=[new, new, new, past, past],
        out_specs=[new, past, past],
        out_shape=[jax.ShapeDtypeStruct((B, Tn, W), F32), jax.ShapeDtypeStruct((B, P, W), F32),
                   jax.ShapeDtypeStruct((B, P, W), F32)],
        compiler_params=_cparams("parallel"),
        name="samp_attn",
    )(q, kn, vn, kp, vp)


def _mem_kv_kernel(mem_ref, gmem_ref, wk_ref, wv_ref, gxk_ref, mk_ref, mv_ref):
    hm = _rms_rows(mem_ref[...], gmem_ref[...]).astype(BF16)
    mk = _dot(hm, wk_ref[...])
    gxk = gxk_ref[...]
    for h in range(X_HEADS):
        sl = slice(h * X_HD, (h + 1) * X_HD)
        mk_ref[:, sl] = _rms_rows(mk[:, sl], gxk)
    mv_ref[...] = _dot(hm, wv_ref[...])


def _mem_kv(mem, gmem, wk, wv, gxk):
    n = mem.shape[0]
    return pl.pallas_call(
        _mem_kv_kernel,
        out_shape=[jax.ShapeDtypeStruct((n, X_W), F32)] * 2,
        compiler_params=pltpu.CompilerParams(vmem_limit_bytes=VMEM_LIMIT),
        name="mem_kv",
    )(mem, gmem, wk, wv, gxk)


def _post_kernel(*refs, n_pat, n_seg):
    n_swa = 2 * n_pat if n_pat > 1 else 1
    x_ref, og_ref = refs[0], refs[1]
    swa_refs = refs[2:2 + n_swa]
    (wout_ref, gx_ref, wxq_ref, gxq_ref, mk_ref, mv_ref, wxo_ref, gffn_ref, wr_ref, br_ref, tri_ref, cnt0_ref,
     x2_ref, h3_ref, ti_ref, tg_ref, cnt_ref, ox_scr, run_scr) = refs[2 + n_swa:]

    @pl.when(pl.program_id(0) == 0)
    def _():
        run_scr[...] = cnt0_ref[...]

    if n_pat > 1:
        o_p = [r[...] for r in swa_refs[:n_pat]]
        l_p = [r[...] for r in swa_refs[n_pat:]]
        lmax = functools.reduce(jnp.maximum, l_p)
        w_p = [jnp.exp(l - lmax) for l in l_p]
        o_swa = sum(w * o for w, o in zip(w_p, o_p)) / sum(w_p)
    else:
        o_swa = swa_refs[0][...]

    x1 = x_ref[...] + _dot(og_ref[...], wout_ref[0:GLA_V_W, :]) + _dot(o_swa.astype(BF16), wout_ref[GLA_V_W:, :])

    q = _dot(_rms_rows(x1, gx_ref[...]).astype(BF16), wxq_ref[...])
    gxq = gxq_ref[...]
    seg = x1.shape[0] // n_seg
    for h in range(X_HEADS):
        sl = slice(h * X_HD, (h + 1) * X_HD)
        qn = _rms_rows(q[:, sl], gxq).astype(BF16)
        for s in range(n_seg):
            rows = slice(s * seg, (s + 1) * seg)
            sc = _dot_nt(qn[rows], mk_ref[s, :, sl].astype(BF16)) * (X_HD ** -0.5)
            p = jnp.exp(sc - jnp.max(sc, axis=-1, keepdims=True))
            p = p / jnp.sum(p, axis=-1, keepdims=True)
            ox_scr[rows, sl] = _dot(p.astype(BF16), mv_ref[s, :, sl].astype(BF16))
    x2 = x1 + _dot(ox_scr[...].astype(BF16), wxo_ref[...])
    x2_ref[...] = x2

    h3 = _rms_rows(x2, gffn_ref[...]).astype(BF16)
    h3_ref[...] = h3
    work = _dot(h3, wr_ref[...]) + br_ref[...]
    lane = lax.broadcasted_iota(jnp.int32, work.shape, 1)
    vals, idxs = [], []
    for _ in range(TOP_K):
        m = jnp.max(work, axis=-1, keepdims=True)
        idx = jnp.min(jnp.where(work == m, lane, LANES), axis=-1, keepdims=True)
        vals.append(m)
        idxs.append(idx)
        work = jnp.where(lane == idx, -jnp.inf, work)
    es = [jnp.exp(v - vals[0]) for v in vals]
    den = sum(es)
    sel = [lane == idx for idx in idxs]
    onehot = functools.reduce(jnp.logical_or, sel).astype(BF16)
    run = run_scr[...]
    rank = _dot(tri_ref[...], onehot) + run
    run_scr[...] = run + jnp.sum(onehot.astype(F32), axis=0, keepdims=True)
    cnt_ref[...] = run_scr[...]
    ti = jnp.zeros(work.shape, jnp.int32)
    tg = jnp.zeros(work.shape, F32)
    for j in range(TOP_K):
        pos = jnp.sum(jnp.where(sel[j], rank, 0.0), axis=-1, keepdims=True).astype(jnp.int32)
        ti = jnp.where(lane == j, idxs[j], ti)
        ti = jnp.where(lane == TOP_K + j, pos, ti)
        tg = jnp.where(lane == j, es[j] / den, tg)
    ti_ref[...] = ti
    tg_ref[...] = tg


def _post(x, og, swa, wout, gx, wxq, gxq, mk, mv, wxo, gffn, wr, br, cnt0, tm, n_seg, rows_per_mem):
    n = x.shape[0]
    n_pat = len(swa) // 2 if len(swa) > 1 else 1
    tri = jnp.asarray(np.tril(np.ones((tm, tm), np.float32), -1), BF16)
    row = lambda w: pl.BlockSpec((tm, w), lambda i: (i, 0))
    full = lambda a: pl.BlockSpec(a.shape, lambda i: (0,) * a.ndim)
    mem = pl.BlockSpec((n_seg, MEM_LEN, X_W), lambda i: ((i * tm) // (rows_per_mem * n_seg), 0, 0))
    return pl.pallas_call(
        functools.partial(_post_kernel, n_pat=n_pat, n_seg=n_seg),
        grid=(n // tm,),
        in_specs=[row(D_MODEL), row(GLA_V_W)] + [row(SWA_W)] * len(swa)
        + [full(wout), full(gx), full(wxq), full(gxq), mem, mem, full(wxo), full(gffn), full(wr), full(br),
           full(tri), full(cnt0)],
        out_specs=[row(D_MODEL), row(D_MODEL), row(LANES), row(LANES), full(cnt0)],
        out_shape=[jax.ShapeDtypeStruct((n, D_MODEL), F32), jax.ShapeDtypeStruct((n, D_MODEL), BF16),
                   jax.ShapeDtypeStruct((n, LANES), jnp.int32), jax.ShapeDtypeStruct((n, LANES), F32),
                   jax.ShapeDtypeStruct((1, LANES), F32)],
        scratch_shapes=[pltpu.VMEM((tm, X_W), F32), pltpu.VMEM((1, LANES), F32)],
        compiler_params=_cparams("arbitrary"),
        name="post",
    )(x, og, *swa, wout, gx, wxq, gxq, mk, mv, wxo, gffn, wr, br, tri, cnt0)


def _moe_kernel(te_ref, nu_ref, x_ref, wg_ref, bg_ref, wu_ref, bu_ref, wd_ref, bd_ref, y_ref, wg_scr, wu_scr, wd_scr):
    i = pl.program_id(0)
    live = i < nu_ref[0]

    @pl.when(live & ((i == 0) | (te_ref[i] != te_ref[jnp.maximum(i - 1, 0)])))
    def _():
        wg_scr[...] = wg_ref[...].astype(BF16)
        wu_scr[...] = wu_ref[...].astype(BF16)
        wd_scr[...] = wd_ref[...].astype(BF16)

    @pl.when(live)
    def _():
        x = x_ref[...]
        g = jnp.minimum(_dot(x, wg_scr[...]) + bg_ref[...], SWIGLU_LIMIT)
        u = jnp.clip(_dot(x, wu_scr[...]) + bu_ref[...], -SWIGLU_LIMIT, SWIGLU_LIMIT)
        a = g * _sigmoid(SWIGLU_ALPHA * g) * (u + 1.0)
        y_ref[...] = (_dot(a.astype(BF16), wd_scr[...]) + bd_ref[...]).astype(y_ref.dtype)

    @pl.when(jnp.logical_not(live))
    def _():
        y_ref[...] = jnp.zeros(y_ref.shape, y_ref.dtype)


def _moe(tile_expert, n_used, xs, wg, bg, wu, bu, wd, bd):
    P = xs.shape[0]
    tm = MOE_TM
    w_spec = pl.BlockSpec((None, D_MODEL, D_MODEL), lambda i, te, nu: (te[i], 0, 0))
    b_spec = pl.BlockSpec((None, 1, D_MODEL), lambda i, te, nu: (te[i], 0, 0))
    return pl.pallas_call(
        _moe_kernel,
        grid_spec=pltpu.PrefetchScalarGridSpec(
            num_scalar_prefetch=2,
            grid=(P // tm,),
            in_specs=[pl.BlockSpec((tm, D_MODEL), lambda i, te, nu: (i, 0)),
                      w_spec, b_spec, w_spec, b_spec, w_spec, b_spec],
            out_specs=pl.BlockSpec((tm, D_MODEL), lambda i, te, nu: (i, 0)),
            scratch_shapes=[pltpu.VMEM((D_MODEL, D_MODEL), BF16)] * 3,
        ),
        out_shape=jax.ShapeDtypeStruct((P, D_MODEL), BF16),
        compiler_params=_cparams("arbitrary"),
        name="moe",
    )(tile_expert, n_used, xs, wg, bg, wu, bu, wd, bd)


def _route(top_i, rank, counts, tm):
    n = top_i.shape[0]
    a = n * TOP_K
    n_tiles = a // tm + N_EXPERTS
    tiles_e = (counts + tm - 1) // tm
    tile_end = jnp.cumsum(tiles_e)
    slot0 = (tile_end - tiles_e) * tm
    experts = jnp.arange(N_EXPERTS, dtype=jnp.int32)
    dest = rank + jnp.sum(jnp.where(top_i[..., None] == experts, slot0, 0), axis=-1)
    n_used = tile_end[-1:].astype(jnp.int32)
    tile_expert = jnp.minimum(
        jnp.sum((tile_end[None, :] <= jnp.arange(n_tiles, dtype=jnp.int32)[:, None]).astype(jnp.int32), axis=1),
        N_EXPERTS - 1)
    src_tok = jnp.zeros((n_tiles * tm,), jnp.int32).at[dest.reshape(a)].set(
        jnp.arange(a, dtype=jnp.int32) // TOP_K, unique_indices=True)
    return tile_expert, n_used, src_tok, dest


def _combine_kernel(x_ref, y_ref, g_ref, o_ref):
    acc = x_ref[...]
    g = g_ref[...]
    for j in range(TOP_K):
        acc = acc + g[:, j:j + 1] * y_ref[j].astype(F32)
    o_ref[...] = acc


def _combine(x2, ysg, tg, row0, tm):
    n = x2.shape[0]
    off = row0 // tm
    return pl.pallas_call(
        _combine_kernel,
        grid=(n // tm,),
        in_specs=[pl.BlockSpec((tm, D_MODEL), lambda i: (i, 0)),
                  pl.BlockSpec((TOP_K, tm, D_MODEL), lambda i: (0, i + off, 0)),
                  pl.BlockSpec((tm, LANES), lambda i: (i, 0))],
        out_specs=pl.BlockSpec((tm, D_MODEL), lambda i: (i, 0)),
        out_shape=jax.ShapeDtypeStruct((n, D_MODEL), F32),
        compiler_params=_cparams("parallel"),
        name="combine",
    )(x2, ysg, tg)


def _mixer_inputs(x2d, w, tm):
    return _in_proj(x2d, w["gmix"], w["w_in_p"], w["wa2_p"], w["ba"], w["gsq"], w["gsk"], w["bd"], tm)


def kernel(x_prompt, x_sample, mem_prompt, state_gla, cache_swa_k, cache_swa_v, cache_mem_k, cache_mem_v, g_mix, w_in, w_gla_a2, b_gla_a, g_gla_out, g_swa_q, g_swa_k, w_out, g_mem, w_mk, w_mv, g_xk, g_xattn, w_xq, g_xq, w_xo, g_ffn, w_router, b_router, w_gate, b_gate, w_up, b_up, w_down, b_down):
    B, T, D = x_prompt.shape
    Bs, Ts, _ = x_sample.shape
    P = cache_swa_k.shape[2]
    l = 0

    wi = w_in[l]
    segs = np.cumsum((0, GLA_QK_W, GLA_QK_W, GLA_V_W, GLA_V_W, GLA_RANK, SWA_W, SWA_W, SWA_W))
    gq_c, gk_c, gv_c, gr_c, ga_c, sq_c, sk_c, sv_c = [wi[:, segs[j]:segs[j + 1]] for j in range(8)]
    w_in_p = jnp.concatenate(
        [gq_c, gk_c, gv_c, gr_c, sq_c, sk_c, sv_c, ga_c, jnp.zeros((D, D_IN_PAD - _C_GA - GLA_RANK), F32)],
        axis=1).astype(BF16)
    heads = np.arange(SWA_W) // SWA_HD
    w = dict(
        gmix=g_mix[l][None], w_in_p=w_in_p,
        wa2_p=jnp.concatenate([w_gla_a2[l], jnp.zeros((LANES - GLA_RANK, GLA_QK_W), F32)], axis=0).astype(BF16),
        ba=b_gla_a[l][None],
        gsq=jnp.tile(g_swa_q[l], SWA_HEADS)[None], gsk=jnp.tile(g_swa_k[l], SWA_HEADS)[None],
        bd=jnp.asarray(heads[:, None] == heads[None, :], BF16),
    )
    gout = g_gla_out[l][None]
    wout = w_out[l].astype(BF16)
    wxq, wxo = w_xq[l].astype(BF16), w_xo[l].astype(BF16)
    wr = jnp.concatenate([w_router[l], jnp.zeros((D, LANES - N_EXPERTS), F32)], axis=1).astype(BF16)
    br = jnp.concatenate([b_router[l], jnp.full((LANES - N_EXPERTS,), NEG, F32)])[None]
    post_w = (wout, g_xattn[l][None], wxq, g_xq[l][None])
    post_w2 = (wxo, g_ffn[l][None], wr, br)

    gq, gk, gv, gr, la, sq, sk, sv = _mixer_inputs(x_prompt.reshape(B * T, D), w, 512)
    r3 = lambda a: a.reshape(B, T, a.shape[-1])
    og_p, st_p = _gla(r3(gq), r3(gk), r3(gv), r3(la), r3(gr),
                      jnp.zeros((B, GLA_HEADS, GLA_DV, GLA_DK), F32), gout, 512)
    o_pat, l_pat = [], []
    for _, dil in DILATED_PATTERNS:
        o_c, l_c = _band_attn(r3(sq), r3(sk), r3(sv), dil)
        o_pat.append(o_c)
        l_pat.append(l_c)
    sk_p, sv_p = sk, sv
    mk_p, mv_p = _mem_kv(mem_prompt.reshape(B * MEM_LEN, D), g_mem[l][None], w_mk[l].astype(BF16),
                         w_mv[l].astype(BF16), g_xk[l][None])
    x2_p, h3_p, ti_p, tg_p, cnt_p = _post(x_prompt.reshape(B * T, D), og_p.reshape(B * T, GLA_V_W), o_pat + l_pat,
                                          *post_w, mk_p.reshape(B, MEM_LEN, X_W), mv_p.reshape(B, MEM_LEN, X_W),
                                          *post_w2, jnp.zeros((1, LANES), F32), tm=256, n_seg=1, rows_per_mem=T)

    gq, gk, gv, gr, la, sq, sk, sv = _mixer_inputs(x_sample.reshape(Bs * Ts, D), w, Bs * Ts)
    pad = lambda a: jnp.pad(a.reshape(Bs, Ts, a.shape[-1]), ((0, 0), (0, GLA_CHUNK - Ts), (0, 0)))
    og_s, st_s = _gla(pad(gq), pad(gk), pad(gv), pad(la), pad(gr),
                      jnp.swapaxes(state_gla[l], -1, -2), gout, GLA_CHUNK)
    og_s = og_s[:, :Ts].reshape(Bs * Ts, GLA_V_W)
    s3 = lambda a: a.reshape(Bs, Ts, SWA_W)
    o_swa_s, kb_s, vb_s = _samp_attn(s3(sq), s3(sk), s3(sv), cache_swa_k[l].reshape(Bs, P, SWA_W),
                                     cache_swa_v[l].reshape(Bs, P, SWA_W))
    seqs = 8
    x2_s, h3_s, ti_s, tg_s, cnt = _post(x_sample.reshape(Bs * Ts, D), og_s, [o_swa_s.reshape(Bs * Ts, SWA_W)],
                                        *post_w, cache_mem_k[l].reshape(Bs, MEM_LEN, X_W),
                                        cache_mem_v[l].reshape(Bs, MEM_LEN, X_W),
                                        *post_w2, cnt_p, tm=seqs * Ts, n_seg=seqs, rows_per_mem=Ts)

    n_p = B * T
    h3 = jnp.concatenate([h3_p, h3_s], axis=0)
    ti = jnp.concatenate([ti_p[:, :2 * TOP_K], ti_s[:, :2 * TOP_K]], axis=0)
    tile_expert, n_used, src_tok, dest = _route(ti[:, :TOP_K], ti[:, TOP_K:], cnt[0, :N_EXPERTS].astype(jnp.int32),
                                                MOE_TM)
    ys = _moe(tile_expert, n_used, h3[src_tok], w_gate[l], b_gate[l][:, None, :], w_up[l], b_up[l][:, None, :],
              w_down[l], b_down[l][:, None, :])
    ysg = ys[dest.T.reshape(-1)].reshape(TOP_K, n_p + Bs * Ts, D)
    y_p = _combine(x2_p, ysg, tg_p, 0, 256)
    y_s = _combine(x2_s, ysg, tg_s, n_p, Bs * Ts)

    w_buf = min(SWA_WINDOW, T)
    kv_p = lambda a: a.reshape(B, T, SWA_HEADS, SWA_HD)[:, T - w_buf:][None]
    return (y_p.reshape(B, T, D), y_s.reshape(Bs, Ts, D),
            jnp.swapaxes(st_p, -1, -2)[None], kv_p(sk_p), kv_p(sv_p),
            mk_p.reshape(1, B, MEM_LEN, X_HEADS, X_HD), mv_p.reshape(1, B, MEM_LEN, X_HEADS, X_HD),
            jnp.swapaxes(st_s, -1, -2)[None],
            kb_s.reshape(1, Bs, P, SWA_HEADS, SWA_HD), vb_s.reshape(1, Bs, P, SWA_HEADS, SWA_HD))
```

```python
import functools

import numpy as np
import jax
import jax.numpy as jnp
from jax import lax
from jax.experimental import pallas as pl
from jax.experimental.compute_on import compute_on
from jax.experimental.pallas import tpu as pltpu

F32 = jnp.float32
BF16 = jnp.bfloat16

D_MODEL = 1024
GLA_HEADS = 4
GLA_DK = 64
GLA_DV = 128
GLA_RANK = 16
GLA_TAU = 16.0
GLA_CHUNK = 64
GLA_SUB = 16
SWA_HEADS = 8
SWA_HD = 64
DILATED_PATTERNS = ((128, 1), (512, 4), (2048, 16))
BAND = 128
SWA_WINDOW = 2048
MEM_LEN = 256
X_HEADS = 4
X_HD = 128
N_EXPERTS = 32
TOP_K = 4
SWIGLU_ALPHA = 1.702
SWIGLU_LIMIT = 7.0
EPS = 1e-6

GLA_QK_W = GLA_HEADS * GLA_DK
GLA_V_W = GLA_HEADS * GLA_DV
SWA_W = SWA_HEADS * SWA_HD
X_W = X_HEADS * X_HD
LANES = 128
NEG = -1e30
VMEM_LIMIT = 56 * 1024 * 1024

_C_GQ, _C_GK, _C_GV, _C_GR, _C_SQ, _C_SK, _C_SV, _C_GA = 0, 256, 512, 1024, 1536, 2048, 2560, 3072
D_IN_PAD = 3200

MOE_TM = 256


def _cparams(*sem):
    return pltpu.CompilerParams(dimension_semantics=sem, vmem_limit_bytes=VMEM_LIMIT)


def _dot(a, b):
    return jnp.dot(a, b, preferred_element_type=F32)


def _dot_nt(a, b):
    return lax.dot_general(a, b, (((1,), (1,)), ((), ())), preferred_element_type=F32)


def _rms_rows(x, g):
    return x * lax.rsqrt(jnp.mean(x * x, axis=-1, keepdims=True) + EPS) * g


def _split_bf16(x, n):
    out = []
    for _ in range(n - 1):
        hi = x.astype(BF16)
        out.append(hi)
        x = x - hi.astype(F32)
    out.append(x.astype(BF16))
    return out


def _group_rms(z, g, ones_bd, group):
    hi, lo = _split_bf16(z * z, 2)
    ss = _dot(hi, ones_bd) + _dot(lo, ones_bd)
    return z * lax.rsqrt(ss * (1.0 / group) + EPS) * g


def _log_sigmoid(x):
    return jnp.minimum(x, 0.0) - jnp.log1p(jnp.exp(-jnp.abs(x)))


def _sigmoid(x):
    return 1.0 / (1.0 + jnp.exp(-x))


def _in_proj_kernel(x_ref, gmix_ref, w_ref, wa2_ref, ba_ref, gsq_ref, gsk_ref, bd_ref,
                    oq_ref, ok_ref, ov_ref, or_ref, ola_ref, osq_ref, osk_ref, osv_ref):
    h = _rms_rows(x_ref[...], gmix_ref[...]).astype(BF16)

    def proj(lo, width):
        return _dot(h, w_ref[:, lo:lo + width])

    oq_ref[...] = proj(_C_GQ, GLA_QK_W) * (GLA_DK ** -0.5)
    ok_ref[...] = proj(_C_GK, GLA_QK_W)
    ov_ref[...] = proj(_C_GV, GLA_V_W)
    or_ref[...] = proj(_C_GR, GLA_V_W)
    osv_ref[...] = proj(_C_SV, SWA_W)
    ga = proj(_C_GA, LANES)
    xa = _dot(ga.astype(BF16), wa2_ref[...]) + ba_ref[...]
    ola_ref[...] = _log_sigmoid(xa) * (1.0 / GLA_TAU)
    bd = bd_ref[...]
    osq_ref[...] = _group_rms(proj(_C_SQ, SWA_W), gsq_ref[...], bd, SWA_HD)
    osk_ref[...] = _group_rms(proj(_C_SK, SWA_W), gsk_ref[...], bd, SWA_HD)


def _in_proj(x, gmix, w_in_p, wa2_p, ba, gsq, gsk, bd, tm):
    n = x.shape[0]
    row = lambda w: pl.BlockSpec((tm, w), lambda i: (i, 0))
    full = lambda a: pl.BlockSpec(a.shape, lambda i: (0,) * a.ndim)
    widths = (GLA_QK_W, GLA_QK_W, GLA_V_W, GLA_V_W, GLA_QK_W, SWA_W, SWA_W, SWA_W)
    return pl.pallas_call(
        _in_proj_kernel,
        grid=(n // tm,),
        in_specs=[row(D_MODEL), full(gmix), full(w_in_p), full(wa2_p), full(ba), full(gsq), full(gsk), full(bd)],
        out_specs=[row(w) for w in widths],
        out_shape=[jax.ShapeDtypeStruct((n, w), F32) for w in widths],
        compiler_params=_cparams("parallel"),
        name="in_proj",
    )(x, gmix, w_in_p, wa2_p, ba, gsq, gsk, bd)


def _gla_kernel(q_ref, k_ref, v_ref, la_ref, r_ref, s0t_ref, gout_ref, lcat_ref, qmask_ref,
                o_ref, st_ref, st_scr, *, n_chunks):
    tb = pl.program_id(1)
    C, S, NS = GLA_CHUNK, GLA_SUB, GLA_CHUNK // GLA_SUB

    @pl.when(tb == 0)
    def _():
        st_scr[...] = s0t_ref[...]

    lcat = lcat_ref[...]
    qmask = qmask_ref[...]
    gout = gout_ref[...]
    row = lax.broadcasted_iota(jnp.int32, (C, GLA_DK), 0)
    tril = lax.broadcasted_iota(jnp.int32, (C, C), 0) >= lax.broadcasted_iota(jnp.int32, (C, C), 1)

    def chunk(c, carry):
        r0 = pl.multiple_of(c * C, C)
        rows = pl.ds(r0, C)
        g3 = _split_bf16(la_ref[rows, :], 3)
        b_r = _dot(lcat, g3[0]) + _dot(lcat, g3[1]) + _dot(lcat, g3[2])
        b = b_r[:C]
        ref = b_r[C:]
        b_end = b[C - 1:C, :]
        q = q_ref[rows, :]
        k = k_ref[rows, :]
        q_hat = q * jnp.exp(b - ref)
        q_til = q * jnp.exp(b)
        k_dec = k * jnp.exp(b_end - b)
        a_end = jnp.exp(b_end)
        for h in range(GLA_HEADS):
            sl = slice(h * GLA_DK, (h + 1) * GLA_DK)
            vl = slice(h * GLA_DV, (h + 1) * GLA_DV)
            b_h, ref_h, k_h = b[:, sl], ref[:, sl], k[:, sl]
            pieces = []
            for i in range(NS):
                e = ref_h[S * i:S * i + 1, :] - b_h
                e = jnp.where(row < S * (i + 1), e, NEG)
                pieces.append(k_h * jnp.exp(e))
            k_big = jnp.concatenate(pieces, axis=1).astype(BF16)
            q_big = (jnp.concatenate([q_hat[:, sl]] * NS, axis=1) * qmask).astype(BF16)
            sc = jnp.where(tril, _dot_nt(q_big, k_big), 0.0)
            v_h = v_ref[rows, vl]
            st = st_scr[h]
            o = _dot(sc.astype(BF16), v_h.astype(BF16)) + _dot_nt(q_til[:, sl].astype(BF16), st.astype(BF16))
            st_scr[h] = st * a_end[:, sl] + _dot(v_h.T.astype(BF16), k_dec[:, sl].astype(BF16))
            gate = r_ref[rows, vl]
            o_ref[rows, vl] = (_rms_rows(o, gout) * (gate * _sigmoid(gate))).astype(o_ref.dtype)
        return carry

    lax.fori_loop(0, n_chunks, chunk, 0)

    @pl.when(tb == pl.num_programs(1) - 1)
    def _():
        st_ref[...] = st_scr[...]


def _gla_consts():
    C, S = GLA_CHUNK, GLA_SUB
    t = np.arange(C)
    incl = (t[None, :] <= t[:, None]).astype(np.float32)
    upto = (t[None, :] < (t[:, None] // S) * S).astype(np.float32)
    lcat = np.concatenate([incl, upto], axis=0)
    col = np.arange((C // S) * GLA_DK)
    qmask = ((t[:, None] // S) == (col[None, :] // GLA_DK)).astype(np.float32)
    return jnp.asarray(lcat, BF16), jnp.asarray(qmask, F32)


def _gla(q, k, v, la, r, s0t, gout, tb):
    B, T, _ = q.shape
    lcat, qmask = _gla_consts()
    blk = lambda w: pl.BlockSpec((None, tb, w), lambda b, t: (b, t, 0))
    full = lambda a: pl.BlockSpec(a.shape, lambda b, t: (0,) * a.ndim)
    st_spec = pl.BlockSpec((None, GLA_HEADS, GLA_DV, GLA_DK), lambda b, t: (b, 0, 0, 0))
    return pl.pallas_call(
        functools.partial(_gla_kernel, n_chunks=tb // GLA_CHUNK),
        grid=(B, T // tb),
        in_specs=[blk(GLA_QK_W), blk(GLA_QK_W), blk(GLA_V_W), blk(GLA_QK_W), blk(GLA_V_W), st_spec,
                  full(gout), full(lcat), full(qmask)],
        out_specs=[blk(GLA_V_W), st_spec],
        out_shape=[jax.ShapeDtypeStruct((B, T, GLA_V_W), BF16),
                   jax.ShapeDtypeStruct((B, GLA_HEADS, GLA_DV, GLA_DK), F32)],
        scratch_shapes=[pltpu.VMEM((GLA_HEADS, GLA_DV, GLA_DK), F32)],
        compiler_params=_cparams("parallel", "arbitrary"),
        name="gla",
    )(q, k, v, la, r, s0t, gout, lcat, qmask)


def _band_attn_kernel(q_ref, kp_ref, kc_ref, vp_ref, vc_ref, o_ref, lse_ref, k_scr, v_scr, *, dil, n_sub):
    i = pl.program_id(2)
    k_scr[0:BAND, :] = kp_ref[...].astype(BF16)
    k_scr[BAND:, :] = kc_ref[...].astype(BF16)
    v_scr[0:BAND, :] = vp_ref[...].astype(BF16)
    v_scr[BAND:, :] = vc_ref[...].astype(BF16)
    t = lax.broadcasted_iota(jnp.int32, (BAND, 2 * BAND), 0)
    c = lax.broadcasted_iota(jnp.int32, (BAND, 2 * BAND), 1)
    dist = BAND + t - c
    in_band = (dist >= 0) & (dist <= BAND)
    distf = (dist * dil).astype(F32)

    def sub_block(j, carry):
        r0 = pl.multiple_of(j * BAND, BAND)
        rows = pl.ds(r0, BAND)
        valid = in_band & ((c >= BAND) | (i > 0) | (j > 0))
        q = (q_ref[rows, :] * (SWA_HD ** -0.5)).astype(BF16)
        k = k_scr[pl.ds(r0, 2 * BAND), :]
        v = v_scr[pl.ds(r0, 2 * BAND), :]
        outs, lses = [], []
        for h in range(SWA_HEADS):
            sl = slice(h * SWA_HD, (h + 1) * SWA_HD)
            s = _dot_nt(q[:, sl], k[:, sl]) - (2.0 ** -(h + 1)) * distf
            s = jnp.where(valid, s, NEG)
            m = jnp.max(s, axis=-1, keepdims=True)
            p = jnp.exp(s - m)
            den = jnp.sum(p, axis=-1, keepdims=True)
            outs.append(_dot(p.astype(BF16), v[:, sl]) / den)
            lses.append(jnp.broadcast_to(m + jnp.log(den), (BAND, SWA_HD)))
        o_ref[rows, :] = jnp.concatenate(outs, axis=1)
        lse_ref[rows, :] = jnp.concatenate(lses, axis=1)
        return carry

    lax.fori_loop(0, n_sub, sub_block, 0)


def _band_attn(q, k, v, dil):
    B, T, W = q.shape
    Tc = T // dil
    qb = min(4 * BAND, Tc)
    n_sub = qb // BAND
    view = lambda a: a.reshape(B, Tc, dil * W)
    cur = pl.BlockSpec((None, qb, W), lambda b, r, i: (b, i, r))
    prev = pl.BlockSpec((None, BAND, W), lambda b, r, i: (b, jnp.maximum(i * n_sub - 1, 0), r))
    o, lse = pl.pallas_call(
        functools.partial(_band_attn_kernel, dil=dil, n_sub=n_sub),
        grid=(B, dil, Tc // qb),
        in_specs=[cur, prev, cur, prev, cur],
        out_specs=[cur, cur],
        out_shape=[jax.ShapeDtypeStruct((B, Tc, dil * W), F32)] * 2,
        scratch_shapes=[pltpu.VMEM((BAND + qb, W), BF16)] * 2,
        compiler_params=_cparams("parallel", "parallel", "parallel"),
        name="band_attn_d%d" % dil,
    )(view(q), view(k), view(k), view(v), view(v))
    return o.reshape(B * T, W), lse.reshape(B * T, W)


def _pattern_count(dist):
    cnt = jnp.zeros(dist.shape, F32)
    for window, dil in DILATED_PATTERNS:
        hit = (dist >= 0) & (dist <= window)
        if dil > 1:
            hit = hit & ((dist & (dil - 1)) == 0)
        cnt = cnt + hit.astype(F32)
    return cnt


def _samp_attn_kernel(q_ref, kn_ref, vn_ref, kp_ref, vp_ref, o_ref, ko_ref, vo_ref):
    P, Tn = kp_ref.shape[0], kn_ref.shape[0]
    ko_ref[0:P - Tn, :] = kp_ref[Tn:P, :]
    ko_ref[P - Tn:P, :] = kn_ref[...]
    vo_ref[0:P - Tn, :] = vp_ref[Tn:P, :]
    vo_ref[P - Tn:P, :] = vn_ref[...]

    qi = lax.broadcasted_iota(jnp.int32, (Tn, P), 0)
    dist_p = P + qi - lax.broadcasted_iota(jnp.int32, (Tn, P), 1)
    dist_n = lax.broadcasted_iota(jnp.int32, (Tn, Tn), 0) - lax.broadcasted_iota(jnp.int32, (Tn, Tn), 1)
    cnt_p, cnt_n = _pattern_count(dist_p), _pattern_count(dist_n)
    dpf, dnf = dist_p.astype(F32), dist_n.astype(F32)
    q = q_ref[...].astype(BF16)
    kp, vp = kp_ref[...].astype(BF16), vp_ref[...].astype(BF16)
    kn, vn = kn_ref[...].astype(BF16), vn_ref[...].astype(BF16)
    outs = []
    for h in range(SWA_HEADS):
        sl = slice(h * SWA_HD, (h + 1) * SWA_HD)
        slope = 2.0 ** -(h + 1)
        lp = jnp.where(cnt_p > 0, _dot_nt(q[:, sl], kp[:, sl]) * (SWA_HD ** -0.5) - slope * dpf, NEG)
        ln = jnp.where(cnt_n > 0, _dot_nt(q[:, sl], kn[:, sl]) * (SWA_HD ** -0.5) - slope * dnf, NEG)
        m = jnp.maximum(jnp.max(lp, axis=-1, keepdims=True), jnp.max(ln, axis=-1, keepdims=True))
        pp = cnt_p * jnp.exp(lp - m)
        pn = cnt_n * jnp.exp(ln - m)
        den = jnp.sum(pp, axis=-1, keepdims=True) + jnp.sum(pn, axis=-1, keepdims=True)
        outs.append((_dot(pp.astype(BF16), vp[:, sl]) + _dot(pn.astype(BF16), vn[:, sl])) / den)
    o_ref[...] = jnp.concatenate(outs, axis=1)


def _samp_attn(q, kn, vn, kp, vp):
    B, Tn, W = q.shape
    P = kp.shape[1]
    new = pl.BlockSpec((None, Tn, W), lambda b: (b, 0, 0))
    past = pl.BlockSpec((None, P, W), lambda b: (b, 0, 0))
    return pl.pallas_call(
        _samp_attn_kernel,
        grid=(B,),
        in_specs=[new, new, new, past, past],
        out_specs=[new, past, past],
        out_shape=[jax.ShapeDtypeStruct((B, Tn, W), F32), jax.ShapeDtypeStruct((B, P, W), F32),
                   jax.ShapeDtypeStruct((B, P, W), F32)],
        compiler_params=_cparams("parallel"),
        name="samp_attn",
    )(q, kn, vn, kp, vp)


def _mem_kv_kernel(mem_ref, gmem_ref, wk_ref, wv_ref, gxk_ref, mk_ref, mv_ref):
    hm = _rms_rows(mem_ref[...], gmem_ref[...]).astype(BF16)
    mk = _dot(hm, wk_ref[...])
    gxk = gxk_ref[...]
    for h in range(X_HEADS):
        sl = slice(h * X_HD, (h + 1) * X_HD)
        mk_ref[:, sl] = _rms_rows(mk[:, sl], gxk)
    mv_ref[...] = _dot(hm, wv_ref[...])


def _mem_kv(mem, gmem, wk, wv, gxk):
    n = mem.shape[0]
    return pl.pallas_call(
        _mem_kv_kernel,
        out_shape=[jax.ShapeDtypeStruct((n, X_W), F32)] * 2,
        compiler_params=pltpu.CompilerParams(vmem_limit_bytes=VMEM_LIMIT),
        name="mem_kv",
    )(mem, gmem, wk, wv, gxk)


def _post_kernel(*refs, n_pat, n_seg):
    n_swa = 2 * n_pat if n_pat > 1 else 1
    x_ref, og_ref = refs[0], refs[1]
    swa_refs = refs[2:2 + n_swa]
    (wout_ref, gx_ref, wxq_ref, gxq_ref, mk_ref, mv_ref, wxo_ref, gffn_ref, wr_ref, br_ref, tri_ref, cnt0_ref,
     x2_ref, h3_ref, ti_ref, tg_ref, cnt_ref, ox_scr, run_scr) = refs[2 + n_swa:]

    @pl.when(pl.program_id(0) == 0)
    def _():
        run_scr[...] = cnt0_ref[...]

    if n_pat > 1:
        o_p = [r[...] for r in swa_refs[:n_pat]]
        l_p = [r[...] for r in swa_refs[n_pat:]]
        lmax = functools.reduce(jnp.maximum, l_p)
        w_p = [jnp.exp(l - lmax) for l in l_p]
        o_swa = sum(w * o for w, o in zip(w_p, o_p)) / sum(w_p)
    else:
        o_swa = swa_refs[0][...]

    x1 = x_ref[...] + _dot(og_ref[...], wout_ref[0:GLA_V_W, :]) + _dot(o_swa.astype(BF16), wout_ref[GLA_V_W:, :])

    q = _dot(_rms_rows(x1, gx_ref[...]).astype(BF16), wxq_ref[...])
    gxq = gxq_ref[...]
    seg = x1.shape[0] // n_seg
    for h in range(X_HEADS):
        sl = slice(h * X_HD, (h + 1) * X_HD)
        qn = _rms_rows(q[:, sl], gxq).astype(BF16)
        for s in range(n_seg):
            rows = slice(s * seg, (s + 1) * seg)
            sc = _dot_nt(qn[rows], mk_ref[s, :, sl].astype(BF16)) * (X_HD ** -0.5)
            p = jnp.exp(sc - jnp.max(sc, axis=-1, keepdims=True))
            p = p / jnp.sum(p, axis=-1, keepdims=True)
            ox_scr[rows, sl] = _dot(p.astype(BF16), mv_ref[s, :, sl].astype(BF16))
    x2 = x1 + _dot(ox_scr[...].astype(BF16), wxo_ref[...])
    x2_ref[...] = x2

    h3 = _rms_rows(x2, gffn_ref[...]).astype(BF16)
    h3_ref[...] = h3
    work = _dot(h3, wr_ref[...]) + br_ref[...]
    lane = lax.broadcasted_iota(jnp.int32, work.shape, 1)
    vals, idxs = [], []
    for _ in range(TOP_K):
        m = jnp.max(work, axis=-1, keepdims=True)
        idx = jnp.min(jnp.where(work == m, lane, LANES), axis=-1, keepdims=True)
        vals.append(m)
        idxs.append(idx)
        work = jnp.where(lane == idx, -jnp.inf, work)
    es = [jnp.exp(v - vals[0]) for v in vals]
    den = sum(es)
    sel = [lane == idx for idx in idxs]
    onehot = functools.reduce(jnp.logical_or, sel).astype(BF16)
    run = run_scr[...]
    rank = _dot(tri_ref[...], onehot) + run
    run_scr[...] = run + jnp.sum(onehot.astype(F32), axis=0, keepdims=True)
    cnt_ref[...] = run_scr[...]
    ti = jnp.zeros(work.shape, jnp.int32)
    tg = jnp.zeros(work.shape, F32)
    for j in range(TOP_K):
        pos = jnp.sum(jnp.where(sel[j], rank, 0.0), axis=-1, keepdims=True).astype(jnp.int32)
        ti = jnp.where(lane == j, idxs[j], ti)
        ti = jnp.where(lane == TOP_K + j, pos, ti)
        tg = jnp.where(lane == j, es[j] / den, tg)
    ti_ref[...] = ti
    tg_ref[...] = tg


def _post(x, og, swa, wout, gx, wxq, gxq, mk, mv, wxo, gffn, wr, br, cnt0, tm, n_seg, rows_per_mem):
    n = x.shape[0]
    n_pat = len(swa) // 2 if len(swa) > 1 else 1
    tri = jnp.asarray(np.tril(np.ones((tm, tm), np.float32), -1), BF16)
    row = lambda w: pl.BlockSpec((tm, w), lambda i: (i, 0))
    full = lambda a: pl.BlockSpec(a.shape, lambda i: (0,) * a.ndim)
    mem = pl.BlockSpec((n_seg, MEM_LEN, X_W), lambda i: ((i * tm) // (rows_per_mem * n_seg), 0, 0))
    return pl.pallas_call(
        functools.partial(_post_kernel, n_pat=n_pat, n_seg=n_seg),
        grid=(n // tm,),
        in_specs=[row(D_MODEL), row(GLA_V_W)] + [row(SWA_W)] * len(swa)
        + [full(wout), full(gx), full(wxq), full(gxq), mem, mem, full(wxo), full(gffn), full(wr), full(br),
           full(tri), full(cnt0)],
        out_specs=[row(D_MODEL), row(D_MODEL), row(LANES), row(LANES), full(cnt0)],
        out_shape=[jax.ShapeDtypeStruct((n, D_MODEL), F32), jax.ShapeDtypeStruct((n, D_MODEL), BF16),
                   jax.ShapeDtypeStruct((n, LANES), jnp.int32), jax.ShapeDtypeStruct((n, LANES), F32),
                   jax.ShapeDtypeStruct((1, LANES), F32)],
        scratch_shapes=[pltpu.VMEM((tm, X_W), F32), pltpu.VMEM((1, LANES), F32)],
        compiler_params=_cparams("arbitrary"),
        name="post",
    )(x, og, *swa, wout, gx, wxq, gxq, mk, mv, wxo, gffn, wr, br, tri, cnt0)


def _moe_kernel(te_ref, nu_ref, x_ref, wg_ref, bg_ref, wu_ref, bu_ref, wd_ref, bd_ref, y_ref, wg_scr, wu_scr, wd_scr):
    i = pl.program_id(0)
    live = i < nu_ref[0]

    @pl.when(live & ((i == 0) | (te_ref[i] != te_ref[jnp.maximum(i - 1, 0)])))
    def _():
        wg_scr[...] = wg_ref[...].astype(BF16)
        wu_scr[...] = wu_ref[...].astype(BF16)
        wd_scr[...] = wd_ref[...].astype(BF16)

    @pl.when(live)
    def _():
        x = x_ref[...]
        g = jnp.minimum(_dot(x, wg_scr[...]) + bg_ref[...], SWIGLU_LIMIT)
        u = jnp.clip(_dot(x, wu_scr[...]) + bu_ref[...], -SWIGLU_LIMIT, SWIGLU_LIMIT)
        a = g * _sigmoid(SWIGLU_ALPHA * g) * (u + 1.0)
        y_ref[...] = (_dot(a.astype(BF16), wd_scr[...]) + bd_ref[...]).astype(y_ref.dtype)

    @pl.when(jnp.logical_not(live))
    def _():
        y_ref[...] = jnp.zeros(y_ref.shape, y_ref.dtype)


def _moe(tile_expert, n_used, xs, wg, bg, wu, bu, wd, bd):
    P = xs.shape[0]
    tm = MOE_TM
    w_spec = pl.BlockSpec((None, D_MODEL, D_MODEL), lambda i, te, nu: (te[i], 0, 0))
    b_spec = pl.BlockSpec((None, 1, D_MODEL), lambda i, te, nu: (te[i], 0, 0))
    return pl.pallas_call(
        _moe_kernel,
        grid_spec=pltpu.PrefetchScalarGridSpec(
            num_scalar_prefetch=2,
            grid=(P // tm,),
            in_specs=[pl.BlockSpec((tm, D_MODEL), lambda i, te, nu: (i, 0)),
                      w_spec, b_spec, w_spec, b_spec, w_spec, b_spec],
            out_specs=pl.BlockSpec((tm, D_MODEL), lambda i, te, nu: (i, 0)),
            scratch_shapes=[pltpu.VMEM((D_MODEL, D_MODEL), BF16)] * 3,
        ),
        out_shape=jax.ShapeDtypeStruct((P, D_MODEL), BF16),
        compiler_params=_cparams("arbitrary"),
        name="moe",
    )(tile_expert, n_used, xs, wg, bg, wu, bu, wd, bd)


def _route(top_i, rank, counts, tm):
    n = top_i.shape[0]
    a = n * TOP_K
    n_tiles = a // tm + N_EXPERTS
    tiles_e = (counts + tm - 1) // tm
    tile_end = jnp.cumsum(tiles_e)
    slot0 = (tile_end - tiles_e) * tm
    experts = jnp.arange(N_EXPERTS, dtype=jnp.int32)
    dest = rank + jnp.sum(jnp.where(top_i[..., None] == experts, slot0, 0), axis=-1)
    n_used = tile_end[-1:].astype(jnp.int32)
    tile_expert = jnp.minimum(
        jnp.sum((tile_end[None, :] <= jnp.arange(n_tiles, dtype=jnp.int32)[:, None]).astype(jnp.int32), axis=1),
        N_EXPERTS - 1)
    pad_tok = -1 - (jnp.arange(n_tiles * tm, dtype=jnp.int32) % n)
    src_tok = pad_tok.at[dest.reshape(a)].max(jnp.arange(a, dtype=jnp.int32) // TOP_K, unique_indices=True)
    src_tok = jnp.where(src_tok < 0, -1 - src_tok, src_tok)
    return tile_expert, n_used, src_tok, dest


def _combine_kernel(x_ref, y_ref, g_ref, o_ref):
    acc = x_ref[...]
    g = g_ref[...]
    for j in range(TOP_K):
        acc = acc + g[:, j:j + 1] * y_ref[j].astype(F32)
    o_ref[...] = acc


def _combine(x2, ysg, tg, row0, tm):
    n = x2.shape[0]
    off = row0 // tm
    return pl.pallas_call(
        _combine_kernel,
        grid=(n // tm,),
        in_specs=[pl.BlockSpec((tm, D_MODEL), lambda i: (i, 0)),
                  pl.BlockSpec((TOP_K, tm, D_MODEL), lambda i: (0, i + off, 0)),
                  pl.BlockSpec((tm, LANES), lambda i: (i, 0))],
        out_specs=pl.BlockSpec((tm, D_MODEL), lambda i: (i, 0)),
        out_shape=jax.ShapeDtypeStruct((n, D_MODEL), F32),
        compiler_params=_cparams("parallel"),
        name="combine",
    )(x2, ysg, tg)


@compute_on("tpu_sparsecore")
@jax.jit
def _sc_take(x, idx):
    return jnp.take(x, idx, axis=0, mode="clip")


def _mixer_inputs(x2d, w, tm):
    return _in_proj(x2d, w["gmix"], w["w_in_p"], w["wa2_p"], w["ba"], w["gsq"], w["gsk"], w["bd"], tm)


def kernel(x_prompt, x_sample, mem_prompt, state_gla, cache_swa_k, cache_swa_v, cache_mem_k, cache_mem_v, g_mix, w_in, w_gla_a2, b_gla_a, g_gla_out, g_swa_q, g_swa_k, w_out, g_mem, w_mk, w_mv, g_xk, g_xattn, w_xq, g_xq, w_xo, g_ffn, w_router, b_router, w_gate, b_gate, w_up, b_up, w_down, b_down):
    B, T, D = x_prompt.shape
    Bs, Ts, _ = x_sample.shape
    P = cache_swa_k.shape[2]
    l = 0

    wi = w_in[l]
    segs = np.cumsum((0, GLA_QK_W, GLA_QK_W, GLA_V_W, GLA_V_W, GLA_RANK, SWA_W, SWA_W, SWA_W))
    gq_c, gk_c, gv_c, gr_c, ga_c, sq_c, sk_c, sv_c = [wi[:, segs[j]:segs[j + 1]] for j in range(8)]
    w_in_p = jnp.concatenate(
        [gq_c, gk_c, gv_c, gr_c, sq_c, sk_c, sv_c, ga_c, jnp.zeros((D, D_IN_PAD - _C_GA - GLA_RANK), F32)],
        axis=1).astype(BF16)
    heads = np.arange(SWA_W) // SWA_HD
    w = dict(
        gmix=g_mix[l][None], w_in_p=w_in_p,
        wa2_p=jnp.concatenate([w_gla_a2[l], jnp.zeros((LANES - GLA_RANK, GLA_QK_W), F32)], axis=0).astype(BF16),
        ba=b_gla_a[l][None],
        gsq=jnp.tile(g_swa_q[l], SWA_HEADS)[None], gsk=jnp.tile(g_swa_k[l], SWA_HEADS)[None],
        bd=jnp.asarray(heads[:, None] == heads[None, :], BF16),
    )
    gout = g_gla_out[l][None]
    wout = w_out[l].astype(BF16)
    wxq, wxo = w_xq[l].astype(BF16), w_xo[l].astype(BF16)
    wr = jnp.concatenate([w_router[l], jnp.zeros((D, LANES - N_EXPERTS), F32)], axis=1).astype(BF16)
    br = jnp.concatenate([b_router[l], jnp.full((LANES - N_EXPERTS,), NEG, F32)])[None]
    post_w = (wout, g_xattn[l][None], wxq, g_xq[l][None])
    post_w2 = (wxo, g_ffn[l][None], wr, br)

    gq, gk, gv, gr, la, sq, sk, sv = _mixer_inputs(x_prompt.reshape(B * T, D), w, 512)
    r3 = lambda a: a.reshape(B, T, a.shape[-1])
    og_p, st_p = _gla(r3(gq), r3(gk), r3(gv), r3(la), r3(gr),
                      jnp.zeros((B, GLA_HEADS, GLA_DV, GLA_DK), F32), gout, 512)
    o_pat, l_pat = [], []
    for _, dil in DILATED_PATTERNS:
        o_c, l_c = _band_attn(r3(sq), r3(sk), r3(sv), dil)
        o_pat.append(o_c)
        l_pat.append(l_c)
    sk_p, sv_p = sk, sv
    mk_p, mv_p = _mem_kv(mem_prompt.reshape(B * MEM_LEN, D), g_mem[l][None], w_mk[l].astype(BF16),
                         w_mv[l].astype(BF16), g_xk[l][None])
    x2_p, h3_p, ti_p, tg_p, cnt_p = _post(x_prompt.reshape(B * T, D), og_p.reshape(B * T, GLA_V_W), o_pat + l_pat,
                                          *post_w, mk_p.reshape(B, MEM_LEN, X_W), mv_p.reshape(B, MEM_LEN, X_W),
                                          *post_w2, jnp.zeros((1, LANES), F32), tm=256, n_seg=1, rows_per_mem=T)

    gq, gk, gv, gr, la, sq, sk, sv = _mixer_inputs(x_sample.reshape(Bs * Ts, D), w, Bs * Ts)
    pad = lambda a: jnp.pad(a.reshape(Bs, Ts, a.shape[-1]), ((0, 0), (0, GLA_CHUNK - Ts), (0, 0)))
    og_s, st_s = _gla(pad(gq), pad(gk), pad(gv), pad(la), pad(gr),
                      jnp.swapaxes(state_gla[l], -1, -2), gout, GLA_CHUNK)
    og_s = og_s[:, :Ts].reshape(Bs * Ts, GLA_V_W)
    s3 = lambda a: a.reshape(Bs, Ts, SWA_W)
    o_swa_s, kb_s, vb_s = _samp_attn(s3(sq), s3(sk), s3(sv), cache_swa_k[l].reshape(Bs, P, SWA_W),
                                     cache_swa_v[l].reshape(Bs, P, SWA_W))
    seqs = 8
    x2_s, h3_s, ti_s, tg_s, cnt = _post(x_sample.reshape(Bs * Ts, D), og_s, [o_swa_s.reshape(Bs * Ts, SWA_W)],
                                        *post_w, cache_mem_k[l].reshape(Bs, MEM_LEN, X_W),
                                        cache_mem_v[l].reshape(Bs, MEM_LEN, X_W),
                                        *post_w2, cnt_p, tm=seqs * Ts, n_seg=seqs, rows_per_mem=Ts)

    n_p = B * T
    h3 = jnp.concatenate([h3_p, h3_s], axis=0)
    ti = jnp.concatenate([ti_p[:, :2 * TOP_K], ti_s[:, :2 * TOP_K]], axis=0)
    tile_expert, n_used, src_tok, dest = _route(ti[:, :TOP_K], ti[:, TOP_K:], cnt[0, :N_EXPERTS].astype(jnp.int32),
                                                MOE_TM)
    ys = _moe(tile_expert, n_used, _sc_take(h3, src_tok), w_gate[l], b_gate[l][:, None, :], w_up[l],
              b_up[l][:, None, :], w_down[l], b_down[l][:, None, :])
    ysg = _sc_take(ys, dest.T.reshape(-1)).reshape(TOP_K, n_p + Bs * Ts, D)
    y_p = _combine(x2_p, ysg, tg_p, 0, 256)
    y_s = _combine(x2_s, ysg, tg_s, n_p, Bs * Ts)

    w_buf = min(SWA_WINDOW, T)
    kv_p = lambda a: a.reshape(B, T, SWA_HEADS, SWA_HD)[:, T - w_buf:][None]
    return (y_p.reshape(B, T, D), y_s.reshape(Bs, Ts, D),
            jnp.swapaxes(st_p, -1, -2)[None], kv_p(sk_p), kv_p(sv_p),
            mk_p.reshape(1, B, MEM_LEN, X_HEADS, X_HD), mv_p.reshape(1, B, MEM_LEN, X_HEADS, X_HD),
            jnp.swapaxes(st_s, -1, -2)[None],
            kb_s.reshape(1, Bs, P, SWA_HEADS, SWA_HD), vb_s.reshape(1, Bs, P, SWA_HEADS, SWA_HD))
```

```python
import functools

import numpy as np
import jax
import jax.numpy as jnp
from jax import lax
from jax.experimental import pallas as pl
from jax.experimental.compute_on import compute_on
from jax.experimental.pallas import tpu as pltpu

F32 = jnp.float32
BF16 = jnp.bfloat16

D_MODEL = 1024
GLA_HEADS = 4
GLA_DK = 64
GLA_DV = 128
GLA_RANK = 16
GLA_TAU = 16.0
GLA_CHUNK = 64
GLA_SUB = 16
SWA_HEADS = 8
SWA_HD = 64
DILATED_PATTERNS = ((128, 1), (512, 4), (2048, 16))
BAND = 128
SWA_WINDOW = 2048
MEM_LEN = 256
X_HEADS = 4
X_HD = 128
N_EXPERTS = 32
TOP_K = 4
SWIGLU_ALPHA = 1.702
SWIGLU_LIMIT = 7.0
EPS = 1e-6

GLA_QK_W = GLA_HEADS * GLA_DK
GLA_V_W = GLA_HEADS * GLA_DV
SWA_W = SWA_HEADS * SWA_HD
X_W = X_HEADS * X_HD
LANES = 128
NEG = -1e30
VMEM_LIMIT = 56 * 1024 * 1024

_C_GQ, _C_GK, _C_GV, _C_GR, _C_SQ, _C_SK, _C_SV, _C_GA = 0, 256, 512, 1024, 1536, 2048, 2560, 3072
D_IN_PAD = 3200

MOE_TM = 512


def _cparams(*sem):
    return pltpu.CompilerParams(dimension_semantics=sem, vmem_limit_bytes=VMEM_LIMIT)


def _dot(a, b):
    return jnp.dot(a, b, preferred_element_type=F32)


def _dot_nt(a, b):
    return lax.dot_general(a, b, (((1,), (1,)), ((), ())), preferred_element_type=F32)


def _rms_rows(x, g):
    return x * lax.rsqrt(jnp.mean(x * x, axis=-1, keepdims=True) + EPS) * g


def _split_bf16(x, n):
    out = []
    for _ in range(n - 1):
        hi = x.astype(BF16)
        out.append(hi)
        x = x - hi.astype(F32)
    out.append(x.astype(BF16))
    return out


def _group_rms(z, g, ones_bd, group):
    hi, lo = _split_bf16(z * z, 2)
    ss = _dot(hi, ones_bd) + _dot(lo, ones_bd)
    return z * lax.rsqrt(ss * (1.0 / group) + EPS) * g


def _log_sigmoid(x):
    return jnp.minimum(x, 0.0) - jnp.log1p(jnp.exp(-jnp.abs(x)))


def _sigmoid(x):
    return 1.0 / (1.0 + jnp.exp(-x))


CLASS_DILS = tuple(d for _, d in DILATED_PATTERNS if d > 1)


def _in_proj_kernel(x_ref, gmix_ref, w_ref, wa2_ref, ba_ref, gsq_ref, gsk_ref, bd_ref,
                    oq_ref, ok_ref, ov_ref, or_ref, ola_ref, osq_ref, osk_ref, osv_ref, *rest, classes):
    h = _rms_rows(x_ref[...], gmix_ref[...]).astype(BF16)

    def proj(lo, width):
        return _dot(h, w_ref[:, lo:lo + width])

    oq_ref[...] = proj(_C_GQ, GLA_QK_W) * (GLA_DK ** -0.5)
    ok_ref[...] = proj(_C_GK, GLA_QK_W)
    ov_ref[...] = proj(_C_GV, GLA_V_W)
    or_ref[...] = proj(_C_GR, GLA_V_W)
    osv_ref[...] = proj(_C_SV, SWA_W)
    ga = proj(_C_GA, LANES)
    xa = _dot(ga.astype(BF16), wa2_ref[...]) + ba_ref[...]
    ola_ref[...] = _log_sigmoid(xa) * (1.0 / GLA_TAU)
    bd = bd_ref[...]
    osq_ref[...] = _group_rms(proj(_C_SQ, SWA_W), gsq_ref[...], bd, SWA_HD)
    osk_ref[...] = _group_rms(proj(_C_SK, SWA_W), gsk_ref[...], bd, SWA_HD)

    if classes:
        cls_refs, col_scr = rest[:-1], rest[-1]
        tm = x_ref.shape[0]
        n_col = SWA_W // LANES
        for j, src in enumerate((osq_ref, osk_ref, osv_ref)):
            for g in range(n_col):
                col_scr[g] = src[:, g * LANES:(g + 1) * LANES]
            for c, d in enumerate(CLASS_DILS):
                dst = cls_refs[j * len(CLASS_DILS) + c]
                for r in range(d):
                    for g in range(n_col):
                        lo = r * SWA_W + g * LANES
                        dst[:, lo:lo + LANES] = col_scr[g, pl.ds(r, tm // d, stride=d), :].astype(BF16)


def _in_proj(x, gmix, w_in_p, wa2_p, ba, gsq, gsk, bd, tm, classes):
    n = x.shape[0]
    row = lambda w: pl.BlockSpec((tm, w), lambda i: (i, 0))
    full = lambda a: pl.BlockSpec(a.shape, lambda i: (0,) * a.ndim)
    widths = (GLA_QK_W, GLA_QK_W, GLA_V_W, GLA_V_W, GLA_QK_W, SWA_W, SWA_W, SWA_W)
    out_specs = [row(w) for w in widths]
    out_shape = [jax.ShapeDtypeStruct((n, w), F32) for w in widths]
    if classes:
        for _ in range(3):
            for d in CLASS_DILS:
                out_specs.append(pl.BlockSpec((tm // d, d * SWA_W), lambda i: (i, 0)))
                out_shape.append(jax.ShapeDtypeStruct((n // d, d * SWA_W), BF16))
    return pl.pallas_call(
        functools.partial(_in_proj_kernel, classes=classes),
        grid=(n // tm,),
        in_specs=[row(D_MODEL), full(gmix), full(w_in_p), full(wa2_p), full(ba), full(gsq), full(gsk), full(bd)],
        out_specs=out_specs,
        out_shape=out_shape,
        scratch_shapes=[pltpu.VMEM((SWA_W // LANES, tm, LANES), F32)] if classes else [],
        compiler_params=_cparams("parallel"),
        name="in_proj",
    )(x, gmix, w_in_p, wa2_p, ba, gsq, gsk, bd)


def _gla_kernel(q_ref, k_ref, v_ref, la_ref, r_ref, s0t_ref, gout_ref, lcat_ref, qmask_ref,
                o_ref, st_ref, st_scr, *, n_chunks):
    tb = pl.program_id(1)
    C, S, NS = GLA_CHUNK, GLA_SUB, GLA_CHUNK // GLA_SUB

    @pl.when(tb == 0)
    def _():
        st_scr[...] = s0t_ref[...]

    lcat = lcat_ref[...]
    qmask = qmask_ref[...]
    gout = gout_ref[...]
    row = lax.broadcasted_iota(jnp.int32, (C, GLA_DK), 0)
    tril = lax.broadcasted_iota(jnp.int32, (C, C), 0) >= lax.broadcasted_iota(jnp.int32, (C, C), 1)

    def chunk(c, carry):
        r0 = pl.multiple_of(c * C, C)
        rows = pl.ds(r0, C)
        g3 = _split_bf16(la_ref[rows, :], 3)
        b_r = _dot(lcat, g3[0]) + _dot(lcat, g3[1]) + _dot(lcat, g3[2])
        b = b_r[:C]
        ref = b_r[C:]
        b_end = b[C - 1:C, :]
        q = q_ref[rows, :]
        k = k_ref[rows, :]
        q_hat = q * jnp.exp(b - ref)
        q_til = q * jnp.exp(b)
        k_dec = k * jnp.exp(b_end - b)
        a_end = jnp.exp(b_end)
        for h in range(GLA_HEADS):
            sl = slice(h * GLA_DK, (h + 1) * GLA_DK)
            vl = slice(h * GLA_DV, (h + 1) * GLA_DV)
            b_h, ref_h, k_h = b[:, sl], ref[:, sl], k[:, sl]
            pieces = []
            for i in range(NS):
                e = ref_h[S * i:S * i + 1, :] - b_h
                e = jnp.where(row < S * (i + 1), e, NEG)
                pieces.append(k_h * jnp.exp(e))
            k_big = jnp.concatenate(pieces, axis=1).astype(BF16)
            q_big = (jnp.concatenate([q_hat[:, sl]] * NS, axis=1) * qmask).astype(BF16)
            sc = jnp.where(tril, _dot_nt(q_big, k_big), 0.0)
            v_h = v_ref[rows, vl]
            st = st_scr[h]
            o = _dot(sc.astype(BF16), v_h.astype(BF16)) + _dot_nt(q_til[:, sl].astype(BF16), st.astype(BF16))
            st_scr[h] = st * a_end[:, sl] + _dot(v_h.T.astype(BF16), k_dec[:, sl].astype(BF16))
            gate = r_ref[rows, vl]
            o_ref[rows, vl] = (_rms_rows(o, gout) * (gate * _sigmoid(gate))).astype(o_ref.dtype)
        return carry

    lax.fori_loop(0, n_chunks, chunk, 0)

    @pl.when(tb == pl.num_programs(1) - 1)
    def _():
        st_ref[...] = st_scr[...]


def _gla_consts():
    C, S = GLA_CHUNK, GLA_SUB
    t = np.arange(C)
    incl = (t[None, :] <= t[:, None]).astype(np.float32)
    upto = (t[None, :] < (t[:, None] // S) * S).astype(np.float32)
    lcat = np.concatenate([incl, upto], axis=0)
    col = np.arange((C // S) * GLA_DK)
    qmask = ((t[:, None] // S) == (col[None, :] // GLA_DK)).astype(np.float32)
    return jnp.asarray(lcat, BF16), jnp.asarray(qmask, F32)


def _gla(q, k, v, la, r, s0t, gout, tb):
    B, T, _ = q.shape
    lcat, qmask = _gla_consts()
    blk = lambda w: pl.BlockSpec((None, tb, w), lambda b, t: (b, t, 0))
    full = lambda a: pl.BlockSpec(a.shape, lambda b, t: (0,) * a.ndim)
    st_spec = pl.BlockSpec((None, GLA_HEADS, GLA_DV, GLA_DK), lambda b, t: (b, 0, 0, 0))
    return pl.pallas_call(
        functools.partial(_gla_kernel, n_chunks=tb // GLA_CHUNK),
        grid=(B, T // tb),
        in_specs=[blk(GLA_QK_W), blk(GLA_QK_W), blk(GLA_V_W), blk(GLA_QK_W), blk(GLA_V_W), st_spec,
                  full(gout), full(lcat), full(qmask)],
        out_specs=[blk(GLA_V_W), st_spec],
        out_shape=[jax.ShapeDtypeStruct((B, T, GLA_V_W), BF16),
                   jax.ShapeDtypeStruct((B, GLA_HEADS, GLA_DV, GLA_DK), F32)],
        scratch_shapes=[pltpu.VMEM((GLA_HEADS, GLA_DV, GLA_DK), F32)],
        compiler_params=_cparams("parallel", "arbitrary"),
        name="gla",
    )(q, k, v, la, r, s0t, gout, lcat, qmask)


def _band_attn_kernel(q_ref, kp_ref, kc_ref, vp_ref, vc_ref, o_ref, lse_ref, k_scr, v_scr, *, dil, n_sub):
    i = pl.program_id(2)
    k_scr[0:BAND, :] = kp_ref[...].astype(BF16)
    k_scr[BAND:, :] = kc_ref[...].astype(BF16)
    v_scr[0:BAND, :] = vp_ref[...].astype(BF16)
    v_scr[BAND:, :] = vc_ref[...].astype(BF16)
    t = lax.broadcasted_iota(jnp.int32, (BAND, 2 * BAND), 0)
    c = lax.broadcasted_iota(jnp.int32, (BAND, 2 * BAND), 1)
    dist = BAND + t - c
    in_band = (dist >= 0) & (dist <= BAND)
    distf = (dist * dil).astype(F32)

    def sub_block(j, carry):
        r0 = pl.multiple_of(j * BAND, BAND)
        rows = pl.ds(r0, BAND)
        valid = in_band & ((c >= BAND) | (i > 0) | (j > 0))
        q = (q_ref[rows, :].astype(F32) * (SWA_HD ** -0.5)).astype(BF16)
        k = k_scr[pl.ds(r0, 2 * BAND), :]
        v = v_scr[pl.ds(r0, 2 * BAND), :]
        outs, lses = [], []
        for h in range(SWA_HEADS):
            sl = slice(h * SWA_HD, (h + 1) * SWA_HD)
            s = _dot_nt(q[:, sl], k[:, sl]) - (2.0 ** -(h + 1)) * distf
            s = jnp.where(valid, s, NEG)
            m = jnp.max(s, axis=-1, keepdims=True)
            p = jnp.exp(s - m)
            den = jnp.sum(p, axis=-1, keepdims=True)
            outs.append(_dot(p.astype(BF16), v[:, sl]) / den)
            lses.append(jnp.broadcast_to(m + jnp.log(den), (BAND, SWA_HD)))
        o_ref[rows, :] = jnp.concatenate(outs, axis=1)
        lse_ref[rows, :] = jnp.concatenate(lses, axis=1)
        return carry

    lax.fori_loop(0, n_sub, sub_block, 0)


def _band_attn(q, k, v, dil):
    B, Tc, dw = q.shape
    W = dw // dil
    qb = min(4 * BAND, Tc)
    n_sub = qb // BAND
    cur = pl.BlockSpec((None, qb, W), lambda b, r, i: (b, i, r))
    prev = pl.BlockSpec((None, BAND, W), lambda b, r, i: (b, jnp.maximum(i * n_sub - 1, 0), r))
    o, lse = pl.pallas_call(
        functools.partial(_band_attn_kernel, dil=dil, n_sub=n_sub),
        grid=(B, dil, Tc // qb),
        in_specs=[cur, prev, cur, prev, cur],
        out_specs=[cur, cur],
        out_shape=[jax.ShapeDtypeStruct((B, Tc, dw), F32)] * 2,
        scratch_shapes=[pltpu.VMEM((BAND + qb, W), BF16)] * 2,
        compiler_params=_cparams("parallel", "parallel", "parallel"),
        name="band_attn_d%d" % dil,
    )(q, k, k, v, v)
    return o.reshape(B * Tc, dw), lse.reshape(B * Tc, dw)


def _pattern_count(dist):
    cnt = jnp.zeros(dist.shape, F32)
    for window, dil in DILATED_PATTERNS:
        hit = (dist >= 0) & (dist <= window)
        if dil > 1:
            hit = hit & ((dist & (dil - 1)) == 0)
        cnt = cnt + hit.astype(F32)
    return cnt


def _samp_attn_kernel(q_ref, kn_ref, vn_ref, kp_ref, vp_ref, o_ref, ko_ref, vo_ref):
    P, Tn = kp_ref.shape[2], kn_ref.shape[2]
    qi = lax.broadcasted_iota(jnp.int32, (Tn, P), 0)
    dist_p = P + qi - lax.broadcasted_iota(jnp.int32, (Tn, P), 1)
    dist_n = lax.broadcasted_iota(jnp.int32, (Tn, Tn), 0) - lax.broadcasted_iota(jnp.int32, (Tn, Tn), 1)
    cnt_p, cnt_n = _pattern_count(dist_p), _pattern_count(dist_n)
    dpf, dnf = dist_p.astype(F32), dist_n.astype(F32)
    q = (q_ref[...] * (SWA_HD ** -0.5)).astype(BF16)
    outs = []
    for h in range(SWA_HEADS):
        kp, vp, kn, vn = kp_ref[h], vp_ref[h], kn_ref[h], vn_ref[h]
        ko_ref[h] = pltpu.roll(kp, P - Tn, axis=1)
        ko_ref[h, :, P - Tn:] = kn
        vo_ref[h] = pltpu.roll(vp, P - Tn, axis=1)
        vo_ref[h, :, P - Tn:] = vn
        qh = q[:, h * SWA_HD:(h + 1) * SWA_HD]
        slope = 2.0 ** -(h + 1)
        lp = jnp.where(cnt_p > 0, _dot(qh, kp.astype(BF16)) - slope * dpf, NEG)
        ln = jnp.where(cnt_n > 0, _dot(qh, kn.astype(BF16)) - slope * dnf, NEG)
        m = jnp.maximum(jnp.max(lp, axis=-1, keepdims=True), jnp.max(ln, axis=-1, keepdims=True))
        pp = cnt_p * jnp.exp(lp - m)
        pn = cnt_n * jnp.exp(ln - m)
        den = jnp.sum(pp, axis=-1, keepdims=True) + jnp.sum(pn, axis=-1, keepdims=True)
        outs.append((_dot_nt(pp.astype(BF16), vp.astype(BF16)) + _dot_nt(pn.astype(BF16), vn.astype(BF16))) / den)
    o_ref[...] = jnp.concatenate(outs, axis=1)


def _samp_attn(q, knT, vnT, kpT, vpT):
    B, Tn, W = q.shape
    P = kpT.shape[-1]
    qs = pl.BlockSpec((None, Tn, W), lambda b: (b, 0, 0))
    new = pl.BlockSpec((None, SWA_HEADS, SWA_HD, Tn), lambda b: (b, 0, 0, 0))
    past = pl.BlockSpec((None, SWA_HEADS, SWA_HD, P), lambda b: (b, 0, 0, 0))
    return pl.pallas_call(
        _samp_attn_kernel,
        grid=(B,),
        in_specs=[qs, new, new, past, past],
        out_specs=[qs, past, past],
        out_shape=[jax.ShapeDtypeStruct((B, Tn, W), F32), jax.ShapeDtypeStruct(kpT.shape, F32),
                   jax.ShapeDtypeStruct(kpT.shape, F32)],
        compiler_params=_cparams("parallel"),
        name="samp_attn",
    )(q, knT, vnT, kpT, vpT)


def _mem_kv_kernel(mem_ref, gmem_ref, wk_ref, wv_ref, gxk_ref, mk_ref, mv_ref):
    hm = _rms_rows(mem_ref[...], gmem_ref[...]).astype(BF16)
    mk = _dot(hm, wk_ref[...])
    gxk = gxk_ref[...]
    for h in range(X_HEADS):
        sl = slice(h * X_HD, (h + 1) * X_HD)
        mk_ref[:, sl] = _rms_rows(mk[:, sl], gxk)
    mv_ref[...] = _dot(hm, wv_ref[...])


def _mem_kv(mem, gmem, wk, wv, gxk):
    n = mem.shape[0]
    return pl.pallas_call(
        _mem_kv_kernel,
        out_shape=[jax.ShapeDtypeStruct((n, X_W), F32)] * 2,
        compiler_params=pltpu.CompilerParams(vmem_limit_bytes=VMEM_LIMIT),
        name="mem_kv",
    )(mem, gmem, wk, wv, gxk)


def _post_kernel(*refs, n_pat, n_seg):
    n_swa = 2 * n_pat if n_pat > 1 else 1
    x_ref, og_ref = refs[0], refs[1]
    swa_refs = refs[2:2 + n_swa]
    (wout_ref, gx_ref, wxq_ref, gxq_ref, mk_ref, mv_ref, wxo_ref, gffn_ref, wr_ref, br_ref, tri_ref, cnt0_ref,
     x2_ref, h3_ref, ti_ref, tg_ref, cnt_ref, ox_scr, run_scr) = refs[2 + n_swa:2 + n_swa + 19]

    @pl.when(pl.program_id(0) == 0)
    def _():
        run_scr[...] = cnt0_ref[...]

    if n_pat > 1:
        dei_scr = refs[-1]
        tm = x_ref.shape[0]
        vals = []
        for a, r in enumerate(swa_refs):
            d = DILATED_PATTERNS[a % n_pat][1]
            if d == 1:
                vals.append(r[...])
            else:
                slot, n_col = len(vals), SWA_W // LANES
                for c in range(d):
                    for g in range(n_col):
                        lo = c * SWA_W + g * LANES
                        dei_scr[slot, g, pl.ds(c, tm // d, stride=d), :] = r[:, lo:lo + LANES]
                vals.append(jnp.concatenate([dei_scr[slot, g] for g in range(n_col)], axis=1))
        o_p, l_p = vals[:n_pat], vals[n_pat:]
        lmax = functools.reduce(jnp.maximum, l_p)
        w_p = [jnp.exp(l - lmax) for l in l_p]
        o_swa = sum(w * o for w, o in zip(w_p, o_p)) / sum(w_p)
    else:
        o_swa = swa_refs[0][...]

    x1 = x_ref[...] + _dot(og_ref[...], wout_ref[0:GLA_V_W, :]) + _dot(o_swa.astype(BF16), wout_ref[GLA_V_W:, :])

    q = _dot(_rms_rows(x1, gx_ref[...]).astype(BF16), wxq_ref[...])
    gxq = gxq_ref[...]
    seg = x1.shape[0] // n_seg
    for h in range(X_HEADS):
        sl = slice(h * X_HD, (h + 1) * X_HD)
        qn = _rms_rows(q[:, sl], gxq).astype(BF16)
        for s in range(n_seg):
            rows = slice(s * seg, (s + 1) * seg)
            sc = _dot_nt(qn[rows], mk_ref[s, :, sl].astype(BF16)) * (X_HD ** -0.5)
            p = jnp.exp(sc - jnp.max(sc, axis=-1, keepdims=True))
            p = p / jnp.sum(p, axis=-1, keepdims=True)
            ox_scr[rows, sl] = _dot(p.astype(BF16), mv_ref[s, :, sl].astype(BF16))
    x2 = x1 + _dot(ox_scr[...].astype(BF16), wxo_ref[...])
    x2_ref[...] = x2

    h3 = _rms_rows(x2, gffn_ref[...]).astype(BF16)
    h3_ref[...] = h3
    work = _dot(h3, wr_ref[...]) + br_ref[...]
    lane = lax.broadcasted_iota(jnp.int32, work.shape, 1)
    vals, idxs = [], []
    for _ in range(TOP_K):
        m = jnp.max(work, axis=-1, keepdims=True)
        idx = jnp.min(jnp.where(work == m, lane, LANES), axis=-1, keepdims=True)
        vals.append(m)
        idxs.append(idx)
        work = jnp.where(lane == idx, -jnp.inf, work)
    es = [jnp.exp(v - vals[0]) for v in vals]
    den = sum(es)
    sel = [lane == idx for idx in idxs]
    onehot = functools.reduce(jnp.logical_or, sel).astype(BF16)
    run = run_scr[...]
    rank = _dot(tri_ref[...], onehot) + run
    run_scr[...] = run + jnp.sum(onehot.astype(F32), axis=0, keepdims=True)
    cnt_ref[...] = run_scr[...]
    ti = jnp.zeros(work.shape, jnp.int32)
    tg = jnp.zeros(work.shape, F32)
    for j in range(TOP_K):
        pos = jnp.sum(jnp.where(sel[j], rank, 0.0), axis=-1, keepdims=True).astype(jnp.int32)
        ti = jnp.where(lane == j, idxs[j], ti)
        ti = jnp.where(lane == TOP_K + j, pos, ti)
        tg = jnp.where(lane == j, es[j] / den, tg)
    ti_ref[...] = ti
    tg_ref[...] = tg


def _post(x, og, swa, wout, gx, wxq, gxq, mk, mv, wxo, gffn, wr, br, cnt0, tm, n_seg, rows_per_mem):
    n = x.shape[0]
    n_pat = len(swa) // 2 if len(swa) > 1 else 1
    tri = jnp.asarray(np.tril(np.ones((tm, tm), np.float32), -1), BF16)
    row = lambda w: pl.BlockSpec((tm, w), lambda i: (i, 0))
    full = lambda a: pl.BlockSpec(a.shape, lambda i: (0,) * a.ndim)
    mem = pl.BlockSpec((n_seg, MEM_LEN, X_W), lambda i: ((i * tm) // (rows_per_mem * n_seg), 0, 0))
    swa_specs = [pl.BlockSpec((tm * SWA_W // a.shape[1], a.shape[1]), lambda i: (i, 0)) for a in swa]
    scratch = [pltpu.VMEM((tm, X_W), F32), pltpu.VMEM((1, LANES), F32)]
    if n_pat > 1:
        scratch.append(pltpu.VMEM((len(swa), SWA_W // LANES, tm, LANES), F32))
    return pl.pallas_call(
        functools.partial(_post_kernel, n_pat=n_pat, n_seg=n_seg),
        grid=(n // tm,),
        in_specs=[row(D_MODEL), row(GLA_V_W)] + swa_specs
        + [full(wout), full(gx), full(wxq), full(gxq), mem, mem, full(wxo), full(gffn), full(wr), full(br),
           full(tri), full(cnt0)],
        out_specs=[row(D_MODEL), row(D_MODEL), row(LANES), row(LANES), full(cnt0)],
        out_shape=[jax.ShapeDtypeStruct((n, D_MODEL), F32), jax.ShapeDtypeStruct((n, D_MODEL), BF16),
                   jax.ShapeDtypeStruct((n, LANES), jnp.int32), jax.ShapeDtypeStruct((n, LANES), F32),
                   jax.ShapeDtypeStruct((1, LANES), F32)],
        scratch_shapes=scratch,
        compiler_params=_cparams("arbitrary"),
        name="post",
    )(x, og, *swa, wout, gx, wxq, gxq, mk, mv, wxo, gffn, wr, br, tri, cnt0)


def _moe_kernel(te_ref, nu_ref, x_ref, wg_ref, bg_ref, wu_ref, bu_ref, wd_ref, bd_ref, y_ref, wg_scr, wu_scr, wd_scr):
    i = pl.program_id(0)
    live = i < nu_ref[0]

    @pl.when(live & ((i == 0) | (te_ref[i] != te_ref[jnp.maximum(i - 1, 0)])))
    def _():
        wg_scr[...] = wg_ref[...].astype(BF16)
        wu_scr[...] = wu_ref[...].astype(BF16)
        wd_scr[...] = wd_ref[...].astype(BF16)

    @pl.when(live)
    def _():
        x = x_ref[...]
        g = jnp.minimum(_dot(x, wg_scr[...]) + bg_ref[...], SWIGLU_LIMIT)
        u = jnp.clip(_dot(x, wu_scr[...]) + bu_ref[...], -SWIGLU_LIMIT, SWIGLU_LIMIT)
        a = g * _sigmoid(SWIGLU_ALPHA * g) * (u + 1.0)
        y_ref[...] = (_dot(a.astype(BF16), wd_scr[...]) + bd_ref[...]).astype(y_ref.dtype)

    @pl.when(jnp.logical_not(live))
    def _():
        y_ref[...] = jnp.zeros(y_ref.shape, y_ref.dtype)


def _moe(tile_expert, n_used, xs, wg, bg, wu, bu, wd, bd):
    P = xs.shape[0]
    tm = MOE_TM
    w_spec = pl.BlockSpec((None, D_MODEL, D_MODEL), lambda i, te, nu: (te[i], 0, 0))
    b_spec = pl.BlockSpec((None, 1, D_MODEL), lambda i, te, nu: (te[i], 0, 0))
    return pl.pallas_call(
        _moe_kernel,
        grid_spec=pltpu.PrefetchScalarGridSpec(
            num_scalar_prefetch=2,
            grid=(P // tm,),
            in_specs=[pl.BlockSpec((tm, D_MODEL), lambda i, te, nu: (i, 0)),
                      w_spec, b_spec, w_spec, b_spec, w_spec, b_spec],
            out_specs=pl.BlockSpec((tm, D_MODEL), lambda i, te, nu: (i, 0)),
            scratch_shapes=[pltpu.VMEM((D_MODEL, D_MODEL), BF16)] * 3,
        ),
        out_shape=jax.ShapeDtypeStruct((P, D_MODEL), BF16),
        compiler_params=_cparams("arbitrary"),
        name="moe",
    )(tile_expert, n_used, xs, wg, bg, wu, bu, wd, bd)


def _route(top_i, rank, counts, tm):
    n = top_i.shape[0]
    a = n * TOP_K
    n_tiles = a // tm + N_EXPERTS
    tiles_e = (counts + tm - 1) // tm
    tile_end = jnp.cumsum(tiles_e)
    slot0 = (tile_end - tiles_e) * tm
    experts = jnp.arange(N_EXPERTS, dtype=jnp.int32)
    dest = rank + jnp.sum(jnp.where(top_i[..., None] == experts, slot0, 0), axis=-1)
    n_used = tile_end[-1:].astype(jnp.int32)
    tile_expert = jnp.minimum(
        jnp.sum((tile_end[None, :] <= jnp.arange(n_tiles, dtype=jnp.int32)[:, None]).astype(jnp.int32), axis=1),
        N_EXPERTS - 1)
    pad_tok = -1 - (jnp.arange(n_tiles * tm, dtype=jnp.int32) % n)
    src_tok = pad_tok.at[dest.reshape(a)].max(jnp.arange(a, dtype=jnp.int32) // TOP_K, unique_indices=True)
    src_tok = jnp.where(src_tok < 0, -1 - src_tok, src_tok)
    return tile_expert, n_used, src_tok, dest


def _combine_kernel(x_ref, y_ref, g_ref, o_ref):
    acc = x_ref[...]
    g = g_ref[...]
    for j in range(TOP_K):
        acc = acc + g[:, j:j + 1] * y_ref[j].astype(F32)
    o_ref[...] = acc


def _combine(x2, ysg, tg, row0, tm):
    n = x2.shape[0]
    off = row0 // tm
    return pl.pallas_call(
        _combine_kernel,
        grid=(n // tm,),
        in_specs=[pl.BlockSpec((tm, D_MODEL), lambda i: (i, 0)),
                  pl.BlockSpec((TOP_K, tm, D_MODEL), lambda i: (0, i + off, 0)),
                  pl.BlockSpec((tm, LANES), lambda i: (i, 0))],
        out_specs=pl.BlockSpec((tm, D_MODEL), lambda i: (i, 0)),
        out_shape=jax.ShapeDtypeStruct((n, D_MODEL), F32),
        compiler_params=_cparams("parallel"),
        name="combine",
    )(x2, ysg, tg)


@compute_on("tpu_sparsecore")
@jax.jit
def _sc_take(x, idx):
    return jnp.take(x, idx, axis=0, mode="clip")


def _mixer_inputs(x2d, w, tm, classes):
    return _in_proj(x2d, w["gmix"], w["w_in_p"], w["wa2_p"], w["ba"], w["gsq"], w["gsk"], w["bd"], tm, classes)


def kernel(x_prompt, x_sample, mem_prompt, state_gla, cache_swa_k, cache_swa_v, cache_mem_k, cache_mem_v, g_mix, w_in, w_gla_a2, b_gla_a, g_gla_out, g_swa_q, g_swa_k, w_out, g_mem, w_mk, w_mv, g_xk, g_xattn, w_xq, g_xq, w_xo, g_ffn, w_router, b_router, w_gate, b_gate, w_up, b_up, w_down, b_down):
    B, T, D = x_prompt.shape
    Bs, Ts, _ = x_sample.shape
    P = cache_swa_k.shape[2]
    l = 0

    wi = w_in[l]
    segs = np.cumsum((0, GLA_QK_W, GLA_QK_W, GLA_V_W, GLA_V_W, GLA_RANK, SWA_W, SWA_W, SWA_W))
    gq_c, gk_c, gv_c, gr_c, ga_c, sq_c, sk_c, sv_c = [wi[:, segs[j]:segs[j + 1]] for j in range(8)]
    w_in_p = jnp.concatenate(
        [gq_c, gk_c, gv_c, gr_c, sq_c, sk_c, sv_c, ga_c, jnp.zeros((D, D_IN_PAD - _C_GA - GLA_RANK), F32)],
        axis=1).astype(BF16)
    heads = np.arange(SWA_W) // SWA_HD
    w = dict(
        gmix=g_mix[l][None], w_in_p=w_in_p,
        wa2_p=jnp.concatenate([w_gla_a2[l], jnp.zeros((LANES - GLA_RANK, GLA_QK_W), F32)], axis=0).astype(BF16),
        ba=b_gla_a[l][None],
        gsq=jnp.tile(g_swa_q[l], SWA_HEADS)[None], gsk=jnp.tile(g_swa_k[l], SWA_HEADS)[None],
        bd=jnp.asarray(heads[:, None] == heads[None, :], BF16),
    )
    gout = g_gla_out[l][None]
    wout = w_out[l].astype(BF16)
    wxq, wxo = w_xq[l].astype(BF16), w_xo[l].astype(BF16)
    wr = jnp.concatenate([w_router[l], jnp.zeros((D, LANES - N_EXPERTS), F32)], axis=1).astype(BF16)
    br = jnp.concatenate([b_router[l], jnp.full((LANES - N_EXPERTS,), NEG, F32)])[None]
    post_w = (wout, g_xattn[l][None], wxq, g_xq[l][None])
    post_w2 = (wxo, g_ffn[l][None], wr, br)

    gq, gk, gv, gr, la, sq, sk, sv, *cls = _mixer_inputs(x_prompt.reshape(B * T, D), w, 512, True)
    r3 = lambda a: a.reshape(B, T, a.shape[-1])
    og_p, st_p = _gla(r3(gq), r3(gk), r3(gv), r3(la), r3(gr),
                      jnp.zeros((B, GLA_HEADS, GLA_DV, GLA_DK), F32), gout, 512)
    o_pat, l_pat = [], []
    for _, dil in DILATED_PATTERNS:
        if dil == 1:
            qkv = [r3(sq), r3(sk), r3(sv)]
        else:
            c = CLASS_DILS.index(dil)
            qkv = [cls[j * len(CLASS_DILS) + c].reshape(B, T // dil, dil * SWA_W) for j in range(3)]
        o_c, l_c = _band_attn(*qkv, dil)
        o_pat.append(o_c)
        l_pat.append(l_c)
    sk_p, sv_p = sk, sv
    mk_p, mv_p = _mem_kv(mem_prompt.reshape(B * MEM_LEN, D), g_mem[l][None], w_mk[l].astype(BF16),
                         w_mv[l].astype(BF16), g_xk[l][None])
    x2_p, h3_p, ti_p, tg_p, cnt_p = _post(x_prompt.reshape(B * T, D), og_p.reshape(B * T, GLA_V_W), o_pat + l_pat,
                                          *post_w, mk_p.reshape(B, MEM_LEN, X_W), mv_p.reshape(B, MEM_LEN, X_W),
                                          *post_w2, jnp.zeros((1, LANES), F32), tm=256, n_seg=1, rows_per_mem=T)

    gq, gk, gv, gr, la, sq, sk, sv = _mixer_inputs(x_sample.reshape(Bs * Ts, D), w, Bs * Ts, False)
    pad = lambda a: jnp.pad(a.reshape(Bs, Ts, a.shape[-1]), ((0, 0), (0, GLA_CHUNK - Ts), (0, 0)))
    og_s, st_s = _gla(pad(gq), pad(gk), pad(gv), pad(la), pad(gr),
                      jnp.swapaxes(state_gla[l], -1, -2), gout, GLA_CHUNK)
    og_s = og_s[:, :Ts].reshape(Bs * Ts, GLA_V_W)
    rows_minor = lambda a: jnp.transpose(a.reshape(Bs, -1, SWA_HEADS, SWA_HD), (0, 2, 3, 1))
    o_swa_s, kb_s, vb_s = _samp_attn(sq.reshape(Bs, Ts, SWA_W), rows_minor(sk), rows_minor(sv),
                                     rows_minor(cache_swa_k[l]), rows_minor(cache_swa_v[l]))
    kb_s, vb_s = jnp.transpose(kb_s, (0, 3, 1, 2)), jnp.transpose(vb_s, (0, 3, 1, 2))
    seqs = 8
    x2_s, h3_s, ti_s, tg_s, cnt = _post(x_sample.reshape(Bs * Ts, D), og_s, [o_swa_s.reshape(Bs * Ts, SWA_W)],
                                        *post_w, cache_mem_k[l].reshape(Bs, MEM_LEN, X_W),
                                        cache_mem_v[l].reshape(Bs, MEM_LEN, X_W),
                                        *post_w2, cnt_p, tm=seqs * Ts, n_seg=seqs, rows_per_mem=Ts)

    n_p = B * T
    h3 = jnp.concatenate([h3_p, h3_s], axis=0)
    ti = jnp.concatenate([ti_p[:, :2 * TOP_K], ti_s[:, :2 * TOP_K]], axis=0)
    tile_expert, n_used, src_tok, dest = _route(ti[:, :TOP_K], ti[:, TOP_K:], cnt[0, :N_EXPERTS].astype(jnp.int32),
                                                MOE_TM)
    ys = _moe(tile_expert, n_used, _sc_take(h3, src_tok), w_gate[l], b_gate[l][:, None, :], w_up[l],
              b_up[l][:, None, :], w_down[l], b_down[l][:, None, :])
    ysg = _sc_take(ys, dest.T.reshape(-1)).reshape(TOP_K, n_p + Bs * Ts, D)
    y_p = _combine(x2_p, ysg, tg_p, 0, 256)
    y_s = _combine(x2_s, ysg, tg_s, n_p, Bs * Ts)

    w_buf = min(SWA_WINDOW, T)
    kv_p = lambda a: a.reshape(B, T, SWA_HEADS, SWA_HD)[:, T - w_buf:][None]
    return (y_p.reshape(B, T, D), y_s.reshape(Bs, Ts, D),
            jnp.swapaxes(st_p, -1, -2)[None], kv_p(sk_p), kv_p(sv_p),
            mk_p.reshape(1, B, MEM_LEN, X_HEADS, X_HD), mv_p.reshape(1, B, MEM_LEN, X_HEADS, X_HD),
            jnp.swapaxes(st_s, -1, -2)[None],
            kb_s.reshape(1, Bs, P, SWA_HEADS, SWA_HD), vb_s.reshape(1, Bs, P, SWA_HEADS, SWA_HD))
```

```python
import functools

import numpy as np
import jax
import jax.numpy as jnp
from jax import lax
from jax.experimental import pallas as pl
from jax.experimental.compute_on import compute_on
from jax.experimental.pallas import tpu as pltpu

F32 = jnp.float32
BF16 = jnp.bfloat16

D_MODEL = 1024
GLA_HEADS = 4
GLA_DK = 64
GLA_DV = 128
GLA_RANK = 16
GLA_TAU = 16.0
GLA_CHUNK = 64
GLA_SUB = 16
SWA_HEADS = 8
SWA_HD = 64
DILATED_PATTERNS = ((128, 1), (512, 4), (2048, 16))
BAND = 128
SWA_WINDOW = 2048
MEM_LEN = 256
X_HEADS = 4
X_HD = 128
N_EXPERTS = 32
TOP_K = 4
SWIGLU_ALPHA = 1.702
SWIGLU_LIMIT = 7.0
EPS = 1e-6

GLA_QK_W = GLA_HEADS * GLA_DK
GLA_V_W = GLA_HEADS * GLA_DV
SWA_W = SWA_HEADS * SWA_HD
X_W = X_HEADS * X_HD
LANES = 128
NEG = -1e30
VMEM_LIMIT = 56 * 1024 * 1024

_C_GQ, _C_GK, _C_GV, _C_GR, _C_SQ, _C_SK, _C_SV, _C_GA = 0, 256, 512, 1024, 1536, 2048, 2560, 3072
D_IN_PAD = 3200

MOE_TM = 512


def _cparams(*sem):
    return pltpu.CompilerParams(dimension_semantics=sem, vmem_limit_bytes=VMEM_LIMIT)


def _dot(a, b):
    return jnp.dot(a, b, preferred_element_type=F32)


def _dot_nt(a, b):
    return lax.dot_general(a, b, (((1,), (1,)), ((), ())), preferred_element_type=F32)


def _rms_rows(x, g):
    return x * lax.rsqrt(jnp.mean(x * x, axis=-1, keepdims=True) + EPS) * g


def _split_bf16(x, n):
    out = []
    for _ in range(n - 1):
        hi = x.astype(BF16)
        out.append(hi)
        x = x - hi.astype(F32)
    out.append(x.astype(BF16))
    return out


def _group_rms(z, g, ones_bd, group):
    hi, lo = _split_bf16(z * z, 2)
    ss = _dot(hi, ones_bd) + _dot(lo, ones_bd)
    return z * lax.rsqrt(ss * (1.0 / group) + EPS) * g


def _log_sigmoid(x):
    return jnp.minimum(x, 0.0) - jnp.log1p(jnp.exp(-jnp.abs(x)))


def _sigmoid(x):
    return 1.0 / (1.0 + jnp.exp(-x))


CLASS_DILS = tuple(d for _, d in DILATED_PATTERNS if d > 1)


def _in_proj_kernel(x_ref, gmix_ref, w_ref, wa2_ref, ba_ref, gsq_ref, gsk_ref, bd_ref,
                    oq_ref, ok_ref, ov_ref, or_ref, ola_ref, osq_ref, osk_ref, osv_ref, *rest, classes, seq_tiles,
                    win_tiles):
    h = _rms_rows(x_ref[...], gmix_ref[...]).astype(BF16)

    def proj(lo, width):
        return _dot(h, w_ref[:, lo:lo + width])

    oq_ref[...] = proj(_C_GQ, GLA_QK_W) * (GLA_DK ** -0.5)
    ok_ref[...] = proj(_C_GK, GLA_QK_W)
    ov_ref[...] = proj(_C_GV, GLA_V_W)
    or_ref[...] = proj(_C_GR, GLA_V_W)
    osv_ref[...] = proj(_C_SV, SWA_W)
    ga = proj(_C_GA, LANES)
    xa = _dot(ga.astype(BF16), wa2_ref[...]) + ba_ref[...]
    ola_ref[...] = _log_sigmoid(xa) * (1.0 / GLA_TAU)
    bd = bd_ref[...]
    osq_ref[...] = _group_rms(proj(_C_SQ, SWA_W), gsq_ref[...], bd, SWA_HD)
    osk_ref[...] = _group_rms(proj(_C_SK, SWA_W), gsk_ref[...], bd, SWA_HD)

    if classes:
        kT_ref, vT_ref = rest[-3], rest[-2]
        rest = rest[:-3] + rest[-1:]

        @pl.when(pl.program_id(0) % seq_tiles >= seq_tiles - win_tiles)
        def _():
            kT_ref[...] = osk_ref[...].T
            vT_ref[...] = osv_ref[...].T

        cls_refs, col_scr = rest[:-1], rest[-1]
        tm = x_ref.shape[0]
        n_col = SWA_W // LANES
        for j, src in enumerate((osq_ref, osk_ref, osv_ref)):
            for g in range(n_col):
                col_scr[g] = src[:, g * LANES:(g + 1) * LANES]
            for c, d in enumerate(CLASS_DILS):
                dst = cls_refs[j * len(CLASS_DILS) + c]
                for r in range(d):
                    for g in range(n_col):
                        lo = r * SWA_W + g * LANES
                        dst[:, lo:lo + LANES] = col_scr[g, pl.ds(r, tm // d, stride=d), :].astype(BF16)


def _in_proj(x, gmix, w_in_p, wa2_p, ba, gsq, gsk, bd, tm, classes, seq_len=0, win_len=0):
    n = x.shape[0]
    row = lambda w: pl.BlockSpec((tm, w), lambda i: (i, 0))
    full = lambda a: pl.BlockSpec(a.shape, lambda i: (0,) * a.ndim)
    widths = (GLA_QK_W, GLA_QK_W, GLA_V_W, GLA_V_W, GLA_QK_W, SWA_W, SWA_W, SWA_W)
    out_specs = [row(w) for w in widths]
    out_shape = [jax.ShapeDtypeStruct((n, w), F32) for w in widths]
    seq_tiles, win_tiles = seq_len // tm, win_len // tm
    if classes:
        for _ in range(3):
            for d in CLASS_DILS:
                out_specs.append(pl.BlockSpec((tm // d, d * SWA_W), lambda i: (i, 0)))
                out_shape.append(jax.ShapeDtypeStruct((n // d, d * SWA_W), BF16))
        win = pl.BlockSpec((None, SWA_W, tm), lambda i: (i // seq_tiles, 0,
                                                         jnp.maximum(i % seq_tiles - (seq_tiles - win_tiles), 0)))
        out_specs += [win, win]
        out_shape += [jax.ShapeDtypeStruct((n // seq_len, SWA_W, win_len), F32)] * 2
    return pl.pallas_call(
        functools.partial(_in_proj_kernel, classes=classes, seq_tiles=seq_tiles, win_tiles=win_tiles),
        grid=(n // tm,),
        in_specs=[row(D_MODEL), full(gmix), full(w_in_p), full(wa2_p), full(ba), full(gsq), full(gsk), full(bd)],
        out_specs=out_specs,
        out_shape=out_shape,
        scratch_shapes=[pltpu.VMEM((SWA_W // LANES, tm, LANES), F32)] if classes else [],
        compiler_params=_cparams("arbitrary"),
        name="in_proj",
    )(x, gmix, w_in_p, wa2_p, ba, gsq, gsk, bd)


GLA_NB = 2


def _gla_kernel(q_ref, k_ref, v_ref, la_ref, r_ref, s0t_ref, gout_ref, lcat_ref, qmask_ref,
                o_ref, st_ref, *st_scr, n_chunks):
    tb = pl.program_id(1)
    C, S, NS = GLA_CHUNK, GLA_SUB, GLA_CHUNK // GLA_SUB

    @pl.when(tb == 0)
    def _():
        for n in range(GLA_NB):
            for h in range(GLA_HEADS):
                st_scr[n * GLA_HEADS + h][...] = s0t_ref[n, h]

    lcat = lcat_ref[...]
    qmask = qmask_ref[...]
    gout = gout_ref[...]
    row = lax.broadcasted_iota(jnp.int32, (C, GLA_DK), 0)
    tril = lax.broadcasted_iota(jnp.int32, (C, C), 0) >= lax.broadcasted_iota(jnp.int32, (C, C), 1)

    def chunk(c, carry):
        r0 = pl.multiple_of(c * C, C)
        rows = pl.ds(r0, C)
        for n in range(GLA_NB):
            g3 = _split_bf16(la_ref[n, rows, :], 3)
            b_r = _dot(lcat, g3[0]) + _dot(lcat, g3[1]) + _dot(lcat, g3[2])
            b = b_r[:C]
            ref = b_r[C:]
            b_end = b[C - 1:C, :]
            q = q_ref[n, rows, :]
            k = k_ref[n, rows, :]
            q_hat = q * jnp.exp(b - ref)
            q_til = q * jnp.exp(b)
            k_dec = k * jnp.exp(b_end - b)
            a_end = jnp.exp(b_end)
            for h in range(GLA_HEADS):
                sl = slice(h * GLA_DK, (h + 1) * GLA_DK)
                vl = slice(h * GLA_DV, (h + 1) * GLA_DV)
                b_h, ref_h, k_h = b[:, sl], ref[:, sl], k[:, sl]
                pieces = []
                for i in range(NS):
                    e = ref_h[S * i:S * i + 1, :] - b_h
                    e = jnp.where(row < S * (i + 1), e, NEG)
                    pieces.append(k_h * jnp.exp(e))
                k_big = jnp.concatenate(pieces, axis=1).astype(BF16)
                q_big = (jnp.concatenate([q_hat[:, sl]] * NS, axis=1) * qmask).astype(BF16)
                sc = jnp.where(tril, _dot_nt(q_big, k_big), 0.0)
                v_h = v_ref[n, rows, vl]
                st_nh = st_scr[n * GLA_HEADS + h]
                st = st_nh[...]
                o = _dot(sc.astype(BF16), v_h.astype(BF16)) + _dot_nt(q_til[:, sl].astype(BF16), st.astype(BF16))
                st_nh[...] = st * a_end[:, sl] + _dot(v_h.T.astype(BF16), k_dec[:, sl].astype(BF16))
                gate = r_ref[n, rows, vl]
                o_ref[n, rows, vl] = (_rms_rows(o, gout) * (gate * _sigmoid(gate))).astype(o_ref.dtype)
        return carry

    lax.fori_loop(0, n_chunks, chunk, 0)

    @pl.when(tb == pl.num_programs(1) - 1)
    def _():
        for n in range(GLA_NB):
            for h in range(GLA_HEADS):
                st_ref[n, h] = st_scr[n * GLA_HEADS + h][...]


def _gla_consts():
    C, S = GLA_CHUNK, GLA_SUB
    t = np.arange(C)
    incl = (t[None, :] <= t[:, None]).astype(np.float32)
    upto = (t[None, :] < (t[:, None] // S) * S).astype(np.float32)
    lcat = np.concatenate([incl, upto], axis=0)
    col = np.arange((C // S) * GLA_DK)
    qmask = ((t[:, None] // S) == (col[None, :] // GLA_DK)).astype(np.float32)
    return jnp.asarray(lcat, BF16), jnp.asarray(qmask, F32)


def _gla(q, k, v, la, r, s0t, gout, tb):
    B, T, _ = q.shape
    lcat, qmask = _gla_consts()
    blk = lambda w: pl.BlockSpec((GLA_NB, tb, w), lambda b, t: (b, t, 0))
    full = lambda a: pl.BlockSpec(a.shape, lambda b, t: (0,) * a.ndim)
    st_spec = pl.BlockSpec((GLA_NB, GLA_HEADS, GLA_DV, GLA_DK), lambda b, t: (b, 0, 0, 0))
    return pl.pallas_call(
        functools.partial(_gla_kernel, n_chunks=tb // GLA_CHUNK),
        grid=(B // GLA_NB, T // tb),
        in_specs=[blk(GLA_QK_W), blk(GLA_QK_W), blk(GLA_V_W), blk(GLA_QK_W), blk(GLA_V_W), st_spec,
                  full(gout), full(lcat), full(qmask)],
        out_specs=[blk(GLA_V_W), st_spec],
        out_shape=[jax.ShapeDtypeStruct((B, T, GLA_V_W), BF16),
                   jax.ShapeDtypeStruct((B, GLA_HEADS, GLA_DV, GLA_DK), F32)],
        scratch_shapes=[pltpu.VMEM((GLA_DV, GLA_DK), F32)] * (GLA_NB * GLA_HEADS),
        compiler_params=_cparams("parallel", "arbitrary"),
        name="gla",
    )(q, k, v, la, r, s0t, gout, lcat, qmask)


def _band_attn_kernel(q_ref, kp_ref, kc_ref, vp_ref, vc_ref, o_ref, lse_ref, k_scr, v_scr, *, dil, n_sub):
    i = pl.program_id(2)
    k_scr[0:BAND, :] = kp_ref[...].astype(BF16)
    k_scr[BAND:, :] = kc_ref[...].astype(BF16)
    v_scr[0:BAND, :] = vp_ref[...].astype(BF16)
    v_scr[BAND:, :] = vc_ref[...].astype(BF16)
    t = lax.broadcasted_iota(jnp.int32, (BAND, 2 * BAND), 0)
    c = lax.broadcasted_iota(jnp.int32, (BAND, 2 * BAND), 1)
    dist = BAND + t - c
    in_band = (dist >= 0) & (dist <= BAND)
    distf = (dist * dil).astype(F32)

    def sub_block(j, carry):
        r0 = pl.multiple_of(j * BAND, BAND)
        rows = pl.ds(r0, BAND)
        valid = in_band & ((c >= BAND) | (i > 0) | (j > 0))
        q = (q_ref[rows, :].astype(F32) * (SWA_HD ** -0.5)).astype(BF16)
        k = k_scr[pl.ds(r0, 2 * BAND), :]
        v = v_scr[pl.ds(r0, 2 * BAND), :]
        outs, lses = [], []
        for h in range(SWA_HEADS):
            sl = slice(h * SWA_HD, (h + 1) * SWA_HD)
            s = _dot_nt(q[:, sl], k[:, sl]) - (2.0 ** -(h + 1)) * distf
            s = jnp.where(valid, s, NEG)
            m = jnp.max(s, axis=-1, keepdims=True)
            p = jnp.exp(s - m)
            den = jnp.sum(p, axis=-1, keepdims=True)
            outs.append(_dot(p.astype(BF16), v[:, sl]) / den)
            lses.append(jnp.broadcast_to(m + jnp.log(den), (BAND, SWA_HD)))
        o_ref[rows, :] = jnp.concatenate(outs, axis=1)
        lse_ref[rows, :] = jnp.concatenate(lses, axis=1)
        return carry

    lax.fori_loop(0, n_sub, sub_block, 0)


def _band_attn(q, k, v, dil):
    B, Tc, dw = q.shape
    W = dw // dil
    qb = min(4 * BAND, Tc)
    n_sub = qb // BAND
    cur = pl.BlockSpec((None, qb, W), lambda b, r, i: (b, i, r))
    prev = pl.BlockSpec((None, BAND, W), lambda b, r, i: (b, jnp.maximum(i * n_sub - 1, 0), r))
    o, lse = pl.pallas_call(
        functools.partial(_band_attn_kernel, dil=dil, n_sub=n_sub),
        grid=(B, dil, Tc // qb),
        in_specs=[cur, prev, cur, prev, cur],
        out_specs=[cur, cur],
        out_shape=[jax.ShapeDtypeStruct((B, Tc, dw), F32)] * 2,
        scratch_shapes=[pltpu.VMEM((BAND + qb, W), BF16)] * 2,
        compiler_params=_cparams("parallel", "parallel", "parallel"),
        name="band_attn_d%d" % dil,
    )(q, k, k, v, v)
    return o.reshape(B * Tc, dw), lse.reshape(B * Tc, dw)


def _pattern_count(dist):
    cnt = jnp.zeros(dist.shape, F32)
    for window, dil in DILATED_PATTERNS:
        hit = (dist >= 0) & (dist <= window)
        if dil > 1:
            hit = hit & ((dist & (dil - 1)) == 0)
        cnt = cnt + hit.astype(F32)
    return cnt


def _samp_attn_kernel(q_ref, kn_ref, vn_ref, kp_ref, vp_ref, o_ref, ko_ref, vo_ref):
    P, Tn = kp_ref.shape[2], kn_ref.shape[2]
    qi = lax.broadcasted_iota(jnp.int32, (Tn, P), 0)
    dist_p = P + qi - lax.broadcasted_iota(jnp.int32, (Tn, P), 1)
    dist_n = lax.broadcasted_iota(jnp.int32, (Tn, Tn), 0) - lax.broadcasted_iota(jnp.int32, (Tn, Tn), 1)
    cnt_p, cnt_n = _pattern_count(dist_p), _pattern_count(dist_n)
    dpf, dnf = dist_p.astype(F32), dist_n.astype(F32)
    q = (q_ref[...] * (SWA_HD ** -0.5)).astype(BF16)
    outs = []
    for h in range(SWA_HEADS):
        kp, vp, kn, vn = kp_ref[h], vp_ref[h], kn_ref[h], vn_ref[h]
        ko_ref[h] = pltpu.roll(kp, P - Tn, axis=1)
        ko_ref[h, :, P - Tn:] = kn
        vo_ref[h] = pltpu.roll(vp, P - Tn, axis=1)
        vo_ref[h, :, P - Tn:] = vn
        qh = q[:, h * SWA_HD:(h + 1) * SWA_HD]
        slope = 2.0 ** -(h + 1)
        lp = jnp.where(cnt_p > 0, _dot(qh, kp.astype(BF16)) - slope * dpf, NEG)
        ln = jnp.where(cnt_n > 0, _dot(qh, kn.astype(BF16)) - slope * dnf, NEG)
        m = jnp.maximum(jnp.max(lp, axis=-1, keepdims=True), jnp.max(ln, axis=-1, keepdims=True))
        pp = cnt_p * jnp.exp(lp - m)
        pn = cnt_n * jnp.exp(ln - m)
        den = jnp.sum(pp, axis=-1, keepdims=True) + jnp.sum(pn, axis=-1, keepdims=True)
        outs.append((_dot_nt(pp.astype(BF16), vp.astype(BF16)) + _dot_nt(pn.astype(BF16), vn.astype(BF16))) / den)
    o_ref[...] = jnp.concatenate(outs, axis=1)


def _samp_attn(q, knT, vnT, kpT, vpT):
    B, Tn, W = q.shape
    P = kpT.shape[-1]
    qs = pl.BlockSpec((None, Tn, W), lambda b: (b, 0, 0))
    new = pl.BlockSpec((None, SWA_HEADS, SWA_HD, Tn), lambda b: (b, 0, 0, 0))
    past = pl.BlockSpec((None, SWA_HEADS, SWA_HD, P), lambda b: (b, 0, 0, 0))
    return pl.pallas_call(
        _samp_attn_kernel,
        grid=(B,),
        in_specs=[qs, new, new, past, past],
        out_specs=[qs, past, past],
        out_shape=[jax.ShapeDtypeStruct((B, Tn, W), F32), jax.ShapeDtypeStruct(kpT.shape, F32),
                   jax.ShapeDtypeStruct(kpT.shape, F32)],
        compiler_params=_cparams("parallel"),
        name="samp_attn",
    )(q, knT, vnT, kpT, vpT)


def _mem_kv_kernel(mem_ref, gmem_ref, wk_ref, wv_ref, gxk_ref, mk_ref, mv_ref):
    hm = _rms_rows(mem_ref[...], gmem_ref[...]).astype(BF16)
    mk = _dot(hm, wk_ref[...])
    gxk = gxk_ref[...]
    for h in range(X_HEADS):
        sl = slice(h * X_HD, (h + 1) * X_HD)
        mk_ref[:, sl] = _rms_rows(mk[:, sl], gxk)
    mv_ref[...] = _dot(hm, wv_ref[...])


def _mem_kv(mem, gmem, wk, wv, gxk):
    n = mem.shape[0]
    return pl.pallas_call(
        _mem_kv_kernel,
        out_shape=[jax.ShapeDtypeStruct((n, X_W), F32)] * 2,
        compiler_params=pltpu.CompilerParams(vmem_limit_bytes=VMEM_LIMIT),
        name="mem_kv",
    )(mem, gmem, wk, wv, gxk)


def _post_kernel(*refs, n_pat, n_seg):
    n_swa = 2 * n_pat if n_pat > 1 else 1
    x_ref, og_ref = refs[0], refs[1]
    swa_refs = refs[2:2 + n_swa]
    (wout_ref, gx_ref, wxq_ref, gxq_ref, mk_ref, mv_ref, wxo_ref, gffn_ref, wr_ref, br_ref, tri_ref, cnt0_ref,
     x2_ref, h3_ref, ti_ref, tg_ref, cnt_ref, ox_scr, run_scr) = refs[2 + n_swa:2 + n_swa + 19]

    @pl.when(pl.program_id(0) == 0)
    def _():
        run_scr[...] = cnt0_ref[...]

    if n_pat > 1:
        dei_scr = refs[-1]
        tm = x_ref.shape[0]
        vals = []
        for a, r in enumerate(swa_refs):
            d = DILATED_PATTERNS[a % n_pat][1]
            if d == 1:
                vals.append(r[...])
            else:
                slot, n_col = len(vals), SWA_W // LANES
                for c in range(d):
                    for g in range(n_col):
                        lo = c * SWA_W + g * LANES
                        dei_scr[slot, g, pl.ds(c, tm // d, stride=d), :] = r[:, lo:lo + LANES]
                vals.append(jnp.concatenate([dei_scr[slot, g] for g in range(n_col)], axis=1))
        o_p, l_p = vals[:n_pat], vals[n_pat:]
        lmax = functools.reduce(jnp.maximum, l_p)
        w_p = [jnp.exp(l - lmax) for l in l_p]
        o_swa = sum(w * o for w, o in zip(w_p, o_p)) / sum(w_p)
    else:
        o_swa = swa_refs[0][...]

    x1 = x_ref[...] + _dot(og_ref[...], wout_ref[0:GLA_V_W, :]) + _dot(o_swa.astype(BF16), wout_ref[GLA_V_W:, :])

    q = _dot(_rms_rows(x1, gx_ref[...]).astype(BF16), wxq_ref[...])
    gxq = gxq_ref[...]
    seg = x1.shape[0] // n_seg
    for h in range(X_HEADS):
        sl = slice(h * X_HD, (h + 1) * X_HD)
        qn = _rms_rows(q[:, sl], gxq).astype(BF16)
        for s in range(n_seg):
            rows = slice(s * seg, (s + 1) * seg)
            sc = _dot_nt(qn[rows], mk_ref[s, :, sl].astype(BF16)) * (X_HD ** -0.5)
            p = jnp.exp(sc - jnp.max(sc, axis=-1, keepdims=True))
            p = p / jnp.sum(p, axis=-1, keepdims=True)
            ox_scr[rows, sl] = _dot(p.astype(BF16), mv_ref[s, :, sl].astype(BF16))
    x2 = x1 + _dot(ox_scr[...].astype(BF16), wxo_ref[...])
    x2_ref[...] = x2

    h3 = _rms_rows(x2, gffn_ref[...]).astype(BF16)
    h3_ref[...] = h3
    work = _dot(h3, wr_ref[...]) + br_ref[...]
    lane = lax.broadcasted_iota(jnp.int32, work.shape, 1)
    vals, idxs = [], []
    for _ in range(TOP_K):
        m = jnp.max(work, axis=-1, keepdims=True)
        idx = jnp.min(jnp.where(work == m, lane, LANES), axis=-1, keepdims=True)
        vals.append(m)
        idxs.append(idx)
        work = jnp.where(lane == idx, -jnp.inf, work)
    es = [jnp.exp(v - vals[0]) for v in vals]
    den = sum(es)
    sel = [lane == idx for idx in idxs]
    onehot = functools.reduce(jnp.logical_or, sel).astype(BF16)
    run = run_scr[...]
    rank = _dot(tri_ref[...], onehot) + run
    run_scr[...] = run + jnp.sum(onehot.astype(F32), axis=0, keepdims=True)
    cnt_ref[...] = run_scr[...]
    ti = jnp.zeros(work.shape, jnp.int32)
    tg = jnp.zeros(work.shape, F32)
    for j in range(TOP_K):
        pos = jnp.sum(jnp.where(sel[j], rank, 0.0), axis=-1, keepdims=True).astype(jnp.int32)
        ti = jnp.where(lane == j, idxs[j], ti)
        ti = jnp.where(lane == TOP_K + j, pos, ti)
        tg = jnp.where(lane == j, es[j] / den, tg)
    ti_ref[...] = ti
    tg_ref[...] = tg


def _post(x, og, swa, wout, gx, wxq, gxq, mk, mv, wxo, gffn, wr, br, cnt0, tm, n_seg, rows_per_mem):
    n = x.shape[0]
    n_pat = len(swa) // 2 if len(swa) > 1 else 1
    tri = jnp.asarray(np.tril(np.ones((tm, tm), np.float32), -1), BF16)
    row = lambda w: pl.BlockSpec((tm, w), lambda i: (i, 0))
    full = lambda a: pl.BlockSpec(a.shape, lambda i: (0,) * a.ndim)
    mem = pl.BlockSpec((n_seg, MEM_LEN, X_W), lambda i: ((i * tm) // (rows_per_mem * n_seg), 0, 0))
    swa_specs = [pl.BlockSpec((tm * SWA_W // a.shape[1], a.shape[1]), lambda i: (i, 0)) for a in swa]
    scratch = [pltpu.VMEM((tm, X_W), F32), pltpu.VMEM((1, LANES), F32)]
    if n_pat > 1:
        scratch.append(pltpu.VMEM((len(swa), SWA_W // LANES, tm, LANES), F32))
    return pl.pallas_call(
        functools.partial(_post_kernel, n_pat=n_pat, n_seg=n_seg),
        grid=(n // tm,),
        in_specs=[row(D_MODEL), row(GLA_V_W)] + swa_specs
        + [full(wout), full(gx), full(wxq), full(gxq), mem, mem, full(wxo), full(gffn), full(wr), full(br),
           full(tri), full(cnt0)],
        out_specs=[row(D_MODEL), row(D_MODEL), row(LANES), row(LANES), full(cnt0)],
        out_shape=[jax.ShapeDtypeStruct((n, D_MODEL), F32), jax.ShapeDtypeStruct((n, D_MODEL), BF16),
                   jax.ShapeDtypeStruct((n, LANES), jnp.int32), jax.ShapeDtypeStruct((n, LANES), F32),
                   jax.ShapeDtypeStruct((1, LANES), F32)],
        scratch_shapes=scratch,
        compiler_params=_cparams("arbitrary"),
        name="post",
    )(x, og, *swa, wout, gx, wxq, gxq, mk, mv, wxo, gffn, wr, br, tri, cnt0)


def _moe_kernel(te_ref, nu_ref, x_ref, wg_ref, bg_ref, wu_ref, bu_ref, wd_ref, bd_ref, y_in_ref, y_ref,
                wg_scr, wu_scr, wd_scr):
    i = pl.program_id(0)
    live = i < nu_ref[0]

    @pl.when(live & ((i == 0) | (te_ref[i] != te_ref[jnp.maximum(i - 1, 0)])))
    def _():
        wg_scr[...] = wg_ref[...].astype(BF16)
        wu_scr[...] = wu_ref[...].astype(BF16)
        wd_scr[...] = wd_ref[...].astype(BF16)

    @pl.when(live)
    def _():
        x = x_ref[...]
        g = jnp.minimum(_dot(x, wg_scr[...]) + bg_ref[...], SWIGLU_LIMIT)
        u = jnp.clip(_dot(x, wu_scr[...]) + bu_ref[...], -SWIGLU_LIMIT, SWIGLU_LIMIT)
        a = g * _sigmoid(SWIGLU_ALPHA * g) * (u + 1.0)
        y_ref[...] = (_dot(a.astype(BF16), wd_scr[...]) + bd_ref[...]).astype(y_ref.dtype)

    @pl.when(jnp.logical_not(live))
    def _():
        y_ref[...] = jnp.zeros(y_ref.shape, y_ref.dtype)


MOE_CHUNKS = 3


def _moe(tile_expert, n_used, h3, src_tok, wg, bg, wu, bu, wd, bd):
    P = src_tok.shape[0]
    tm = MOE_TM
    n_tiles = P // tm
    ys = jnp.zeros((P, D_MODEL), BF16)
    for c in range(MOE_CHUNKS):
        lo, hi = n_tiles * c // MOE_CHUNKS, n_tiles * (c + 1) // MOE_CHUNKS
        w_spec = pl.BlockSpec((None, D_MODEL, D_MODEL), lambda i, te, nu: (te[i], 0, 0))
        b_spec = pl.BlockSpec((None, 1, D_MODEL), lambda i, te, nu: (te[i], 0, 0))
        ys = pl.pallas_call(
            _moe_kernel,
            grid_spec=pltpu.PrefetchScalarGridSpec(
                num_scalar_prefetch=2,
                grid=(hi - lo,),
                in_specs=[pl.BlockSpec((tm, D_MODEL), lambda i, te, nu: (i, 0)),
                          w_spec, b_spec, w_spec, b_spec, w_spec, b_spec,
                          pl.BlockSpec(memory_space=pl.ANY)],
                out_specs=pl.BlockSpec((tm, D_MODEL), lambda i, te, nu, lo=lo: (i + lo, 0)),
                scratch_shapes=[pltpu.VMEM((D_MODEL, D_MODEL), BF16)] * 3,
            ),
            out_shape=jax.ShapeDtypeStruct((P, D_MODEL), BF16),
            input_output_aliases={9: 0},
            compiler_params=_cparams("arbitrary"),
            name="moe",
        )(tile_expert[lo:hi], n_used - lo, _sc_take(h3, src_tok[lo * tm:hi * tm]), wg, bg, wu, bu, wd, bd, ys)
    return ys


def _route(top_i, rank, counts, tm):
    n = top_i.shape[0]
    a = n * TOP_K
    n_tiles = a // tm + N_EXPERTS
    tiles_e = (counts + tm - 1) // tm
    tile_end = jnp.cumsum(tiles_e)
    slot0 = (tile_end - tiles_e) * tm
    experts = jnp.arange(N_EXPERTS, dtype=jnp.int32)
    dest = rank + jnp.sum(jnp.where(top_i[..., None] == experts, slot0, 0), axis=-1)
    n_used = tile_end[-1:].astype(jnp.int32)
    tile_expert = jnp.minimum(
        jnp.sum((tile_end[None, :] <= jnp.arange(n_tiles, dtype=jnp.int32)[:, None]).astype(jnp.int32), axis=1),
        N_EXPERTS - 1)
    pad_tok = -1 - (jnp.arange(n_tiles * tm, dtype=jnp.int32) % n)
    src_tok = pad_tok.at[dest.reshape(a)].max(jnp.arange(a, dtype=jnp.int32) // TOP_K, unique_indices=True)
    src_tok = jnp.where(src_tok < 0, -1 - src_tok, src_tok)
    return tile_expert, n_used, src_tok, dest


def _combine_kernel(x_ref, y_ref, g_ref, o_ref):
    acc = x_ref[...]
    g = g_ref[...]
    for j in range(TOP_K):
        acc = acc + g[:, j:j + 1] * y_ref[j].astype(F32)
    o_ref[...] = acc


def _combine(x2, ysg, tg, row0, tm):
    n = x2.shape[0]
    off = row0 // tm
    return pl.pallas_call(
        _combine_kernel,
        grid=(n // tm,),
        in_specs=[pl.BlockSpec((tm, D_MODEL), lambda i: (i, 0)),
                  pl.BlockSpec((TOP_K, tm, D_MODEL), lambda i: (0, i + off, 0)),
                  pl.BlockSpec((tm, LANES), lambda i: (i, 0))],
        out_specs=pl.BlockSpec((tm, D_MODEL), lambda i: (i, 0)),
        out_shape=jax.ShapeDtypeStruct((n, D_MODEL), F32),
        compiler_params=_cparams("parallel"),
        name="combine",
    )(x2, ysg, tg)


@compute_on("tpu_sparsecore")
@jax.jit
def _sc_take(x, idx):
    return jnp.take(x, idx, axis=0, mode="clip")


def _mixer_inputs(x2d, w, tm, classes, seq_len=0, win_len=0):
    return _in_proj(x2d, w["gmix"], w["w_in_p"], w["wa2_p"], w["ba"], w["gsq"], w["gsk"], w["bd"], tm, classes,
                    seq_len, win_len)


def kernel(x_prompt, x_sample, mem_prompt, state_gla, cache_swa_k, cache_swa_v, cache_mem_k, cache_mem_v, g_mix, w_in, w_gla_a2, b_gla_a, g_gla_out, g_swa_q, g_swa_k, w_out, g_mem, w_mk, w_mv, g_xk, g_xattn, w_xq, g_xq, w_xo, g_ffn, w_router, b_router, w_gate, b_gate, w_up, b_up, w_down, b_down):
    B, T, D = x_prompt.shape
    Bs, Ts, _ = x_sample.shape
    P = cache_swa_k.shape[2]
    l = 0

    wi = w_in[l]
    segs = np.cumsum((0, GLA_QK_W, GLA_QK_W, GLA_V_W, GLA_V_W, GLA_RANK, SWA_W, SWA_W, SWA_W))
    gq_c, gk_c, gv_c, gr_c, ga_c, sq_c, sk_c, sv_c = [wi[:, segs[j]:segs[j + 1]] for j in range(8)]
    w_in_p = jnp.concatenate(
        [gq_c, gk_c, gv_c, gr_c, sq_c, sk_c, sv_c, ga_c, jnp.zeros((D, D_IN_PAD - _C_GA - GLA_RANK), F32)],
        axis=1).astype(BF16)
    heads = np.arange(SWA_W) // SWA_HD
    w = dict(
        gmix=g_mix[l][None], w_in_p=w_in_p,
        wa2_p=jnp.concatenate([w_gla_a2[l], jnp.zeros((LANES - GLA_RANK, GLA_QK_W), F32)], axis=0).astype(BF16),
        ba=b_gla_a[l][None],
        gsq=jnp.tile(g_swa_q[l], SWA_HEADS)[None], gsk=jnp.tile(g_swa_k[l], SWA_HEADS)[None],
        bd=jnp.asarray(heads[:, None] == heads[None, :], BF16),
    )
    gout = g_gla_out[l][None]
    wout = w_out[l].astype(BF16)
    wxq, wxo = w_xq[l].astype(BF16), w_xo[l].astype(BF16)
    wr = jnp.concatenate([w_router[l], jnp.zeros((D, LANES - N_EXPERTS), F32)], axis=1).astype(BF16)
    br = jnp.concatenate([b_router[l], jnp.full((LANES - N_EXPERTS,), NEG, F32)])[None]
    post_w = (wout, g_xattn[l][None], wxq, g_xq[l][None])
    post_w2 = (wxo, g_ffn[l][None], wr, br)

    w_buf = min(SWA_WINDOW, T)
    gq, gk, gv, gr, la, sq, sk, sv, *cls, kT_p, vT_p = _mixer_inputs(x_prompt.reshape(B * T, D), w, 512, True, T, w_buf)
    r3 = lambda a: a.reshape(B, T, a.shape[-1])
    og_p, st_p = _gla(r3(gq), r3(gk), r3(gv), r3(la), r3(gr),
                      jnp.zeros((B, GLA_HEADS, GLA_DV, GLA_DK), F32), gout, 512)
    o_pat, l_pat = [], []
    for _, dil in DILATED_PATTERNS:
        if dil == 1:
            qkv = [r3(sq), r3(sk), r3(sv)]
        else:
            c = CLASS_DILS.index(dil)
            qkv = [cls[j * len(CLASS_DILS) + c].reshape(B, T // dil, dil * SWA_W) for j in range(3)]
        o_c, l_c = _band_attn(*qkv, dil)
        o_pat.append(o_c)
        l_pat.append(l_c)
    mk_p, mv_p = _mem_kv(mem_prompt.reshape(B * MEM_LEN, D), g_mem[l][None], w_mk[l].astype(BF16),
                         w_mv[l].astype(BF16), g_xk[l][None])
    x2_p, h3_p, ti_p, tg_p, cnt_p = _post(x_prompt.reshape(B * T, D), og_p.reshape(B * T, GLA_V_W), o_pat + l_pat,
                                          *post_w, mk_p.reshape(B, MEM_LEN, X_W), mv_p.reshape(B, MEM_LEN, X_W),
                                          *post_w2, jnp.zeros((1, LANES), F32), tm=256, n_seg=1, rows_per_mem=T)

    gq, gk, gv, gr, la, sq, sk, sv = _mixer_inputs(x_sample.reshape(Bs * Ts, D), w, Bs * Ts, False)
    pad = lambda a: jnp.pad(a.reshape(Bs, Ts, a.shape[-1]), ((0, 0), (0, GLA_CHUNK - Ts), (0, 0)))
    og_s, st_s = _gla(pad(gq), pad(gk), pad(gv), pad(la), pad(gr),
                      jnp.swapaxes(state_gla[l], -1, -2), gout, GLA_CHUNK)
    og_s = og_s[:, :Ts].reshape(Bs * Ts, GLA_V_W)
    rows_minor = lambda a: jnp.transpose(a.reshape(Bs, -1, SWA_HEADS, SWA_HD), (0, 2, 3, 1))
    o_swa_s, kb_s, vb_s = _samp_attn(sq.reshape(Bs, Ts, SWA_W), rows_minor(sk), rows_minor(sv),
                                     rows_minor(cache_swa_k[l]), rows_minor(cache_swa_v[l]))
    kb_s, vb_s = jnp.transpose(kb_s, (0, 3, 1, 2)), jnp.transpose(vb_s, (0, 3, 1, 2))
    seqs = 8
    x2_s, h3_s, ti_s, tg_s, cnt = _post(x_sample.reshape(Bs * Ts, D), og_s, [o_swa_s.reshape(Bs * Ts, SWA_W)],
                                        *post_w, cache_mem_k[l].reshape(Bs, MEM_LEN, X_W),
                                        cache_mem_v[l].reshape(Bs, MEM_LEN, X_W),
                                        *post_w2, cnt_p, tm=seqs * Ts, n_seg=seqs, rows_per_mem=Ts)

    n_p = B * T
    h3 = jnp.concatenate([h3_p, h3_s], axis=0)
    ti = jnp.concatenate([ti_p[:, :2 * TOP_K], ti_s[:, :2 * TOP_K]], axis=0)
    tile_expert, n_used, src_tok, dest = _route(ti[:, :TOP_K], ti[:, TOP_K:], cnt[0, :N_EXPERTS].astype(jnp.int32),
                                                MOE_TM)
    ys = _moe(tile_expert, n_used, h3, src_tok, w_gate[l], b_gate[l][:, None, :], w_up[l],
              b_up[l][:, None, :], w_down[l], b_down[l][:, None, :])
    ysg = _sc_take(ys, dest.T.reshape(-1)).reshape(TOP_K, n_p + Bs * Ts, D)
    y_p = _combine(x2_p, ysg, tg_p, 0, 256)
    y_s = _combine(x2_s, ysg, tg_s, n_p, Bs * Ts)

    kv_p = lambda a: jnp.transpose(a.reshape(B, SWA_HEADS, SWA_HD, w_buf), (0, 3, 1, 2))[None]
    return (y_p.reshape(B, T, D), y_s.reshape(Bs, Ts, D),
            jnp.swapaxes(st_p, -1, -2)[None], kv_p(kT_p), kv_p(vT_p),
            mk_p.reshape(1, B, MEM_LEN, X_HEADS, X_HD), mv_p.reshape(1, B, MEM_LEN, X_HEADS, X_HD),
            jnp.swapaxes(st_s, -1, -2)[None],
            kb_s.reshape(1, Bs, P, SWA_HEADS, SWA_HD), vb_s.reshape(1, Bs, P, SWA_HEADS, SWA_HD))
```

```python
import functools

import numpy as np
import jax
import jax.numpy as jnp
from jax import lax
from jax.experimental import pallas as pl
from jax.experimental.compute_on import compute_on
from jax.experimental.pallas import tpu as pltpu

F32 = jnp.float32
BF16 = jnp.bfloat16

D_MODEL = 1024
GLA_HEADS = 4
GLA_DK = 64
GLA_DV = 128
GLA_RANK = 16
GLA_TAU = 16.0
GLA_CHUNK = 64
GLA_SUB = 16
SWA_HEADS = 8
SWA_HD = 64
DILATED_PATTERNS = ((128, 1), (512, 4), (2048, 16))
BAND = 128
SWA_WINDOW = 2048
MEM_LEN = 256
X_HEADS = 4
X_HD = 128
N_EXPERTS = 32
TOP_K = 4
SWIGLU_ALPHA = 1.702
SWIGLU_LIMIT = 7.0
EPS = 1e-6

GLA_QK_W = GLA_HEADS * GLA_DK
GLA_V_W = GLA_HEADS * GLA_DV
SWA_W = SWA_HEADS * SWA_HD
X_W = X_HEADS * X_HD
LANES = 128
NEG = -1e30
VMEM_LIMIT = 56 * 1024 * 1024

_C_GQ, _C_GK, _C_GV, _C_GR, _C_SQ, _C_SK, _C_SV, _C_GA = 0, 256, 512, 1024, 1536, 2048, 2560, 3072
D_IN_PAD = 3200

MOE_TM = 512


def _cparams(*sem):
    return pltpu.CompilerParams(dimension_semantics=sem, vmem_limit_bytes=VMEM_LIMIT)


def _dot(a, b):
    return jnp.dot(a, b, preferred_element_type=F32)


def _dot_nt(a, b):
    return lax.dot_general(a, b, (((1,), (1,)), ((), ())), preferred_element_type=F32)


def _rms_rows(x, g):
    return x * lax.rsqrt(jnp.mean(x * x, axis=-1, keepdims=True) + EPS) * g


def _split_bf16(x, n):
    out = []
    for _ in range(n - 1):
        hi = x.astype(BF16)
        out.append(hi)
        x = x - hi.astype(F32)
    out.append(x.astype(BF16))
    return out


def _group_rms(z, g, ones_bd, group):
    hi, lo = _split_bf16(z * z, 2)
    ss = _dot(hi, ones_bd) + _dot(lo, ones_bd)
    return z * lax.rsqrt(ss * (1.0 / group) + EPS) * g


def _log_sigmoid(x):
    return jnp.minimum(x, 0.0) - jnp.log1p(jnp.exp(-jnp.abs(x)))


def _sigmoid(x):
    return 1.0 / (1.0 + jnp.exp(-x))


CLASS_DILS = tuple(d for _, d in DILATED_PATTERNS if d > 1)


def _in_proj_kernel(x_ref, gmix_ref, w_ref, wa2_ref, ba_ref, gsq_ref, gsk_ref, bd_ref,
                    oq_ref, ok_ref, ov_ref, or_ref, ola_ref, osq_ref, osk_ref, osv_ref, *rest, classes, seq_tiles,
                    win_tiles):
    h = _rms_rows(x_ref[...], gmix_ref[...]).astype(BF16)

    def proj(lo, width):
        return _dot(h, w_ref[:, lo:lo + width])

    oq_ref[...] = proj(_C_GQ, GLA_QK_W) * (GLA_DK ** -0.5)
    ok_ref[...] = proj(_C_GK, GLA_QK_W)
    ov_ref[...] = proj(_C_GV, GLA_V_W)
    or_ref[...] = proj(_C_GR, GLA_V_W)
    osv_ref[...] = proj(_C_SV, SWA_W)
    ga = proj(_C_GA, LANES)
    xa = _dot(ga.astype(BF16), wa2_ref[...]) + ba_ref[...]
    ola_ref[...] = _log_sigmoid(xa) * (1.0 / GLA_TAU)
    bd = bd_ref[...]
    osq_ref[...] = _group_rms(proj(_C_SQ, SWA_W), gsq_ref[...], bd, SWA_HD)
    osk_ref[...] = _group_rms(proj(_C_SK, SWA_W), gsk_ref[...], bd, SWA_HD)

    if classes:
        kT_ref, vT_ref = rest[-3], rest[-2]
        rest = rest[:-3] + rest[-1:]

        @pl.when(pl.program_id(0) % seq_tiles >= seq_tiles - win_tiles)
        def _():
            kT_ref[...] = osk_ref[...].T
            vT_ref[...] = osv_ref[...].T

        cls_refs, col_scr = rest[:-1], rest[-1]
        tm = x_ref.shape[0]
        n_col = SWA_W // LANES
        for j, src in enumerate((osq_ref, osk_ref, osv_ref)):
            for g in range(n_col):
                col_scr[g] = src[:, g * LANES:(g + 1) * LANES]
            for c, d in enumerate(CLASS_DILS):
                dst = cls_refs[j * len(CLASS_DILS) + c]
                for r in range(d):
                    for g in range(n_col):
                        lo = r * SWA_W + g * LANES
                        dst[:, lo:lo + LANES] = col_scr[g, pl.ds(r, tm // d, stride=d), :].astype(BF16)


def _in_proj(x, gmix, w_in_p, wa2_p, ba, gsq, gsk, bd, tm, classes, seq_len=0, win_len=0):
    n = x.shape[0]
    row = lambda w: pl.BlockSpec((tm, w), lambda i: (i, 0))
    full = lambda a: pl.BlockSpec(a.shape, lambda i: (0,) * a.ndim)
    widths = (GLA_QK_W, GLA_QK_W, GLA_V_W, GLA_V_W, GLA_QK_W, SWA_W, SWA_W, SWA_W)
    out_specs = [row(w) for w in widths]
    out_shape = [jax.ShapeDtypeStruct((n, w), F32) for w in widths]
    seq_tiles, win_tiles = seq_len // tm, win_len // tm
    if classes:
        for _ in range(3):
            for d in CLASS_DILS:
                out_specs.append(pl.BlockSpec((tm // d, d * SWA_W), lambda i: (i, 0)))
                out_shape.append(jax.ShapeDtypeStruct((n // d, d * SWA_W), BF16))
        win = pl.BlockSpec((None, SWA_W, tm), lambda i: (i // seq_tiles, 0,
                                                         jnp.maximum(i % seq_tiles - (seq_tiles - win_tiles), 0)))
        out_specs += [win, win]
        out_shape += [jax.ShapeDtypeStruct((n // seq_len, SWA_W, win_len), F32)] * 2
    return pl.pallas_call(
        functools.partial(_in_proj_kernel, classes=classes, seq_tiles=seq_tiles, win_tiles=win_tiles),
        grid=(n // tm,),
        in_specs=[row(D_MODEL), full(gmix), full(w_in_p), full(wa2_p), full(ba), full(gsq), full(gsk), full(bd)],
        out_specs=out_specs,
        out_shape=out_shape,
        scratch_shapes=[pltpu.VMEM((SWA_W // LANES, tm, LANES), F32)] if classes else [],
        compiler_params=_cparams("arbitrary"),
        name="in_proj",
    )(x, gmix, w_in_p, wa2_p, ba, gsq, gsk, bd)


GLA_NB = 2


def _gla_kernel(q_ref, k_ref, v_ref, la_ref, r_ref, s0t_ref, gout_ref, lcat_ref, qmask_ref,
                o_ref, st_ref, *st_scr, n_chunks):
    tb = pl.program_id(1)
    C, S, NS = GLA_CHUNK, GLA_SUB, GLA_CHUNK // GLA_SUB

    @pl.when(tb == 0)
    def _():
        for n in range(GLA_NB):
            for h in range(GLA_HEADS):
                st_scr[n * GLA_HEADS + h][...] = s0t_ref[n, h]

    lcat = lcat_ref[...]
    qmask = qmask_ref[...]
    gout = gout_ref[...]
    row = lax.broadcasted_iota(jnp.int32, (C, GLA_DK), 0)
    tril = lax.broadcasted_iota(jnp.int32, (C, C), 0) >= lax.broadcasted_iota(jnp.int32, (C, C), 1)

    def chunk(c, carry):
        r0 = pl.multiple_of(c * C, C)
        rows = pl.ds(r0, C)
        for n in range(GLA_NB):
            g3 = _split_bf16(la_ref[n, rows, :], 3)
            b_r = _dot(lcat, g3[0]) + _dot(lcat, g3[1]) + _dot(lcat, g3[2])
            b = b_r[:C]
            ref = b_r[C:]
            b_end = b[C - 1:C, :]
            q = q_ref[n, rows, :]
            k = k_ref[n, rows, :]
            q_hat = q * jnp.exp(b - ref)
            q_til = q * jnp.exp(b)
            k_dec = k * jnp.exp(b_end - b)
            a_end = jnp.exp(b_end)
            for h in range(GLA_HEADS):
                sl = slice(h * GLA_DK, (h + 1) * GLA_DK)
                vl = slice(h * GLA_DV, (h + 1) * GLA_DV)
                b_h, ref_h, k_h = b[:, sl], ref[:, sl], k[:, sl]
                pieces = []
                for i in range(NS):
                    e = ref_h[S * i:S * i + 1, :] - b_h
                    e = jnp.where(row < S * (i + 1), e, NEG)
                    pieces.append(k_h * jnp.exp(e))
                k_big = jnp.concatenate(pieces, axis=1).astype(BF16)
                q_big = (jnp.concatenate([q_hat[:, sl]] * NS, axis=1) * qmask).astype(BF16)
                sc = jnp.where(tril, _dot_nt(q_big, k_big), 0.0)
                v_h = v_ref[n, rows, vl]
                st_nh = st_scr[n * GLA_HEADS + h]
                st = st_nh[...]
                o = _dot(sc.astype(BF16), v_h.astype(BF16)) + _dot_nt(q_til[:, sl].astype(BF16), st.astype(BF16))
                st_nh[...] = st * a_end[:, sl] + _dot(v_h.T.astype(BF16), k_dec[:, sl].astype(BF16))
                gate = r_ref[n, rows, vl]
                o_ref[n, rows, vl] = (_rms_rows(o, gout) * (gate * _sigmoid(gate))).astype(o_ref.dtype)
        return carry

    lax.fori_loop(0, n_chunks, chunk, 0)

    @pl.when(tb == pl.num_programs(1) - 1)
    def _():
        for n in range(GLA_NB):
            for h in range(GLA_HEADS):
                st_ref[n, h] = st_scr[n * GLA_HEADS + h][...]


def _gla_consts():
    C, S = GLA_CHUNK, GLA_SUB
    t = np.arange(C)
    incl = (t[None, :] <= t[:, None]).astype(np.float32)
    upto = (t[None, :] < (t[:, None] // S) * S).astype(np.float32)
    lcat = np.concatenate([incl, upto], axis=0)
    col = np.arange((C // S) * GLA_DK)
    qmask = ((t[:, None] // S) == (col[None, :] // GLA_DK)).astype(np.float32)
    return jnp.asarray(lcat, BF16), jnp.asarray(qmask, F32)


def _gla(q, k, v, la, r, s0t, gout, tb):
    B, T, _ = q.shape
    lcat, qmask = _gla_consts()
    blk = lambda w: pl.BlockSpec((GLA_NB, tb, w), lambda b, t: (b, t, 0))
    full = lambda a: pl.BlockSpec(a.shape, lambda b, t: (0,) * a.ndim)
    st_spec = pl.BlockSpec((GLA_NB, GLA_HEADS, GLA_DV, GLA_DK), lambda b, t: (b, 0, 0, 0))
    return pl.pallas_call(
        functools.partial(_gla_kernel, n_chunks=tb // GLA_CHUNK),
        grid=(B // GLA_NB, T // tb),
        in_specs=[blk(GLA_QK_W), blk(GLA_QK_W), blk(GLA_V_W), blk(GLA_QK_W), blk(GLA_V_W), st_spec,
                  full(gout), full(lcat), full(qmask)],
        out_specs=[blk(GLA_V_W), st_spec],
        out_shape=[jax.ShapeDtypeStruct((B, T, GLA_V_W), BF16),
                   jax.ShapeDtypeStruct((B, GLA_HEADS, GLA_DV, GLA_DK), F32)],
        scratch_shapes=[pltpu.VMEM((GLA_DV, GLA_DK), F32)] * (GLA_NB * GLA_HEADS),
        compiler_params=_cparams("parallel", "arbitrary"),
        name="gla",
    )(q, k, v, la, r, s0t, gout, lcat, qmask)


def _band_attn_kernel(q_ref, kp_ref, kc_ref, vp_ref, vc_ref, o_ref, lse_ref, k_scr, v_scr, *, dil, n_sub, qs):
    i = pl.program_id(2)
    k_scr[0:BAND, :] = kp_ref[...].astype(BF16)
    k_scr[BAND:, :] = kc_ref[...].astype(BF16)
    v_scr[0:BAND, :] = vp_ref[...].astype(BF16)
    v_scr[BAND:, :] = vc_ref[...].astype(BF16)
    nk = qs + BAND
    t = lax.broadcasted_iota(jnp.int32, (qs, nk), 0)
    c = lax.broadcasted_iota(jnp.int32, (qs, nk), 1)
    dist = BAND + t - c
    in_band = (dist >= 0) & (dist <= BAND)
    distf = (dist * dil).astype(F32)
    first = lax.broadcasted_iota(jnp.int32, (nk, LANES), 1) < SWA_HD
    first_q = lax.broadcasted_iota(jnp.int32, (qs, LANES), 1) < SWA_HD
    zero = jnp.zeros((nk, LANES), BF16)
    ones_bd = jnp.concatenate([first, jnp.logical_not(first)], axis=0).astype(BF16)

    def sub_block(j, carry):
        r0 = pl.multiple_of(j * qs, qs)
        rows = pl.ds(r0, qs)
        valid = in_band & ((c >= BAND) | (i > 0) | (j > 0))
        for pr in range(SWA_HEADS // 2):
            cols = slice(pr * LANES, (pr + 1) * LANES)
            q2 = (q_ref[rows, cols].astype(F32) * (SWA_HD ** -0.5)).astype(BF16)
            k2 = k_scr[pl.ds(r0, nk), cols]
            v2 = v_scr[pl.ds(r0, nk), cols]
            k_bd = jnp.concatenate([jnp.where(first, k2, zero), jnp.where(first, zero, k2)], axis=0)
            v_bd = jnp.concatenate([jnp.where(first, v2, zero), jnp.where(first, zero, v2)], axis=0)
            s2 = _dot_nt(q2, k_bd)
            ps, ms = [], []
            for u in range(2):
                s = s2[:, u * nk:(u + 1) * nk] - (2.0 ** -(2 * pr + u + 1)) * distf
                s = jnp.where(valid, s, NEG)
                m = jnp.max(s, axis=-1, keepdims=True)
                ps.append(jnp.exp(s - m).astype(BF16))
                ms.append(m)
            od = _dot(jnp.concatenate(ps, axis=1), jnp.concatenate([v_bd, ones_bd], axis=1))
            den = od[:, LANES:]
            o_ref[rows, cols] = od[:, :LANES] / den
            lse_ref[rows, cols] = jnp.where(first_q, ms[0], ms[1]) + jnp.log(den)
        return carry

    lax.fori_loop(0, n_sub, sub_block, 0)


def _band_attn(q, k, v, dil):
    B, Tc, dw = q.shape
    W = dw // dil
    qb = min(4 * BAND, Tc)
    qs = min(2 * BAND, Tc)
    n_sub = qb // qs
    cur = pl.BlockSpec((None, qb, W), lambda b, r, i: (b, i, r))
    prev = pl.BlockSpec((None, BAND, W), lambda b, r, i: (b, jnp.maximum(i * (qb // BAND) - 1, 0), r))
    o, lse = pl.pallas_call(
        functools.partial(_band_attn_kernel, dil=dil, n_sub=n_sub, qs=qs),
        grid=(B, dil, Tc // qb),
        in_specs=[cur, prev, cur, prev, cur],
        out_specs=[cur, cur],
        out_shape=[jax.ShapeDtypeStruct((B, Tc, dw), F32)] * 2,
        scratch_shapes=[pltpu.VMEM((BAND + qb, W), BF16)] * 2,
        compiler_params=_cparams("parallel", "parallel", "parallel"),
        name="band_attn_d%d" % dil,
    )(q, k, k, v, v)
    return o.reshape(B * Tc, dw), lse.reshape(B * Tc, dw)


def _pattern_count(dist):
    cnt = jnp.zeros(dist.shape, F32)
    for window, dil in DILATED_PATTERNS:
        hit = (dist >= 0) & (dist <= window)
        if dil > 1:
            hit = hit & ((dist & (dil - 1)) == 0)
        cnt = cnt + hit.astype(F32)
    return cnt


def _samp_attn_kernel(q_ref, kn_ref, vn_ref, kp_ref, vp_ref, o_ref, ko_ref, vo_ref):
    P, Tn = kp_ref.shape[2], kn_ref.shape[2]
    qi = lax.broadcasted_iota(jnp.int32, (Tn, P), 0)
    dist_p = P + qi - lax.broadcasted_iota(jnp.int32, (Tn, P), 1)
    dist_n = lax.broadcasted_iota(jnp.int32, (Tn, Tn), 0) - lax.broadcasted_iota(jnp.int32, (Tn, Tn), 1)
    cnt_p, cnt_n = _pattern_count(dist_p), _pattern_count(dist_n)
    dpf, dnf = dist_p.astype(F32), dist_n.astype(F32)
    q = (q_ref[...] * (SWA_HD ** -0.5)).astype(BF16)
    outs = []
    for h in range(SWA_HEADS):
        kp, vp, kn, vn = kp_ref[h], vp_ref[h], kn_ref[h], vn_ref[h]
        ko_ref[h] = pltpu.roll(kp, P - Tn, axis=1)
        ko_ref[h, :, P - Tn:] = kn
        vo_ref[h] = pltpu.roll(vp, P - Tn, axis=1)
        vo_ref[h, :, P - Tn:] = vn
        qh = q[:, h * SWA_HD:(h + 1) * SWA_HD]
        slope = 2.0 ** -(h + 1)
        lp = jnp.where(cnt_p > 0, _dot(qh, kp.astype(BF16)) - slope * dpf, NEG)
        ln = jnp.where(cnt_n > 0, _dot(qh, kn.astype(BF16)) - slope * dnf, NEG)
        m = jnp.maximum(jnp.max(lp, axis=-1, keepdims=True), jnp.max(ln, axis=-1, keepdims=True))
        pp = cnt_p * jnp.exp(lp - m)
        pn = cnt_n * jnp.exp(ln - m)
        den = jnp.sum(pp, axis=-1, keepdims=True) + jnp.sum(pn, axis=-1, keepdims=True)
        outs.append((_dot_nt(pp.astype(BF16), vp.astype(BF16)) + _dot_nt(pn.astype(BF16), vn.astype(BF16))) / den)
    o_ref[...] = jnp.concatenate(outs, axis=1)


def _samp_attn(q, knT, vnT, kpT, vpT):
    B, Tn, W = q.shape
    P = kpT.shape[-1]
    qs = pl.BlockSpec((None, Tn, W), lambda b: (b, 0, 0))
    new = pl.BlockSpec((None, SWA_HEADS, SWA_HD, Tn), lambda b: (b, 0, 0, 0))
    past = pl.BlockSpec((None, SWA_HEADS, SWA_HD, P), lambda b: (b, 0, 0, 0))
    return pl.pallas_call(
        _samp_attn_kernel,
        grid=(B,),
        in_specs=[qs, new, new, past, past],
        out_specs=[qs, past, past],
        out_shape=[jax.ShapeDtypeStruct((B, Tn, W), F32), jax.ShapeDtypeStruct(kpT.shape, F32),
                   jax.ShapeDtypeStruct(kpT.shape, F32)],
        compiler_params=_cparams("parallel"),
        name="samp_attn",
    )(q, knT, vnT, kpT, vpT)


def _mem_kv_kernel(mem_ref, gmem_ref, wk_ref, wv_ref, gxk_ref, mk_ref, mv_ref):
    hm = _rms_rows(mem_ref[...], gmem_ref[...]).astype(BF16)
    mk = _dot(hm, wk_ref[...])
    gxk = gxk_ref[...]
    for h in range(X_HEADS):
        sl = slice(h * X_HD, (h + 1) * X_HD)
        mk_ref[:, sl] = _rms_rows(mk[:, sl], gxk)
    mv_ref[...] = _dot(hm, wv_ref[...])


def _mem_kv(mem, gmem, wk, wv, gxk):
    n = mem.shape[0]
    return pl.pallas_call(
        _mem_kv_kernel,
        out_shape=[jax.ShapeDtypeStruct((n, X_W), F32)] * 2,
        compiler_params=pltpu.CompilerParams(vmem_limit_bytes=VMEM_LIMIT),
        name="mem_kv",
    )(mem, gmem, wk, wv, gxk)


def _post_kernel(*refs, n_pat, n_seg):
    n_swa = 2 * n_pat if n_pat > 1 else 1
    x_ref, og_ref = refs[0], refs[1]
    swa_refs = refs[2:2 + n_swa]
    (wout_ref, gx_ref, wxq_ref, gxq_ref, mk_ref, mv_ref, wxo_ref, gffn_ref, wr_ref, br_ref, tri_ref, cnt0_ref,
     x2_ref, h3_ref, ti_ref, tg_ref, cnt_ref, ox_scr, run_scr) = refs[2 + n_swa:2 + n_swa + 19]

    @pl.when(pl.program_id(0) == 0)
    def _():
        run_scr[...] = cnt0_ref[...]

    if n_pat > 1:
        dei_scr = refs[-1]
        tm = x_ref.shape[0]
        vals = []
        for a, r in enumerate(swa_refs):
            d = DILATED_PATTERNS[a % n_pat][1]
            if d == 1:
                vals.append(r[...])
            else:
                slot, n_col = len(vals), SWA_W // LANES
                for c in range(d):
                    for g in range(n_col):
                        lo = c * SWA_W + g * LANES
                        dei_scr[slot, g, pl.ds(c, tm // d, stride=d), :] = r[:, lo:lo + LANES]
                vals.append(jnp.concatenate([dei_scr[slot, g] for g in range(n_col)], axis=1))
        o_p, l_p = vals[:n_pat], vals[n_pat:]
        lmax = functools.reduce(jnp.maximum, l_p)
        w_p = [jnp.exp(l - lmax) for l in l_p]
        o_swa = sum(w * o for w, o in zip(w_p, o_p)) / sum(w_p)
    else:
        o_swa = swa_refs[0][...]

    x1 = x_ref[...] + _dot(og_ref[...], wout_ref[0:GLA_V_W, :]) + _dot(o_swa.astype(BF16), wout_ref[GLA_V_W:, :])

    q = _dot(_rms_rows(x1, gx_ref[...]).astype(BF16), wxq_ref[...])
    gxq = gxq_ref[...]
    seg = x1.shape[0] // n_seg
    for h in range(X_HEADS):
        sl = slice(h * X_HD, (h + 1) * X_HD)
        qn = _rms_rows(q[:, sl], gxq).astype(BF16)
        for s in range(n_seg):
            rows = slice(s * seg, (s + 1) * seg)
            sc = _dot_nt(qn[rows], mk_ref[s, :, sl].astype(BF16)) * (X_HD ** -0.5)
            p = jnp.exp(sc - jnp.max(sc, axis=-1, keepdims=True))
            p = p / jnp.sum(p, axis=-1, keepdims=True)
            ox_scr[rows, sl] = _dot(p.astype(BF16), mv_ref[s, :, sl].astype(BF16))
    x2 = x1 + _dot(ox_scr[...].astype(BF16), wxo_ref[...])
    x2_ref[...] = x2

    h3 = _rms_rows(x2, gffn_ref[...]).astype(BF16)
    h3_ref[...] = h3
    work = _dot(h3, wr_ref[...]) + br_ref[...]
    lane = lax.broadcasted_iota(jnp.int32, work.shape, 1)
    vals, idxs = [], []
    for _ in range(TOP_K):
        m = jnp.max(work, axis=-1, keepdims=True)
        idx = jnp.min(jnp.where(work == m, lane, LANES), axis=-1, keepdims=True)
        vals.append(m)
        idxs.append(idx)
        work = jnp.where(lane == idx, -jnp.inf, work)
    es = [jnp.exp(v - vals[0]) for v in vals]
    den = sum(es)
    sel = [lane == idx for idx in idxs]
    onehot = functools.reduce(jnp.logical_or, sel).astype(BF16)
    run = run_scr[...]
    rank = _dot(tri_ref[...], onehot) + run
    run_scr[...] = run + jnp.sum(onehot.astype(F32), axis=0, keepdims=True)
    cnt_ref[...] = run_scr[...]
    ti = jnp.zeros(work.shape, jnp.int32)
    tg = jnp.zeros(work.shape, F32)
    for j in range(TOP_K):
        pos = jnp.sum(jnp.where(sel[j], rank, 0.0), axis=-1, keepdims=True).astype(jnp.int32)
        ti = jnp.where(lane == j, idxs[j], ti)
        ti = jnp.where(lane == TOP_K + j, pos, ti)
        tg = jnp.where(lane == j, es[j] / den, tg)
    ti_ref[...] = ti
    tg_ref[...] = tg


def _post(x, og, swa, wout, gx, wxq, gxq, mk, mv, wxo, gffn, wr, br, cnt0, tm, n_seg, rows_per_mem):
    n = x.shape[0]
    n_pat = len(swa) // 2 if len(swa) > 1 else 1
    tri = jnp.asarray(np.tril(np.ones((tm, tm), np.float32), -1), BF16)
    row = lambda w: pl.BlockSpec((tm, w), lambda i: (i, 0))
    full = lambda a: pl.BlockSpec(a.shape, lambda i: (0,) * a.ndim)
    mem = pl.BlockSpec((n_seg, MEM_LEN, X_W), lambda i: ((i * tm) // (rows_per_mem * n_seg), 0, 0))
    swa_specs = [pl.BlockSpec((tm * SWA_W // a.shape[1], a.shape[1]), lambda i: (i, 0)) for a in swa]
    scratch = [pltpu.VMEM((tm, X_W), F32), pltpu.VMEM((1, LANES), F32)]
    if n_pat > 1:
        scratch.append(pltpu.VMEM((len(swa), SWA_W // LANES, tm, LANES), F32))
    return pl.pallas_call(
        functools.partial(_post_kernel, n_pat=n_pat, n_seg=n_seg),
        grid=(n // tm,),
        in_specs=[row(D_MODEL), row(GLA_V_W)] + swa_specs
        + [full(wout), full(gx), full(wxq), full(gxq), mem, mem, full(wxo), full(gffn), full(wr), full(br),
           full(tri), full(cnt0)],
        out_specs=[row(D_MODEL), row(D_MODEL), row(LANES), row(LANES), full(cnt0)],
        out_shape=[jax.ShapeDtypeStruct((n, D_MODEL), F32), jax.ShapeDtypeStruct((n, D_MODEL), BF16),
                   jax.ShapeDtypeStruct((n, LANES), jnp.int32), jax.ShapeDtypeStruct((n, LANES), F32),
                   jax.ShapeDtypeStruct((1, LANES), F32)],
        scratch_shapes=scratch,
        compiler_params=_cparams("arbitrary"),
        name="post",
    )(x, og, *swa, wout, gx, wxq, gxq, mk, mv, wxo, gffn, wr, br, tri, cnt0)


def _moe_kernel(te_ref, nu_ref, x_ref, wg_ref, bg_ref, wu_ref, bu_ref, wd_ref, bd_ref, *rest):
    y_ref, wg_scr, wu_scr, wd_scr = rest[-4:]
    i = pl.program_id(0)
    live = i < nu_ref[0]

    @pl.when(live & ((i == 0) | (te_ref[i] != te_ref[jnp.maximum(i - 1, 0)])))
    def _():
        wg_scr[...] = wg_ref[...].astype(BF16)
        wu_scr[...] = wu_ref[...].astype(BF16)
        wd_scr[...] = wd_ref[...].astype(BF16)

    @pl.when(live)
    def _():
        x = x_ref[...]
        g = jnp.minimum(_dot(x, wg_scr[...]) + bg_ref[...], SWIGLU_LIMIT)
        u = jnp.clip(_dot(x, wu_scr[...]) + bu_ref[...], -SWIGLU_LIMIT, SWIGLU_LIMIT)
        a = g * _sigmoid(SWIGLU_ALPHA * g) * (u + 1.0)
        y_ref[...] = (_dot(a.astype(BF16), wd_scr[...]) + bd_ref[...]).astype(y_ref.dtype)

    @pl.when(jnp.logical_not(live))
    def _():
        y_ref[...] = jnp.zeros(y_ref.shape, y_ref.dtype)


MOE_CHUNKS = 3


def _moe(tile_expert, n_used, h3, src_tok, wg, bg, wu, bu, wd, bd):
    P = src_tok.shape[0]
    tm = MOE_TM
    n_tiles = P // tm
    ys = None
    for c in range(MOE_CHUNKS):
        lo, hi = n_tiles * c // MOE_CHUNKS, n_tiles * (c + 1) // MOE_CHUNKS
        w_spec = pl.BlockSpec((None, D_MODEL, D_MODEL), lambda i, te, nu: (te[i], 0, 0))
        b_spec = pl.BlockSpec((None, 1, D_MODEL), lambda i, te, nu: (te[i], 0, 0))
        prior = [] if ys is None else [ys]
        ys = pl.pallas_call(
            _moe_kernel,
            grid_spec=pltpu.PrefetchScalarGridSpec(
                num_scalar_prefetch=2,
                grid=(hi - lo,),
                in_specs=[pl.BlockSpec((tm, D_MODEL), lambda i, te, nu: (i, 0)),
                          w_spec, b_spec, w_spec, b_spec, w_spec, b_spec]
                + [pl.BlockSpec(memory_space=pl.ANY)] * len(prior),
                out_specs=pl.BlockSpec((tm, D_MODEL), lambda i, te, nu, lo=lo: (i + lo, 0)),
                scratch_shapes=[pltpu.VMEM((D_MODEL, D_MODEL), BF16)] * 3,
            ),
            out_shape=jax.ShapeDtypeStruct((P, D_MODEL), BF16),
            input_output_aliases={9: 0} if prior else {},
            compiler_params=_cparams("arbitrary"),
            name="moe",
        )(tile_expert[lo:hi], n_used - lo, _sc_take(h3, src_tok[lo * tm:hi * tm]), wg, bg, wu, bu, wd, bd, *prior)
    return ys


def _route(top_i, rank, counts, tm):
    n = top_i.shape[0]
    a = n * TOP_K
    n_tiles = a // tm + N_EXPERTS
    tiles_e = (counts + tm - 1) // tm
    tile_end = jnp.cumsum(tiles_e)
    slot0 = (tile_end - tiles_e) * tm
    experts = jnp.arange(N_EXPERTS, dtype=jnp.int32)
    dest = rank + jnp.sum(jnp.where(top_i[..., None] == experts, slot0, 0), axis=-1)
    n_used = tile_end[-1:].astype(jnp.int32)
    tile_expert = jnp.minimum(
        jnp.sum((tile_end[None, :] <= jnp.arange(n_tiles, dtype=jnp.int32)[:, None]).astype(jnp.int32), axis=1),
        N_EXPERTS - 1)
    pad_tok = -1 - (jnp.arange(n_tiles * tm, dtype=jnp.int32) % n)
    src_tok = pad_tok.at[dest.reshape(a)].max(jnp.arange(a, dtype=jnp.int32) // TOP_K, unique_indices=True)
    src_tok = jnp.where(src_tok < 0, -1 - src_tok, src_tok)
    return tile_expert, n_used, src_tok, dest


def _combine_kernel(x_ref, y_ref, g_ref, *rest):
    o_ref = rest[-1]
    acc = x_ref[...]
    g = g_ref[...]
    for j in range(TOP_K):
        acc = acc + g[:, j:j + 1] * y_ref[j].astype(F32)
    o_ref[...] = acc


COMBINE_CHUNKS = 4


def _combine(x2, ysg, tg, row0, tm, prior):
    n = ysg.shape[1]
    off = row0 // tm
    return pl.pallas_call(
        _combine_kernel,
        grid=(n // tm,),
        in_specs=[pl.BlockSpec((tm, D_MODEL), lambda i: (i + off, 0)),
                  pl.BlockSpec((TOP_K, tm, D_MODEL), lambda i: (0, i, 0)),
                  pl.BlockSpec((tm, LANES), lambda i: (i + off, 0))] + [pl.BlockSpec(memory_space=pl.ANY)] * len(prior),
        out_specs=pl.BlockSpec((tm, D_MODEL), lambda i: (i + off, 0)),
        out_shape=jax.ShapeDtypeStruct(x2.shape, F32),
        input_output_aliases={3: 0} if prior else {},
        compiler_params=_cparams("parallel"),
        name="combine",
    )(x2, ysg, tg, *prior)


@compute_on("tpu_sparsecore")
@jax.jit
def _sc_take(x, idx):
    return jnp.take(x, idx, axis=0, mode="clip")


def _mixer_inputs(x2d, w, tm, classes, seq_len=0, win_len=0):
    return _in_proj(x2d, w["gmix"], w["w_in_p"], w["wa2_p"], w["ba"], w["gsq"], w["gsk"], w["bd"], tm, classes,
                    seq_len, win_len)


def kernel(x_prompt, x_sample, mem_prompt, state_gla, cache_swa_k, cache_swa_v, cache_mem_k, cache_mem_v, g_mix, w_in, w_gla_a2, b_gla_a, g_gla_out, g_swa_q, g_swa_k, w_out, g_mem, w_mk, w_mv, g_xk, g_xattn, w_xq, g_xq, w_xo, g_ffn, w_router, b_router, w_gate, b_gate, w_up, b_up, w_down, b_down):
    B, T, D = x_prompt.shape
    Bs, Ts, _ = x_sample.shape
    P = cache_swa_k.shape[2]
    l = 0

    wi = w_in[l]
    segs = np.cumsum((0, GLA_QK_W, GLA_QK_W, GLA_V_W, GLA_V_W, GLA_RANK, SWA_W, SWA_W, SWA_W))
    gq_c, gk_c, gv_c, gr_c, ga_c, sq_c, sk_c, sv_c = [wi[:, segs[j]:segs[j + 1]] for j in range(8)]
    w_in_p = jnp.concatenate(
        [gq_c, gk_c, gv_c, gr_c, sq_c, sk_c, sv_c, ga_c, jnp.zeros((D, D_IN_PAD - _C_GA - GLA_RANK), F32)],
        axis=1).astype(BF16)
    heads = np.arange(SWA_W) // SWA_HD
    w = dict(
        gmix=g_mix[l][None], w_in_p=w_in_p,
        wa2_p=jnp.concatenate([w_gla_a2[l], jnp.zeros((LANES - GLA_RANK, GLA_QK_W), F32)], axis=0).astype(BF16),
        ba=b_gla_a[l][None],
        gsq=jnp.tile(g_swa_q[l], SWA_HEADS)[None], gsk=jnp.tile(g_swa_k[l], SWA_HEADS)[None],
        bd=jnp.asarray(heads[:, None] == heads[None, :], BF16),
    )
    gout = g_gla_out[l][None]
    wout = w_out[l].astype(BF16)
    wxq, wxo = w_xq[l].astype(BF16), w_xo[l].astype(BF16)
    wr = jnp.concatenate([w_router[l], jnp.zeros((D, LANES - N_EXPERTS), F32)], axis=1).astype(BF16)
    br = jnp.concatenate([b_router[l], jnp.full((LANES - N_EXPERTS,), NEG, F32)])[None]
    post_w = (wout, g_xattn[l][None], wxq, g_xq[l][None])
    post_w2 = (wxo, g_ffn[l][None], wr, br)

    w_buf = min(SWA_WINDOW, T)
    gq, gk, gv, gr, la, sq, sk, sv, *cls, kT_p, vT_p = _mixer_inputs(x_prompt.reshape(B * T, D), w, 512, True, T, w_buf)
    r3 = lambda a: a.reshape(B, T, a.shape[-1])
    og_p, st_p = _gla(r3(gq), r3(gk), r3(gv), r3(la), r3(gr),
                      jnp.zeros((B, GLA_HEADS, GLA_DV, GLA_DK), F32), gout, 512)
    o_pat, l_pat = [], []
    for _, dil in DILATED_PATTERNS:
        if dil == 1:
            qkv = [r3(sq), r3(sk), r3(sv)]
        else:
            c = CLASS_DILS.index(dil)
            qkv = [cls[j * len(CLASS_DILS) + c].reshape(B, T // dil, dil * SWA_W) for j in range(3)]
        o_c, l_c = _band_attn(*qkv, dil)
        o_pat.append(o_c)
        l_pat.append(l_c)
    mk_p, mv_p = _mem_kv(mem_prompt.reshape(B * MEM_LEN, D), g_mem[l][None], w_mk[l].astype(BF16),
                         w_mv[l].astype(BF16), g_xk[l][None])
    x2_p, h3_p, ti_p, tg_p, cnt_p = _post(x_prompt.reshape(B * T, D), og_p.reshape(B * T, GLA_V_W), o_pat + l_pat,
                                          *post_w, mk_p.reshape(B, MEM_LEN, X_W), mv_p.reshape(B, MEM_LEN, X_W),
                                          *post_w2, jnp.zeros((1, LANES), F32), tm=256, n_seg=1, rows_per_mem=T)

    gq, gk, gv, gr, la, sq, sk, sv = _mixer_inputs(x_sample.reshape(Bs * Ts, D), w, Bs * Ts, False)
    pad = lambda a: jnp.pad(a.reshape(Bs, Ts, a.shape[-1]), ((0, 0), (0, GLA_CHUNK - Ts), (0, 0)))
    og_s, st_s = _gla(pad(gq), pad(gk), pad(gv), pad(la), pad(gr),
                      jnp.swapaxes(state_gla[l], -1, -2), gout, GLA_CHUNK)
    og_s = og_s[:, :Ts].reshape(Bs * Ts, GLA_V_W)
    rows_minor = lambda a: jnp.transpose(a.reshape(Bs, -1, SWA_HEADS, SWA_HD), (0, 2, 3, 1))
    o_swa_s, kb_s, vb_s = _samp_attn(sq.reshape(Bs, Ts, SWA_W), rows_minor(sk), rows_minor(sv),
                                     rows_minor(cache_swa_k[l]), rows_minor(cache_swa_v[l]))
    kb_s, vb_s = jnp.transpose(kb_s, (0, 3, 1, 2)), jnp.transpose(vb_s, (0, 3, 1, 2))
    seqs = 8
    x2_s, h3_s, ti_s, tg_s, cnt = _post(x_sample.reshape(Bs * Ts, D), og_s, [o_swa_s.reshape(Bs * Ts, SWA_W)],
                                        *post_w, cache_mem_k[l].reshape(Bs, MEM_LEN, X_W),
                                        cache_mem_v[l].reshape(Bs, MEM_LEN, X_W),
                                        *post_w2, cnt_p, tm=seqs * Ts, n_seg=seqs, rows_per_mem=Ts)

    n_p = B * T
    h3 = jnp.concatenate([h3_p, h3_s], axis=0)
    ti = jnp.concatenate([ti_p[:, :2 * TOP_K], ti_s[:, :2 * TOP_K]], axis=0)
    tile_expert, n_used, src_tok, dest = _route(ti[:, :TOP_K], ti[:, TOP_K:], cnt[0, :N_EXPERTS].astype(jnp.int32),
                                                MOE_TM)
    ys = _moe(tile_expert, n_used, h3, src_tok, w_gate[l], b_gate[l][:, None, :], w_up[l],
              b_up[l][:, None, :], w_down[l], b_down[l][:, None, :])
    def combined(x2, tg, row0, first, n, tm, prior):
        ysg = _sc_take(ys, dest[first:first + n].T.reshape(-1)).reshape(TOP_K, n, D)
        return [_combine(x2, ysg, tg, row0, tm, prior)]

    y_p, n_c = [], n_p // COMBINE_CHUNKS
    for c in range(COMBINE_CHUNKS):
        y_p = combined(x2_p, tg_p, c * n_c, c * n_c, n_c, 256, y_p)
    y_p, y_s = y_p[0], combined(x2_s, tg_s, 0, n_p, Bs * Ts, Bs * Ts, [])[0]

    kv_p = lambda a: jnp.transpose(a.reshape(B, SWA_HEADS, SWA_HD, w_buf), (0, 3, 1, 2))[None]
    return (y_p.reshape(B, T, D), y_s.reshape(Bs, Ts, D),
            jnp.swapaxes(st_p, -1, -2)[None], kv_p(kT_p), kv_p(vT_p),
            mk_p.reshape(1, B, MEM_LEN, X_HEADS, X_HD), mv_p.reshape(1, B, MEM_LEN, X_HEADS, X_HD),
            jnp.swapaxes(st_s, -1, -2)[None],
            kb_s.reshape(1, Bs, P, SWA_HEADS, SWA_HD), vb_s.reshape(1, Bs, P, SWA_HEADS, SWA_HD))
```

```python
import functools

import numpy as np
import jax
import jax.numpy as jnp
from jax import lax
from jax.experimental import pallas as pl
from jax.experimental.compute_on import compute_on
from jax.experimental.pallas import tpu as pltpu

F32 = jnp.float32
BF16 = jnp.bfloat16

D_MODEL = 1024
GLA_HEADS = 4
GLA_DK = 64
GLA_DV = 128
GLA_RANK = 16
GLA_TAU = 16.0
GLA_CHUNK = 64
GLA_SUB = 16
SWA_HEADS = 8
SWA_HD = 64
DILATED_PATTERNS = ((128, 1), (512, 4), (2048, 16))
BAND = 128
SWA_WINDOW = 2048
MEM_LEN = 256
X_HEADS = 4
X_HD = 128
N_EXPERTS = 32
TOP_K = 4
SWIGLU_ALPHA = 1.702
SWIGLU_LIMIT = 7.0
EPS = 1e-6

GLA_QK_W = GLA_HEADS * GLA_DK
GLA_V_W = GLA_HEADS * GLA_DV
SWA_W = SWA_HEADS * SWA_HD
X_W = X_HEADS * X_HD
LANES = 128
NEG = -1e30
VMEM_LIMIT = 56 * 1024 * 1024

_C_GQ, _C_GK, _C_GV, _C_GR, _C_SQ, _C_SK, _C_SV, _C_GA = 0, 256, 512, 1024, 1536, 2048, 2560, 3072
D_IN_PAD = 3200

MOE_TM = 512


def _cparams(*sem):
    return pltpu.CompilerParams(dimension_semantics=sem, vmem_limit_bytes=VMEM_LIMIT)


def _dot(a, b):
    return jnp.dot(a, b, preferred_element_type=F32)


def _dot_nt(a, b):
    return lax.dot_general(a, b, (((1,), (1,)), ((), ())), preferred_element_type=F32)


def _rms_rows(x, g):
    return x * lax.rsqrt(jnp.mean(x * x, axis=-1, keepdims=True) + EPS) * g


def _split_bf16(x, n):
    out = []
    for _ in range(n - 1):
        hi = x.astype(BF16)
        out.append(hi)
        x = x - hi.astype(F32)
    out.append(x.astype(BF16))
    return out


def _group_rms(z, g, ones_bd, group):
    hi, lo = _split_bf16(z * z, 2)
    ss = _dot(hi, ones_bd) + _dot(lo, ones_bd)
    return z * lax.rsqrt(ss * (1.0 / group) + EPS) * g


def _log_sigmoid(x):
    return jnp.minimum(x, 0.0) - jnp.log1p(jnp.exp(-jnp.abs(x)))


def _sigmoid(x):
    return 1.0 / (1.0 + jnp.exp(-x))


CLASS_DILS = tuple(d for _, d in DILATED_PATTERNS if d > 1)


def _in_proj_kernel(x_ref, gmix_ref, w_ref, wa2_ref, ba_ref, gsq_ref, gsk_ref, bd_ref,
                    oq_ref, ok_ref, ov_ref, or_ref, ola_ref, osq_ref, osk_ref, osv_ref, *rest, classes, seq_tiles,
                    win_tiles):
    h = _rms_rows(x_ref[...], gmix_ref[...]).astype(BF16)

    def proj(lo, width):
        return _dot(h, w_ref[:, lo:lo + width])

    oq_ref[...] = proj(_C_GQ, GLA_QK_W) * (GLA_DK ** -0.5)
    ok_ref[...] = proj(_C_GK, GLA_QK_W)
    ov_ref[...] = proj(_C_GV, GLA_V_W)
    or_ref[...] = proj(_C_GR, GLA_V_W)
    osv_ref[...] = proj(_C_SV, SWA_W)
    ga = proj(_C_GA, LANES)
    xa = _dot(ga.astype(BF16), wa2_ref[...]) + ba_ref[...]
    ola_ref[...] = _log_sigmoid(xa) * (1.0 / GLA_TAU)
    bd = bd_ref[...]
    osq_ref[...] = _group_rms(proj(_C_SQ, SWA_W), gsq_ref[...], bd, SWA_HD)
    osk_ref[...] = _group_rms(proj(_C_SK, SWA_W), gsk_ref[...], bd, SWA_HD)

    if classes:
        kT_ref, vT_ref = rest[-3], rest[-2]
        rest = rest[:-3] + rest[-1:]

        @pl.when(pl.program_id(0) % seq_tiles >= seq_tiles - win_tiles)
        def _():
            kT_ref[...] = osk_ref[...].T
            vT_ref[...] = osv_ref[...].T

        cls_refs, col_scr = rest[:-1], rest[-1]
        tm = x_ref.shape[0]
        n_col = SWA_W // LANES
        for j, src in enumerate((osq_ref, osk_ref, osv_ref)):
            for g in range(n_col):
                col_scr[g] = src[:, g * LANES:(g + 1) * LANES]
            for c, d in enumerate(CLASS_DILS):
                dst = cls_refs[j * len(CLASS_DILS) + c]
                for r in range(d):
                    for g in range(n_col):
                        lo = r * SWA_W + g * LANES
                        dst[:, lo:lo + LANES] = col_scr[g, pl.ds(r, tm // d, stride=d), :].astype(BF16)


def _in_proj(x, gmix, w_in_p, wa2_p, ba, gsq, gsk, bd, tm, classes, seq_len=0, win_len=0):
    n = x.shape[0]
    row = lambda w: pl.BlockSpec((tm, w), lambda i: (i, 0))
    full = lambda a: pl.BlockSpec(a.shape, lambda i: (0,) * a.ndim)
    widths = (GLA_QK_W, GLA_QK_W, GLA_V_W, GLA_V_W, GLA_QK_W, SWA_W, SWA_W, SWA_W)
    out_specs = [row(w) for w in widths]
    out_shape = [jax.ShapeDtypeStruct((n, w), F32) for w in widths]
    seq_tiles, win_tiles = seq_len // tm, win_len // tm
    if classes:
        for _ in range(3):
            for d in CLASS_DILS:
                out_specs.append(pl.BlockSpec((tm // d, d * SWA_W), lambda i: (i, 0)))
                out_shape.append(jax.ShapeDtypeStruct((n // d, d * SWA_W), BF16))
        win = pl.BlockSpec((None, SWA_W, tm), lambda i: (i // seq_tiles, 0,
                                                         jnp.maximum(i % seq_tiles - (seq_tiles - win_tiles), 0)))
        out_specs += [win, win]
        out_shape += [jax.ShapeDtypeStruct((n // seq_len, SWA_W, win_len), F32)] * 2
    return pl.pallas_call(
        functools.partial(_in_proj_kernel, classes=classes, seq_tiles=seq_tiles, win_tiles=win_tiles),
        grid=(n // tm,),
        in_specs=[row(D_MODEL), full(gmix), full(w_in_p), full(wa2_p), full(ba), full(gsq), full(gsk), full(bd)],
        out_specs=out_specs,
        out_shape=out_shape,
        scratch_shapes=[pltpu.VMEM((SWA_W // LANES, tm, LANES), F32)] if classes else [],
        compiler_params=_cparams("arbitrary"),
        name="in_proj",
    )(x, gmix, w_in_p, wa2_p, ba, gsq, gsk, bd)


GLA_NB = 2


def _gla_kernel(q_ref, k_ref, v_ref, la_ref, r_ref, s0_ref, gout_ref, lcat_ref,
                o_ref, s_ref, s_scr, knt_scr, rhs_scr, *, n_chunks):
    tb = pl.program_id(1)
    C, S, NS, H, DK, DV = GLA_CHUNK, GLA_SUB, GLA_CHUNK // GLA_SUB, GLA_HEADS, GLA_DK, GLA_DV

    @pl.when(tb == 0)
    def _():
        s_scr[...] = s0_ref[...]
        knt_scr[...] = jnp.zeros(knt_scr.shape, BF16)
        rhs_scr[...] = jnp.zeros(rhs_scr.shape, BF16)

    lcat = lcat_ref[...]
    gout = gout_ref[...]
    row = lax.broadcasted_iota(jnp.int32, (C, H * DK), 0)
    tril = (lax.broadcasted_iota(jnp.int32, (C, H * C), 1) & (C - 1)) <= lax.broadcasted_iota(jnp.int32, (C, H * C), 0)
    ones_k = jnp.ones((3 * C, LANES), BF16)

    def chunk(c, carry):
        r0 = pl.multiple_of(c * C, C)
        rows = pl.ds(r0, C)
        for n in range(GLA_NB):
            g = la_ref[n, rows, :]
            b_r = _dot(lcat, jnp.concatenate(_split_bf16(g, 3), axis=1))
            b_r = b_r[:, :H * DK] + b_r[:, H * DK:2 * H * DK] + b_r[:, 2 * H * DK:]
            b = b_r[:C]
            ref = b_r[C:]
            b_end = b[C - 1:C, :]
            b_col = _dot(jnp.concatenate(_split_bf16(g.T, 3), axis=1), ones_k)
            q = q_ref[n, rows, :]
            k = k_ref[n, rows, :]
            v = v_ref[n, rows, :]
            q_hat = q * jnp.exp(b - ref)
            q_til = q * jnp.exp(b)
            k_dec = k * jnp.exp(b_end - b)
            for i in range(NS):
                e = ref[S * i:S * i + 1, :] - b
                piece = (k * jnp.exp(jnp.where(row < S * (i + 1), e, NEG))).astype(BF16)
                for h in range(H):
                    lo = i * H * DK + h * DK
                    knt_scr[n, h * C:(h + 1) * C, lo:lo + DK] = piece[:, h * DK:(h + 1) * DK]
            q_cat = jnp.concatenate([jnp.where((row >= S * i) & (row < S * (i + 1)), q_hat, 0.0) for i in range(NS)],
                                    axis=1).astype(BF16)
            sc = jnp.where(tril, _dot_nt(q_cat, knt_scr[n]), 0.0)
            st = s_scr[n]
            for h in range(H):
                vl = slice(h * DV, (h + 1) * DV)
                rhs_scr[n, h * C:(h + 1) * C, vl] = v[:, vl].astype(BF16)
                rhs_scr[n, H * C + h * DK:H * C + (h + 1) * DK, vl] = st[h * DK:(h + 1) * DK, :].astype(BF16)
            o = _dot(jnp.concatenate([sc, q_til], axis=1).astype(BF16), rhs_scr[n])
            kv = _dot(k_dec.T.astype(BF16), v.astype(BF16))
            s_scr[n] = st * jnp.exp(b_col) + jnp.concatenate(
                [kv[h * DK:(h + 1) * DK, h * DV:(h + 1) * DV] for h in range(H)], axis=0)
            gate = r_ref[n, rows, :]
            for h in range(H):
                vl = slice(h * DV, (h + 1) * DV)
                o_ref[n, rows, vl] = (_rms_rows(o[:, vl], gout) * (gate[:, vl] * _sigmoid(gate[:, vl]))).astype(o_ref.dtype)
        return carry

    lax.fori_loop(0, n_chunks, chunk, 0)

    @pl.when(tb == pl.num_programs(1) - 1)
    def _():
        s_ref[...] = s_scr[...]


def _gla_consts():
    C, S = GLA_CHUNK, GLA_SUB
    t = np.arange(C)
    incl = (t[None, :] <= t[:, None]).astype(np.float32)
    upto = (t[None, :] < (t[:, None] // S) * S).astype(np.float32)
    return jnp.asarray(np.concatenate([incl, upto], axis=0), BF16)


def _gla(q, k, v, la, r, s0, gout, tb):
    B, T, _ = q.shape
    H, C, DK, DV, NS = GLA_HEADS, GLA_CHUNK, GLA_DK, GLA_DV, GLA_CHUNK // GLA_SUB
    lcat = _gla_consts()
    blk = lambda w: pl.BlockSpec((GLA_NB, tb, w), lambda b, t: (b, t, 0))
    full = lambda a: pl.BlockSpec(a.shape, lambda b, t: (0,) * a.ndim)
    st_spec = pl.BlockSpec((GLA_NB, H * DK, DV), lambda b, t: (b, 0, 0))
    return pl.pallas_call(
        functools.partial(_gla_kernel, n_chunks=tb // C),
        grid=(B // GLA_NB, T // tb),
        in_specs=[blk(GLA_QK_W), blk(GLA_QK_W), blk(GLA_V_W), blk(GLA_QK_W), blk(GLA_V_W), st_spec,
                  full(gout), full(lcat)],
        out_specs=[blk(GLA_V_W), st_spec],
        out_shape=[jax.ShapeDtypeStruct((B, T, GLA_V_W), BF16),
                   jax.ShapeDtypeStruct((B, H * DK, DV), F32)],
        scratch_shapes=[pltpu.VMEM((GLA_NB, H * DK, DV), F32),
                        pltpu.VMEM((GLA_NB, H * C, NS * H * DK), BF16),
                        pltpu.VMEM((GLA_NB, H * C + H * DK, H * DV), BF16)],
        compiler_params=_cparams("parallel", "arbitrary"),
        name="gla",
    )(q, k, v, la, r, s0, gout, lcat)


def _band_attn_kernel(q_ref, kp_ref, kc_ref, vp_ref, vc_ref, o_ref, lse_ref, k_scr, v_scr, *, dil, n_sub, qs):
    i = pl.program_id(2)
    k_scr[0:BAND, :] = kp_ref[...].astype(BF16)
    k_scr[BAND:, :] = kc_ref[...].astype(BF16)
    v_scr[0:BAND, :] = vp_ref[...].astype(BF16)
    v_scr[BAND:, :] = vc_ref[...].astype(BF16)
    nk = qs + BAND
    t = lax.broadcasted_iota(jnp.int32, (qs, nk), 0)
    c = lax.broadcasted_iota(jnp.int32, (qs, nk), 1)
    dist = BAND + t - c
    in_band = (dist >= 0) & (dist <= BAND)
    distf = (dist * dil).astype(F32)
    first = lax.broadcasted_iota(jnp.int32, (nk, LANES), 1) < SWA_HD
    first_q = lax.broadcasted_iota(jnp.int32, (qs, LANES), 1) < SWA_HD
    zero = jnp.zeros((nk, LANES), BF16)
    ones_bd = jnp.concatenate([first, jnp.logical_not(first)], axis=0).astype(BF16)

    def sub_block(j, carry):
        r0 = pl.multiple_of(j * qs, qs)
        rows = pl.ds(r0, qs)
        valid = in_band & ((c >= BAND) | (i > 0) | (j > 0))
        for pr in range(SWA_HEADS // 2):
            cols = slice(pr * LANES, (pr + 1) * LANES)
            q2 = (q_ref[rows, cols].astype(F32) * (SWA_HD ** -0.5)).astype(BF16)
            k2 = k_scr[pl.ds(r0, nk), cols]
            v2 = v_scr[pl.ds(r0, nk), cols]
            k_bd = jnp.concatenate([jnp.where(first, k2, zero), jnp.where(first, zero, k2)], axis=0)
            v_bd = jnp.concatenate([jnp.where(first, v2, zero), jnp.where(first, zero, v2)], axis=0)
            s2 = _dot_nt(q2, k_bd)
            ps, ms = [], []
            for u in range(2):
                s = s2[:, u * nk:(u + 1) * nk] - (2.0 ** -(2 * pr + u + 1)) * distf
                s = jnp.where(valid, s, NEG)
                m = jnp.max(s, axis=-1, keepdims=True)
                ps.append(jnp.exp(s - m).astype(BF16))
                ms.append(m)
            od = _dot(jnp.concatenate(ps, axis=1), jnp.concatenate([v_bd, ones_bd], axis=1))
            den = od[:, LANES:]
            o_ref[rows, cols] = od[:, :LANES] / den
            lse_ref[rows, cols] = jnp.where(first_q, ms[0], ms[1]) + jnp.log(den)
        return carry

    lax.fori_loop(0, n_sub, sub_block, 0)


def _band_attn(q, k, v, dil):
    B, Tc, dw = q.shape
    W = dw // dil
    qb = min(4 * BAND, Tc)
    qs = min(2 * BAND, Tc)
    n_sub = qb // qs
    cur = pl.BlockSpec((None, qb, W), lambda b, r, i: (b, i, r))
    prev = pl.BlockSpec((None, BAND, W), lambda b, r, i: (b, jnp.maximum(i * (qb // BAND) - 1, 0), r))
    o, lse = pl.pallas_call(
        functools.partial(_band_attn_kernel, dil=dil, n_sub=n_sub, qs=qs),
        grid=(B, dil, Tc // qb),
        in_specs=[cur, prev, cur, prev, cur],
        out_specs=[cur, cur],
        out_shape=[jax.ShapeDtypeStruct((B, Tc, dw), F32)] * 2,
        scratch_shapes=[pltpu.VMEM((BAND + qb, W), BF16)] * 2,
        compiler_params=_cparams("parallel", "parallel", "parallel"),
        name="band_attn_d%d" % dil,
    )(q, k, k, v, v)
    return o.reshape(B * Tc, dw), lse.reshape(B * Tc, dw)


def _pattern_count(dist):
    cnt = jnp.zeros(dist.shape, F32)
    for window, dil in DILATED_PATTERNS:
        hit = (dist >= 0) & (dist <= window)
        if dil > 1:
            hit = hit & ((dist & (dil - 1)) == 0)
        cnt = cnt + hit.astype(F32)
    return cnt


def _samp_attn_kernel(q_ref, kn_ref, vn_ref, kp_ref, vp_ref, o_ref):
    P, Tn = kp_ref.shape[2], kn_ref.shape[2]
    qi = lax.broadcasted_iota(jnp.int32, (Tn, P), 0)
    dist_p = P + qi - lax.broadcasted_iota(jnp.int32, (Tn, P), 1)
    dist_n = lax.broadcasted_iota(jnp.int32, (Tn, Tn), 0) - lax.broadcasted_iota(jnp.int32, (Tn, Tn), 1)
    cnt_p, cnt_n = _pattern_count(dist_p), _pattern_count(dist_n)
    dpf, dnf = dist_p.astype(F32), dist_n.astype(F32)
    q = (q_ref[...] * (SWA_HD ** -0.5)).astype(BF16)
    outs = []
    for h in range(SWA_HEADS):
        kp, vp, kn, vn = kp_ref[h], vp_ref[h], kn_ref[h], vn_ref[h]
        qh = q[:, h * SWA_HD:(h + 1) * SWA_HD]
        slope = 2.0 ** -(h + 1)
        lp = jnp.where(cnt_p > 0, _dot(qh, kp.astype(BF16)) - slope * dpf, NEG)
        ln = jnp.where(cnt_n > 0, _dot(qh, kn.astype(BF16)) - slope * dnf, NEG)
        m = jnp.maximum(jnp.max(lp, axis=-1, keepdims=True), jnp.max(ln, axis=-1, keepdims=True))
        pp = cnt_p * jnp.exp(lp - m)
        pn = cnt_n * jnp.exp(ln - m)
        den = jnp.sum(pp, axis=-1, keepdims=True) + jnp.sum(pn, axis=-1, keepdims=True)
        outs.append((_dot_nt(pp.astype(BF16), vp.astype(BF16)) + _dot_nt(pn.astype(BF16), vn.astype(BF16))) / den)
    o_ref[...] = jnp.concatenate(outs, axis=1)


def _cache_specs(B, Tn, P):
    new = pl.BlockSpec((None, SWA_HEADS, SWA_HD, Tn), lambda b, *_: (b, 0, 0, 0))
    past = pl.BlockSpec((None, SWA_HEADS, SWA_HD, P), lambda b, *_: (b, 0, 0, 0))
    return new, past


def _samp_attn(q, knT, vnT, kpT, vpT):
    B, Tn, W = q.shape
    new, past = _cache_specs(B, Tn, kpT.shape[-1])
    qs = pl.BlockSpec((None, Tn, W), lambda b: (b, 0, 0))
    return pl.pallas_call(
        _samp_attn_kernel,
        grid=(B,),
        in_specs=[qs, new, new, past, past],
        out_specs=qs,
        out_shape=jax.ShapeDtypeStruct((B, Tn, W), F32),
        compiler_params=_cparams("parallel"),
        name="samp_attn",
    )(q, knT, vnT, kpT, vpT)


def _cache_shift_kernel(after_ref, kn_ref, vn_ref, kp_ref, vp_ref, ko_ref, vo_ref):
    P, Tn = kp_ref.shape[2], kn_ref.shape[2]
    for h in range(SWA_HEADS):
        ko_ref[h] = pltpu.roll(kp_ref[h], P - Tn, axis=1)
        ko_ref[h, :, P - Tn:] = kn_ref[h]
        vo_ref[h] = pltpu.roll(vp_ref[h], P - Tn, axis=1)
        vo_ref[h, :, P - Tn:] = vn_ref[h]


def _cache_shift(after, knT, vnT, kpT, vpT):
    B, _, _, Tn = knT.shape
    new, past = _cache_specs(B, Tn, kpT.shape[-1])
    return pl.pallas_call(
        _cache_shift_kernel,
        grid_spec=pltpu.PrefetchScalarGridSpec(num_scalar_prefetch=1, grid=(B,), in_specs=[new, new, past, past],
                                               out_specs=[past, past]),
        out_shape=[jax.ShapeDtypeStruct(kpT.shape, F32)] * 2,
        compiler_params=_cparams("parallel"),
        name="cache_shift",
    )(after, knT, vnT, kpT, vpT)


def _mem_kv_kernel(mem_ref, gmem_ref, wk_ref, wv_ref, gxk_ref, mk_ref, mv_ref):
    hm = _rms_rows(mem_ref[...], gmem_ref[...]).astype(BF16)
    mk = _dot(hm, wk_ref[...])
    gxk = gxk_ref[...]
    for h in range(X_HEADS):
        sl = slice(h * X_HD, (h + 1) * X_HD)
        mk_ref[:, sl] = _rms_rows(mk[:, sl], gxk)
    mv_ref[...] = _dot(hm, wv_ref[...])


def _mem_kv(mem, gmem, wk, wv, gxk):
    n = mem.shape[0]
    return pl.pallas_call(
        _mem_kv_kernel,
        out_shape=[jax.ShapeDtypeStruct((n, X_W), F32)] * 2,
        compiler_params=pltpu.CompilerParams(vmem_limit_bytes=VMEM_LIMIT),
        name="mem_kv",
    )(mem, gmem, wk, wv, gxk)


def _post_kernel(*refs, n_pat, n_seg):
    n_swa = 2 * n_pat if n_pat > 1 else 1
    x_ref, og_ref = refs[0], refs[1]
    swa_refs = refs[2:2 + n_swa]
    (wout_ref, gx_ref, wxq_ref, gxq_ref, mk_ref, mv_ref, wxo_ref, gffn_ref, wr_ref, br_ref, tri_ref,
     cnt0_ref) = refs[2 + n_swa:14 + n_swa]
    n_scr = 3 if n_pat > 1 else 2
    x2_ref, h3_ref, ti_ref, tg_ref, cnt_ref = refs[len(refs) - n_scr - 5:len(refs) - n_scr]
    ox_scr, run_scr = refs[len(refs) - n_scr:len(refs) - n_scr + 2]

    @pl.when(pl.program_id(0) == 0)
    def _():
        run_scr[...] = cnt0_ref[...]

    if n_pat > 1:
        dei_scr = refs[-1]
        tm = x_ref.shape[0]
        vals = []
        for a, r in enumerate(swa_refs):
            d = DILATED_PATTERNS[a % n_pat][1]
            if d == 1:
                vals.append(r[...])
            else:
                slot, n_col = len(vals), SWA_W // LANES
                for c in range(d):
                    for g in range(n_col):
                        lo = c * SWA_W + g * LANES
                        dei_scr[slot, g, pl.ds(c, tm // d, stride=d), :] = r[:, lo:lo + LANES]
                vals.append(jnp.concatenate([dei_scr[slot, g] for g in range(n_col)], axis=1))
        o_p, l_p = vals[:n_pat], vals[n_pat:]
        lmax = functools.reduce(jnp.maximum, l_p)
        w_p = [jnp.exp(l - lmax) for l in l_p]
        o_swa = sum(w * o for w, o in zip(w_p, o_p)) / sum(w_p)
    else:
        o_swa = swa_refs[0][...]

    x1 = x_ref[...] + _dot(og_ref[...], wout_ref[0:GLA_V_W, :]) + _dot(o_swa.astype(BF16), wout_ref[GLA_V_W:, :])

    q = _dot(_rms_rows(x1, gx_ref[...]).astype(BF16), wxq_ref[...])
    gxq = gxq_ref[...]
    seg = x1.shape[0] // n_seg
    for h in range(X_HEADS):
        sl = slice(h * X_HD, (h + 1) * X_HD)
        qn = _rms_rows(q[:, sl], gxq).astype(BF16)
        for s in range(n_seg):
            rows = slice(s * seg, (s + 1) * seg)
            sc = _dot_nt(qn[rows], mk_ref[s, :, sl].astype(BF16)) * (X_HD ** -0.5)
            p = jnp.exp(sc - jnp.max(sc, axis=-1, keepdims=True))
            p = p / jnp.sum(p, axis=-1, keepdims=True)
            ox_scr[rows, sl] = _dot(p.astype(BF16), mv_ref[s, :, sl].astype(BF16))
    x2 = x1 + _dot(ox_scr[...].astype(BF16), wxo_ref[...])
    x2_ref[...] = x2

    h3 = _rms_rows(x2, gffn_ref[...]).astype(BF16)
    h3_ref[...] = h3
    work = _dot(h3, wr_ref[...]) + br_ref[...]
    lane = lax.broadcasted_iota(jnp.int32, work.shape, 1)
    vals, idxs = [], []
    for _ in range(TOP_K):
        m = jnp.max(work, axis=-1, keepdims=True)
        idx = jnp.min(jnp.where(work == m, lane, LANES), axis=-1, keepdims=True)
        vals.append(m)
        idxs.append(idx)
        work = jnp.where(lane == idx, -jnp.inf, work)
    es = [jnp.exp(v - vals[0]) for v in vals]
    den = sum(es)
    sel = [lane == idx for idx in idxs]
    onehot = functools.reduce(jnp.logical_or, sel).astype(BF16)
    run = run_scr[...]
    rank = _dot(tri_ref[...], onehot) + run
    run_scr[...] = run + jnp.sum(onehot.astype(F32), axis=0, keepdims=True)
    cnt_ref[...] = run_scr[...]
    ti = jnp.zeros(work.shape, jnp.int32)
    tg = jnp.zeros(work.shape, F32)
    for j in range(TOP_K):
        pos = jnp.sum(jnp.where(sel[j], rank, 0.0), axis=-1, keepdims=True).astype(jnp.int32)
        ti = jnp.where(lane == j, idxs[j], ti)
        ti = jnp.where(lane == TOP_K + j, pos, ti)
        tg = jnp.where(lane == j, es[j] / den, tg)
    ti_ref[...] = ti
    tg_ref[...] = tg


def _post(x, og, swa, wout, gx, wxq, gxq, mk, mv, wxo, gffn, wr, br, cnt0, tm, n_seg, rows_per_mem, h3_rows, h3_row0,
          h3_prior):
    n = x.shape[0]
    n_pat = len(swa) // 2 if len(swa) > 1 else 1
    tri = jnp.asarray(np.tril(np.ones((tm, tm), np.float32), -1), BF16)
    row = lambda w: pl.BlockSpec((tm, w), lambda i: (i, 0))
    full = lambda a: pl.BlockSpec(a.shape, lambda i: (0,) * a.ndim)
    mem = pl.BlockSpec((n_seg, MEM_LEN, X_W), lambda i: ((i * tm) // (rows_per_mem * n_seg), 0, 0))
    swa_specs = [pl.BlockSpec((tm * SWA_W // a.shape[1], a.shape[1]), lambda i: (i, 0)) for a in swa]
    scratch = [pltpu.VMEM((tm, X_W), F32), pltpu.VMEM((1, LANES), F32)]
    if n_pat > 1:
        scratch.append(pltpu.VMEM((len(swa), SWA_W // LANES, tm, LANES), F32))
    return pl.pallas_call(
        functools.partial(_post_kernel, n_pat=n_pat, n_seg=n_seg),
        grid=(n // tm,),
        in_specs=[row(D_MODEL), row(GLA_V_W)] + swa_specs
        + [full(wout), full(gx), full(wxq), full(gxq), mem, mem, full(wxo), full(gffn), full(wr), full(br),
           full(tri), full(cnt0)] + [pl.BlockSpec(memory_space=pl.ANY)] * len(h3_prior),
        out_specs=[row(D_MODEL), pl.BlockSpec((tm, D_MODEL), lambda i: (i + h3_row0 // tm, 0)), row(LANES), row(LANES),
                   full(cnt0)],
        out_shape=[jax.ShapeDtypeStruct((n, D_MODEL), F32), jax.ShapeDtypeStruct((h3_rows, D_MODEL), BF16),
                   jax.ShapeDtypeStruct((n, LANES), jnp.int32), jax.ShapeDtypeStruct((n, LANES), F32),
                   jax.ShapeDtypeStruct((1, LANES), F32)],
        scratch_shapes=scratch,
        compiler_params=_cparams("arbitrary"),
        name="post",
        input_output_aliases={14 + len(swa): 1} if h3_prior else {},
    )(x, og, *swa, wout, gx, wxq, gxq, mk, mv, wxo, gffn, wr, br, tri, cnt0, *h3_prior)


def _moe_kernel(te_ref, nu_ref, after_ref, x_ref, wg_ref, bg_ref, wu_ref, bu_ref, wd_ref, bd_ref, *rest):
    y_ref, wg_scr, wu_scr, wd_scr = rest[-4:]
    i = pl.program_id(0)
    live = i < nu_ref[0]

    @pl.when(live & ((i == 0) | (te_ref[i] != te_ref[jnp.maximum(i - 1, 0)])))
    def _():
        wg_scr[...] = wg_ref[...].astype(BF16)
        wu_scr[...] = wu_ref[...].astype(BF16)
        wd_scr[...] = wd_ref[...].astype(BF16)

    @pl.when(live)
    def _():
        x = x_ref[...]
        g = jnp.minimum(_dot(x, wg_scr[...]) + bg_ref[...], SWIGLU_LIMIT)
        u = jnp.clip(_dot(x, wu_scr[...]) + bu_ref[...], -SWIGLU_LIMIT, SWIGLU_LIMIT)
        a = g * _sigmoid(SWIGLU_ALPHA * g) * (u + 1.0)
        y_ref[...] = (_dot(a.astype(BF16), wd_scr[...]) + bd_ref[...]).astype(y_ref.dtype)

    @pl.when(jnp.logical_not(live))
    def _():
        y_ref[...] = jnp.zeros(y_ref.shape, y_ref.dtype)


MOE_CHUNKS = 3


def _moe(tile_expert, n_used, after, h3, src_tok, wg, bg, wu, bu, wd, bd):
    P = src_tok.shape[0]
    tm = MOE_TM
    n_tiles = P // tm
    ys = None
    for c in range(MOE_CHUNKS):
        lo, hi = n_tiles * c // MOE_CHUNKS, n_tiles * (c + 1) // MOE_CHUNKS
        w_spec = pl.BlockSpec((None, D_MODEL, D_MODEL), lambda i, te, *_: (te[i], 0, 0))
        b_spec = pl.BlockSpec((None, 1, D_MODEL), lambda i, te, *_: (te[i], 0, 0))
        prior = [] if ys is None else [ys]
        ys = pl.pallas_call(
            _moe_kernel,
            grid_spec=pltpu.PrefetchScalarGridSpec(
                num_scalar_prefetch=3,
                grid=(hi - lo,),
                in_specs=[pl.BlockSpec((tm, D_MODEL), lambda i, *_: (i, 0)),
                          w_spec, b_spec, w_spec, b_spec, w_spec, b_spec]
                + [pl.BlockSpec(memory_space=pl.ANY)] * len(prior),
                out_specs=pl.BlockSpec((tm, D_MODEL), lambda i, *_, lo=lo: (i + lo, 0)),
                scratch_shapes=[pltpu.VMEM((D_MODEL, D_MODEL), BF16)] * 3,
            ),
            out_shape=jax.ShapeDtypeStruct((P, D_MODEL), BF16),
            input_output_aliases={10: 0} if prior else {},
            compiler_params=_cparams("arbitrary"),
            name="moe",
        )(tile_expert[lo:hi], n_used - lo, after, _sc_take(h3, src_tok[lo * tm:hi * tm]), wg, bg, wu, bu, wd, bd,
          *prior)
    return ys


def _route(top_i, rank, counts, tm):
    n = top_i.shape[0]
    a = n * TOP_K
    n_tiles = a // tm + N_EXPERTS
    tiles_e = (counts + tm - 1) // tm
    tile_end = jnp.cumsum(tiles_e)
    slot0 = (tile_end - tiles_e) * tm
    experts = jnp.arange(N_EXPERTS, dtype=jnp.int32)
    dest = rank + jnp.sum(jnp.where(top_i[..., None] == experts, slot0, 0), axis=-1)
    n_used = tile_end[-1:].astype(jnp.int32)
    tile_expert = jnp.minimum(
        jnp.sum((tile_end[None, :] <= jnp.arange(n_tiles, dtype=jnp.int32)[:, None]).astype(jnp.int32), axis=1),
        N_EXPERTS - 1)
    pad_tok = -1 - (jnp.arange(n_tiles * tm, dtype=jnp.int32) % n)
    src_tok = pad_tok.at[dest.reshape(a)].max(jnp.arange(a, dtype=jnp.int32) // TOP_K, unique_indices=True)
    src_tok = jnp.where(src_tok < 0, -1 - src_tok, src_tok)
    return tile_expert, n_used, src_tok, dest


def _combine_kernel(x_ref, y_ref, g_ref, *rest):
    o_ref = rest[-1]
    acc = x_ref[...]
    g = g_ref[...]
    for j in range(TOP_K):
        acc = acc + g[:, j:j + 1] * y_ref[j].astype(F32)
    o_ref[...] = acc


COMBINE_CHUNKS = 4


def _combine(x2, ysg, tg, row0, tm, prior):
    n = ysg.shape[1]
    off = row0 // tm
    return pl.pallas_call(
        _combine_kernel,
        grid=(n // tm,),
        in_specs=[pl.BlockSpec((tm, D_MODEL), lambda i: (i + off, 0)),
                  pl.BlockSpec((TOP_K, tm, D_MODEL), lambda i: (0, i, 0)),
                  pl.BlockSpec((tm, LANES), lambda i: (i + off, 0))] + [pl.BlockSpec(memory_space=pl.ANY)] * len(prior),
        out_specs=pl.BlockSpec((tm, D_MODEL), lambda i: (i + off, 0)),
        out_shape=jax.ShapeDtypeStruct(x2.shape, F32),
        input_output_aliases={3: 0} if prior else {},
        compiler_params=_cparams("parallel"),
        name="combine",
    )(x2, ysg, tg, *prior)


@compute_on("tpu_sparsecore")
@jax.jit
def _sc_take(x, idx):
    return jnp.take(x, idx, axis=0, mode="clip")


def _mixer_inputs(x2d, w, tm, classes, seq_len=0, win_len=0):
    return _in_proj(x2d, w["gmix"], w["w_in_p"], w["wa2_p"], w["ba"], w["gsq"], w["gsk"], w["bd"], tm, classes,
                    seq_len, win_len)


def kernel(x_prompt, x_sample, mem_prompt, state_gla, cache_swa_k, cache_swa_v, cache_mem_k, cache_mem_v, g_mix, w_in, w_gla_a2, b_gla_a, g_gla_out, g_swa_q, g_swa_k, w_out, g_mem, w_mk, w_mv, g_xk, g_xattn, w_xq, g_xq, w_xo, g_ffn, w_router, b_router, w_gate, b_gate, w_up, b_up, w_down, b_down):
    B, T, D = x_prompt.shape
    Bs, Ts, _ = x_sample.shape
    P = cache_swa_k.shape[2]
    l = 0

    wi = w_in[l]
    segs = np.cumsum((0, GLA_QK_W, GLA_QK_W, GLA_V_W, GLA_V_W, GLA_RANK, SWA_W, SWA_W, SWA_W))
    gq_c, gk_c, gv_c, gr_c, ga_c, sq_c, sk_c, sv_c = [wi[:, segs[j]:segs[j + 1]] for j in range(8)]
    w_in_p = jnp.concatenate(
        [gq_c, gk_c, gv_c, gr_c, sq_c, sk_c, sv_c, ga_c, jnp.zeros((D, D_IN_PAD - _C_GA - GLA_RANK), F32)],
        axis=1).astype(BF16)
    heads = np.arange(SWA_W) // SWA_HD
    w = dict(
        gmix=g_mix[l][None], w_in_p=w_in_p,
        wa2_p=jnp.concatenate([w_gla_a2[l], jnp.zeros((LANES - GLA_RANK, GLA_QK_W), F32)], axis=0).astype(BF16),
        ba=b_gla_a[l][None],
        gsq=jnp.tile(g_swa_q[l], SWA_HEADS)[None], gsk=jnp.tile(g_swa_k[l], SWA_HEADS)[None],
        bd=jnp.asarray(heads[:, None] == heads[None, :], BF16),
    )
    gout = g_gla_out[l][None]
    wout = w_out[l].astype(BF16)
    wxq, wxo = w_xq[l].astype(BF16), w_xo[l].astype(BF16)
    wr = jnp.concatenate([w_router[l], jnp.zeros((D, LANES - N_EXPERTS), F32)], axis=1).astype(BF16)
    br = jnp.concatenate([b_router[l], jnp.full((LANES - N_EXPERTS,), NEG, F32)])[None]
    post_w = (wout, g_xattn[l][None], wxq, g_xq[l][None])
    post_w2 = (wxo, g_ffn[l][None], wr, br)

    w_buf = min(SWA_WINDOW, T)
    gq, gk, gv, gr, la, sq, sk, sv, *cls, kT_p, vT_p = _mixer_inputs(x_prompt.reshape(B * T, D), w, 512, True, T, w_buf)
    r3 = lambda a: a.reshape(B, T, a.shape[-1])
    og_p, st_p = _gla(r3(gq), r3(gk), r3(gv), r3(la), r3(gr),
                      jnp.zeros((B, GLA_QK_W, GLA_DV), F32), gout, 512)
    o_pat, l_pat = [], []
    for _, dil in DILATED_PATTERNS:
        if dil == 1:
            qkv = [r3(sq), r3(sk), r3(sv)]
        else:
            c = CLASS_DILS.index(dil)
            qkv = [cls[j * len(CLASS_DILS) + c].reshape(B, T // dil, dil * SWA_W) for j in range(3)]
        o_c, l_c = _band_attn(*qkv, dil)
        o_pat.append(o_c)
        l_pat.append(l_c)
    mk_p, mv_p = _mem_kv(mem_prompt.reshape(B * MEM_LEN, D), g_mem[l][None], w_mk[l].astype(BF16),
                         w_mv[l].astype(BF16), g_xk[l][None])
    n_p, n_all = B * T, B * T + Bs * Ts
    x2_p, h3, ti_p, tg_p, cnt_p = _post(x_prompt.reshape(B * T, D), og_p.reshape(B * T, GLA_V_W), o_pat + l_pat,
                                        *post_w, mk_p.reshape(B, MEM_LEN, X_W), mv_p.reshape(B, MEM_LEN, X_W),
                                        *post_w2, jnp.zeros((1, LANES), F32), tm=256, n_seg=1, rows_per_mem=T,
                                        h3_rows=n_all, h3_row0=0, h3_prior=[])

    gq, gk, gv, gr, la, sq, sk, sv = _mixer_inputs(x_sample.reshape(Bs * Ts, D), w, Bs * Ts, False)
    pad = lambda a: jnp.pad(a.reshape(Bs, Ts, a.shape[-1]), ((0, 0), (0, GLA_CHUNK - Ts), (0, 0)))
    og_s, st_s = _gla(pad(gq), pad(gk), pad(gv), pad(la), pad(gr),
                      state_gla[l].reshape(Bs, GLA_QK_W, GLA_DV), gout, GLA_CHUNK)
    og_s = og_s[:, :Ts].reshape(Bs * Ts, GLA_V_W)
    rows_minor = lambda a: jnp.transpose(a.reshape(Bs, -1, SWA_HEADS, SWA_HD), (0, 2, 3, 1))
    knT, vnT, kpT, vpT = rows_minor(sk), rows_minor(sv), rows_minor(cache_swa_k[l]), rows_minor(cache_swa_v[l])
    o_swa_s = _samp_attn(sq.reshape(Bs, Ts, SWA_W), knT, vnT, kpT, vpT)
    seqs = 8
    x2_s, h3, ti_s, tg_s, cnt = _post(x_sample.reshape(Bs * Ts, D), og_s, [o_swa_s.reshape(Bs * Ts, SWA_W)],
                                      *post_w, cache_mem_k[l].reshape(Bs, MEM_LEN, X_W),
                                      cache_mem_v[l].reshape(Bs, MEM_LEN, X_W),
                                      *post_w2, cnt_p, tm=seqs * Ts, n_seg=seqs, rows_per_mem=Ts,
                                      h3_rows=n_all, h3_row0=n_p, h3_prior=[h3])

    ti = jnp.concatenate([ti_p[:, :2 * TOP_K], ti_s[:, :2 * TOP_K]], axis=0)
    tile_expert, n_used, src_tok, dest = _route(ti[:, :TOP_K], ti[:, TOP_K:], cnt[0, :N_EXPERTS].astype(jnp.int32),
                                                MOE_TM)
    kb_s, vb_s = _cache_shift(dest[0, :1], knT, vnT, kpT, vpT)
    copied = kb_s[0, 0, 0, :1].astype(jnp.int32)
    kb_s, vb_s = jnp.transpose(kb_s, (0, 3, 1, 2)), jnp.transpose(vb_s, (0, 3, 1, 2))
    ys = _moe(tile_expert, n_used, copied, h3, src_tok, w_gate[l], b_gate[l][:, None, :], w_up[l],
              b_up[l][:, None, :], w_down[l], b_down[l][:, None, :])
    def combined(x2, tg, row0, first, n, tm, prior):
        ysg = _sc_take(ys, dest[first:first + n].T.reshape(-1)).reshape(TOP_K, n, D)
        return [_combine(x2, ysg, tg, row0, tm, prior)]

    y_p, n_c = [], n_p // COMBINE_CHUNKS
    for c in range(COMBINE_CHUNKS):
        y_p = combined(x2_p, tg_p, c * n_c, c * n_c, n_c, 256, y_p)
    y_p, y_s = y_p[0], combined(x2_s, tg_s, 0, n_p, Bs * Ts, Bs * Ts, [])[0]

    kv_p = lambda a: jnp.transpose(a.reshape(B, SWA_HEADS, SWA_HD, w_buf), (0, 3, 1, 2))[None]
    return (y_p.reshape(B, T, D), y_s.reshape(Bs, Ts, D),
            st_p.reshape(1, B, GLA_HEADS, GLA_DK, GLA_DV), kv_p(kT_p), kv_p(vT_p),
            mk_p.reshape(1, B, MEM_LEN, X_HEADS, X_HD), mv_p.reshape(1, B, MEM_LEN, X_HEADS, X_HD),
            st_s.reshape(1, Bs, GLA_HEADS, GLA_DK, GLA_DV),
            kb_s.reshape(1, Bs, P, SWA_HEADS, SWA_HD), vb_s.reshape(1, Bs, P, SWA_HEADS, SWA_HD))
```

```python
import functools

import numpy as np
import jax
import jax.numpy as jnp
from jax import lax
from jax.experimental import pallas as pl
from jax.experimental.compute_on import compute_on
from jax.experimental.pallas import tpu as pltpu

F32 = jnp.float32
BF16 = jnp.bfloat16

D_MODEL = 1024
GLA_HEADS = 4
GLA_DK = 64
GLA_DV = 128
GLA_RANK = 16
GLA_TAU = 16.0
GLA_CHUNK = 64
GLA_SUB = 16
SWA_HEADS = 8
SWA_HD = 64
DILATED_PATTERNS = ((128, 1), (512, 4), (2048, 16))
BAND = 128
SWA_WINDOW = 2048
MEM_LEN = 256
X_HEADS = 4
X_HD = 128
N_EXPERTS = 32
TOP_K = 4
SWIGLU_ALPHA = 1.702
SWIGLU_LIMIT = 7.0
EPS = 1e-6

GLA_QK_W = GLA_HEADS * GLA_DK
GLA_V_W = GLA_HEADS * GLA_DV
SWA_W = SWA_HEADS * SWA_HD
X_W = X_HEADS * X_HD
LANES = 128
NEG = -1e30
VMEM_LIMIT = 56 * 1024 * 1024

_C_GQ, _C_GK, _C_GV, _C_GR, _C_SQ, _C_SK, _C_SV, _C_GA = 0, 256, 512, 1024, 1536, 2048, 2560, 3072
D_IN_PAD = 3200

MOE_TM = 512


def _cparams(*sem):
    return pltpu.CompilerParams(dimension_semantics=sem, vmem_limit_bytes=VMEM_LIMIT)


def _dot(a, b):
    return jnp.dot(a, b, preferred_element_type=F32)


def _dot_nt(a, b):
    return lax.dot_general(a, b, (((1,), (1,)), ((), ())), preferred_element_type=F32)


def _rms_rows(x, g):
    return x * lax.rsqrt(jnp.mean(x * x, axis=-1, keepdims=True) + EPS) * g


def _split_bf16(x, n):
    out = []
    for _ in range(n - 1):
        hi = x.astype(BF16)
        out.append(hi)
        x = x - hi.astype(F32)
    out.append(x.astype(BF16))
    return out


def _group_rms(z, g, ones_bd, group):
    hi, lo = _split_bf16(z * z, 2)
    ss = _dot(hi, ones_bd) + _dot(lo, ones_bd)
    return z * lax.rsqrt(ss * (1.0 / group) + EPS) * g


def _log_sigmoid(x):
    return jnp.minimum(x, 0.0) - jnp.log1p(jnp.exp(-jnp.abs(x)))


def _sigmoid(x):
    return 1.0 / (1.0 + jnp.exp(-x))


CLASS_DILS = tuple(d for _, d in DILATED_PATTERNS if d > 1)


def _in_proj_kernel(x_ref, gmix_ref, w_ref, wa2_ref, ba_ref, gsq_ref, gsk_ref, bd_ref,
                    oq_ref, ok_ref, ov_ref, or_ref, ola_ref, osq_ref, osk_ref, osv_ref, *rest, classes, seq_tiles,
                    win_tiles):
    h = _rms_rows(x_ref[...], gmix_ref[...]).astype(BF16)

    def proj(lo, width):
        return _dot(h, w_ref[:, lo:lo + width])

    oq_ref[...] = proj(_C_GQ, GLA_QK_W) * (GLA_DK ** -0.5)
    ok_ref[...] = proj(_C_GK, GLA_QK_W)
    ov_ref[...] = proj(_C_GV, GLA_V_W)
    or_ref[...] = proj(_C_GR, GLA_V_W)
    osv_ref[...] = proj(_C_SV, SWA_W)
    ga = proj(_C_GA, LANES)
    xa = _dot(ga.astype(BF16), wa2_ref[...]) + ba_ref[...]
    ola_ref[...] = _log_sigmoid(xa) * (1.0 / GLA_TAU)
    bd = bd_ref[...]
    osq_ref[...] = _group_rms(proj(_C_SQ, SWA_W), gsq_ref[...], bd, SWA_HD)
    osk_ref[...] = _group_rms(proj(_C_SK, SWA_W), gsk_ref[...], bd, SWA_HD)

    if classes:
        kT_ref, vT_ref = rest[-3], rest[-2]
        rest = rest[:-3] + rest[-1:]

        @pl.when(pl.program_id(0) % seq_tiles >= seq_tiles - win_tiles)
        def _():
            kT_ref[...] = osk_ref[...].T
            vT_ref[...] = osv_ref[...].T

        cls_refs, col_scr = rest[:-1], rest[-1]
        tm = x_ref.shape[0]
        n_col = SWA_W // LANES
        for j, src in enumerate((osq_ref, osk_ref, osv_ref)):
            for g in range(n_col):
                col_scr[g] = src[:, g * LANES:(g + 1) * LANES]
            for c, d in enumerate(CLASS_DILS):
                dst = cls_refs[j * len(CLASS_DILS) + c]
                for r in range(d):
                    for g in range(n_col):
                        lo = r * SWA_W + g * LANES
                        dst[:, lo:lo + LANES] = col_scr[g, pl.ds(r, tm // d, stride=d), :].astype(BF16)


def _in_proj(x, gmix, w_in_p, wa2_p, ba, gsq, gsk, bd, tm, classes, seq_len=0, win_len=0):
    n = x.shape[0]
    row = lambda w: pl.BlockSpec((tm, w), lambda i: (i, 0))
    full = lambda a: pl.BlockSpec(a.shape, lambda i: (0,) * a.ndim)
    widths = (GLA_QK_W, GLA_QK_W, GLA_V_W, GLA_V_W, GLA_QK_W, SWA_W, SWA_W, SWA_W)
    out_specs = [row(w) for w in widths]
    out_shape = [jax.ShapeDtypeStruct((n, w), F32) for w in widths]
    seq_tiles, win_tiles = seq_len // tm, win_len // tm
    if classes:
        for _ in range(3):
            for d in CLASS_DILS:
                out_specs.append(pl.BlockSpec((tm // d, d * SWA_W), lambda i: (i, 0)))
                out_shape.append(jax.ShapeDtypeStruct((n // d, d * SWA_W), BF16))
        win = pl.BlockSpec((None, SWA_W, tm), lambda i: (i // seq_tiles, 0,
                                                         jnp.maximum(i % seq_tiles - (seq_tiles - win_tiles), 0)))
        out_specs += [win, win]
        out_shape += [jax.ShapeDtypeStruct((n // seq_len, SWA_W, win_len), F32)] * 2
    return pl.pallas_call(
        functools.partial(_in_proj_kernel, classes=classes, seq_tiles=seq_tiles, win_tiles=win_tiles),
        grid=(n // tm,),
        in_specs=[row(D_MODEL), full(gmix), full(w_in_p), full(wa2_p), full(ba), full(gsq), full(gsk), full(bd)],
        out_specs=out_specs,
        out_shape=out_shape,
        scratch_shapes=[pltpu.VMEM((SWA_W // LANES, tm, LANES), F32)] if classes else [],
        compiler_params=_cparams("arbitrary"),
        name="in_proj",
    )(x, gmix, w_in_p, wa2_p, ba, gsq, gsk, bd)


GLA_NB = 2


def _gla_kernel(q_ref, k_ref, v_ref, la_ref, r_ref, s0_ref, gout_ref, lcat_ref,
                o_ref, s_ref, s_scr, knt_scr, rhs_scr, *, n_chunks):
    tb = pl.program_id(1)
    C, S, NS, H, DK, DV = GLA_CHUNK, GLA_SUB, GLA_CHUNK // GLA_SUB, GLA_HEADS, GLA_DK, GLA_DV

    @pl.when(tb == 0)
    def _():
        s_scr[...] = s0_ref[...]
        knt_scr[...] = jnp.zeros(knt_scr.shape, BF16)
        rhs_scr[...] = jnp.zeros(rhs_scr.shape, BF16)

    lcat = lcat_ref[...]
    gout = gout_ref[...]
    row = lax.broadcasted_iota(jnp.int32, (C, H * DK), 0)
    tril = (lax.broadcasted_iota(jnp.int32, (C, H * C), 1) & (C - 1)) <= lax.broadcasted_iota(jnp.int32, (C, H * C), 0)
    ones_k = jnp.ones((3 * C, LANES), BF16)

    def chunk(c, carry):
        r0 = pl.multiple_of(c * C, C)
        rows = pl.ds(r0, C)
        for n in range(GLA_NB):
            g = la_ref[n, rows, :]
            b_r = _dot(lcat, jnp.concatenate(_split_bf16(g, 3), axis=1))
            b_r = b_r[:, :H * DK] + b_r[:, H * DK:2 * H * DK] + b_r[:, 2 * H * DK:]
            b = b_r[:C]
            ref = b_r[C:]
            b_end = b[C - 1:C, :]
            b_col = _dot(jnp.concatenate(_split_bf16(g.T, 3), axis=1), ones_k)
            q = q_ref[n, rows, :]
            k = k_ref[n, rows, :]
            v = v_ref[n, rows, :]
            q_hat = q * jnp.exp(b - ref)
            q_til = q * jnp.exp(b)
            k_dec = k * jnp.exp(b_end - b)
            for i in range(NS):
                e = ref[S * i:S * i + 1, :] - b
                piece = (k * jnp.exp(jnp.where(row < S * (i + 1), e, NEG))).astype(BF16)
                for h in range(H):
                    lo = i * H * DK + h * DK
                    knt_scr[n, h * C:(h + 1) * C, lo:lo + DK] = piece[:, h * DK:(h + 1) * DK]
            q_cat = jnp.concatenate([jnp.where((row >= S * i) & (row < S * (i + 1)), q_hat, 0.0) for i in range(NS)],
                                    axis=1).astype(BF16)
            sc = jnp.where(tril, _dot_nt(q_cat, knt_scr[n]), 0.0)
            st = s_scr[n]
            for h in range(H):
                vl = slice(h * DV, (h + 1) * DV)
                rhs_scr[n, h * C:(h + 1) * C, vl] = v[:, vl].astype(BF16)
                rhs_scr[n, H * C + h * DK:H * C + (h + 1) * DK, vl] = st[h * DK:(h + 1) * DK, :].astype(BF16)
            o = _dot(jnp.concatenate([sc, q_til], axis=1).astype(BF16), rhs_scr[n])
            kv = _dot(k_dec.T.astype(BF16), v.astype(BF16))
            s_scr[n] = st * jnp.exp(b_col) + jnp.concatenate(
                [kv[h * DK:(h + 1) * DK, h * DV:(h + 1) * DV] for h in range(H)], axis=0)
            gate = r_ref[n, rows, :]
            for h in range(H):
                vl = slice(h * DV, (h + 1) * DV)
                o_ref[n, rows, vl] = (_rms_rows(o[:, vl], gout) * (gate[:, vl] * _sigmoid(gate[:, vl]))).astype(o_ref.dtype)
        return carry

    lax.fori_loop(0, n_chunks, chunk, 0)

    @pl.when(tb == pl.num_programs(1) - 1)
    def _():
        s_ref[...] = s_scr[...]


def _gla_consts():
    C, S = GLA_CHUNK, GLA_SUB
    t = np.arange(C)
    incl = (t[None, :] <= t[:, None]).astype(np.float32)
    upto = (t[None, :] < (t[:, None] // S) * S).astype(np.float32)
    return jnp.asarray(np.concatenate([incl, upto], axis=0), BF16)


def _gla(q, k, v, la, r, s0, gout, tb):
    B, T, _ = q.shape
    H, C, DK, DV, NS = GLA_HEADS, GLA_CHUNK, GLA_DK, GLA_DV, GLA_CHUNK // GLA_SUB
    lcat = _gla_consts()
    blk = lambda w: pl.BlockSpec((GLA_NB, tb, w), lambda b, t: (b, t, 0))
    full = lambda a: pl.BlockSpec(a.shape, lambda b, t: (0,) * a.ndim)
    st_spec = pl.BlockSpec((GLA_NB, H * DK, DV), lambda b, t: (b, 0, 0))
    return pl.pallas_call(
        functools.partial(_gla_kernel, n_chunks=tb // C),
        grid=(B // GLA_NB, T // tb),
        in_specs=[blk(GLA_QK_W), blk(GLA_QK_W), blk(GLA_V_W), blk(GLA_QK_W), blk(GLA_V_W), st_spec,
                  full(gout), full(lcat)],
        out_specs=[blk(GLA_V_W), st_spec],
        out_shape=[jax.ShapeDtypeStruct((B, T, GLA_V_W), BF16),
                   jax.ShapeDtypeStruct((B, H * DK, DV), F32)],
        scratch_shapes=[pltpu.VMEM((GLA_NB, H * DK, DV), F32),
                        pltpu.VMEM((GLA_NB, H * C, NS * H * DK), BF16),
                        pltpu.VMEM((GLA_NB, H * C + H * DK, H * DV), BF16)],
        compiler_params=_cparams("parallel", "arbitrary"),
        name="gla",
    )(q, k, v, la, r, s0, gout, lcat)


def _band_attn_kernel(q_ref, kp_ref, kc_ref, vp_ref, vc_ref, o_ref, lse_ref, k_scr, v_scr, *, dil, n_sub, qs):
    i = pl.program_id(2)
    k_scr[0:BAND, :] = kp_ref[...].astype(BF16)
    k_scr[BAND:, :] = kc_ref[...].astype(BF16)
    v_scr[0:BAND, :] = vp_ref[...].astype(BF16)
    v_scr[BAND:, :] = vc_ref[...].astype(BF16)
    nk = qs + BAND
    t = lax.broadcasted_iota(jnp.int32, (qs, nk), 0)
    c = lax.broadcasted_iota(jnp.int32, (qs, nk), 1)
    dist = BAND + t - c
    in_band = (dist >= 0) & (dist <= BAND)
    distf = (dist * dil).astype(F32)
    first = lax.broadcasted_iota(jnp.int32, (nk, LANES), 1) < SWA_HD
    first_q = lax.broadcasted_iota(jnp.int32, (qs, LANES), 1) < SWA_HD
    zero = jnp.zeros((nk, LANES), BF16)
    ones_bd = jnp.concatenate([first, jnp.logical_not(first)], axis=0).astype(BF16)

    def sub_block(j, carry):
        r0 = pl.multiple_of(j * qs, qs)
        rows = pl.ds(r0, qs)
        valid = in_band & ((c >= BAND) | (i > 0) | (j > 0))
        for pr in range(SWA_HEADS // 2):
            cols = slice(pr * LANES, (pr + 1) * LANES)
            q2 = (q_ref[rows, cols].astype(F32) * (SWA_HD ** -0.5)).astype(BF16)
            k2 = k_scr[pl.ds(r0, nk), cols]
            v2 = v_scr[pl.ds(r0, nk), cols]
            k_bd = jnp.concatenate([jnp.where(first, k2, zero), jnp.where(first, zero, k2)], axis=0)
            v_bd = jnp.concatenate([jnp.where(first, v2, zero), jnp.where(first, zero, v2)], axis=0)
            s2 = _dot_nt(q2, k_bd)
            ps, ms = [], []
            for u in range(2):
                s = s2[:, u * nk:(u + 1) * nk] - (2.0 ** -(2 * pr + u + 1)) * distf
                s = jnp.where(valid, s, NEG)
                m = jnp.max(s, axis=-1, keepdims=True)
                ps.append(jnp.exp(s - m).astype(BF16))
                ms.append(m)
            od = _dot(jnp.concatenate(ps, axis=1), jnp.concatenate([v_bd, ones_bd], axis=1))
            den = od[:, LANES:]
            o_ref[rows, cols] = od[:, :LANES] / den
            lse_ref[rows, cols] = jnp.where(first_q, ms[0], ms[1]) + jnp.log(den)
        return carry

    lax.fori_loop(0, n_sub, sub_block, 0)


def _band_attn(q, k, v, dil):
    B, Tc, dw = q.shape
    W = dw // dil
    qb = min(4 * BAND, Tc)
    qs = min(2 * BAND, Tc)
    n_sub = qb // qs
    cur = pl.BlockSpec((None, qb, W), lambda b, r, i: (b, i, r))
    prev = pl.BlockSpec((None, BAND, W), lambda b, r, i: (b, jnp.maximum(i * (qb // BAND) - 1, 0), r))
    o, lse = pl.pallas_call(
        functools.partial(_band_attn_kernel, dil=dil, n_sub=n_sub, qs=qs),
        grid=(B, dil, Tc // qb),
        in_specs=[cur, prev, cur, prev, cur],
        out_specs=[cur, cur],
        out_shape=[jax.ShapeDtypeStruct((B, Tc, dw), F32)] * 2,
        scratch_shapes=[pltpu.VMEM((BAND + qb, W), BF16)] * 2,
        compiler_params=_cparams("parallel", "parallel", "parallel"),
        name="band_attn_d%d" % dil,
    )(q, k, k, v, v)
    return o.reshape(B * Tc, dw), lse.reshape(B * Tc, dw)


def _pattern_count(dist):
    cnt = jnp.zeros(dist.shape, F32)
    for window, dil in DILATED_PATTERNS:
        hit = (dist >= 0) & (dist <= window)
        if dil > 1:
            hit = hit & ((dist & (dil - 1)) == 0)
        cnt = cnt + hit.astype(F32)
    return cnt


def _samp_attn_kernel(q_ref, kn_ref, vn_ref, kp_ref, vp_ref, o_ref, ko_ref, vo_ref):
    P, Tn = kp_ref.shape[2], kn_ref.shape[2]
    qi = lax.broadcasted_iota(jnp.int32, (Tn, P), 0)
    dist_p = P + qi - lax.broadcasted_iota(jnp.int32, (Tn, P), 1)
    dist_n = lax.broadcasted_iota(jnp.int32, (Tn, Tn), 0) - lax.broadcasted_iota(jnp.int32, (Tn, Tn), 1)
    cnt_p, cnt_n = _pattern_count(dist_p), _pattern_count(dist_n)
    dpf, dnf = dist_p.astype(F32), dist_n.astype(F32)
    q = (q_ref[...] * (SWA_HD ** -0.5)).astype(BF16)
    outs = []
    for h in range(SWA_HEADS):
        kp, vp, kn, vn = kp_ref[h], vp_ref[h], kn_ref[h], vn_ref[h]
        ko_ref[h] = pltpu.roll(kp, P - Tn, axis=1)
        ko_ref[h, :, P - Tn:] = kn
        vo_ref[h] = pltpu.roll(vp, P - Tn, axis=1)
        vo_ref[h, :, P - Tn:] = vn
        qh = q[:, h * SWA_HD:(h + 1) * SWA_HD]
        slope = 2.0 ** -(h + 1)
        lp = jnp.where(cnt_p > 0, _dot(qh, kp.astype(BF16)) - slope * dpf, NEG)
        ln = jnp.where(cnt_n > 0, _dot(qh, kn.astype(BF16)) - slope * dnf, NEG)
        m = jnp.maximum(jnp.max(lp, axis=-1, keepdims=True), jnp.max(ln, axis=-1, keepdims=True))
        pp = cnt_p * jnp.exp(lp - m)
        pn = cnt_n * jnp.exp(ln - m)
        den = jnp.sum(pp, axis=-1, keepdims=True) + jnp.sum(pn, axis=-1, keepdims=True)
        outs.append((_dot_nt(pp.astype(BF16), vp.astype(BF16)) + _dot_nt(pn.astype(BF16), vn.astype(BF16))) / den)
    o_ref[...] = jnp.concatenate(outs, axis=1)


def _samp_attn(q, knT, vnT, kpT, vpT):
    B, Tn, W = q.shape
    P = kpT.shape[-1]
    qs = pl.BlockSpec((None, Tn, W), lambda b: (b, 0, 0))
    new = pl.BlockSpec((None, SWA_HEADS, SWA_HD, Tn), lambda b: (b, 0, 0, 0))
    past = pl.BlockSpec((None, SWA_HEADS, SWA_HD, P), lambda b: (b, 0, 0, 0))
    return pl.pallas_call(
        _samp_attn_kernel,
        grid=(B,),
        in_specs=[qs, new, new, past, past],
        out_specs=[qs, past, past],
        out_shape=[jax.ShapeDtypeStruct((B, Tn, W), F32), jax.ShapeDtypeStruct(kpT.shape, F32),
                   jax.ShapeDtypeStruct(kpT.shape, F32)],
        compiler_params=_cparams("parallel"),
        name="samp_attn",
    )(q, knT, vnT, kpT, vpT)


def _mem_kv_kernel(mem_ref, gmem_ref, wk_ref, wv_ref, gxk_ref, mk_ref, mv_ref):
    hm = _rms_rows(mem_ref[...], gmem_ref[...]).astype(BF16)
    mk = _dot(hm, wk_ref[...])
    gxk = gxk_ref[...]
    for h in range(X_HEADS):
        sl = slice(h * X_HD, (h + 1) * X_HD)
        mk_ref[:, sl] = _rms_rows(mk[:, sl], gxk)
    mv_ref[...] = _dot(hm, wv_ref[...])


def _mem_kv(mem, gmem, wk, wv, gxk):
    n = mem.shape[0]
    return pl.pallas_call(
        _mem_kv_kernel,
        out_shape=[jax.ShapeDtypeStruct((n, X_W), F32)] * 2,
        compiler_params=pltpu.CompilerParams(vmem_limit_bytes=VMEM_LIMIT),
        name="mem_kv",
    )(mem, gmem, wk, wv, gxk)


def _post_kernel(*refs, n_pat, n_seg):
    n_swa = 2 * n_pat if n_pat > 1 else 1
    x_ref, og_ref = refs[0], refs[1]
    swa_refs = refs[2:2 + n_swa]
    (wout_ref, gx_ref, wxq_ref, gxq_ref, mk_ref, mv_ref, wxo_ref, gffn_ref, wr_ref, br_ref, tri_ref,
     cnt0_ref) = refs[2 + n_swa:14 + n_swa]
    n_scr = 3 if n_pat > 1 else 2
    x2_ref, h3_ref, ti_ref, tg_ref, cnt_ref = refs[len(refs) - n_scr - 5:len(refs) - n_scr]
    ox_scr, run_scr = refs[len(refs) - n_scr:len(refs) - n_scr + 2]

    @pl.when(pl.program_id(0) == 0)
    def _():
        run_scr[...] = cnt0_ref[...]

    if n_pat > 1:
        dei_scr = refs[-1]
        tm = x_ref.shape[0]
        vals = []
        for a, r in enumerate(swa_refs):
            d = DILATED_PATTERNS[a % n_pat][1]
            if d == 1:
                vals.append(r[...])
            else:
                slot, n_col = len(vals), SWA_W // LANES
                for c in range(d):
                    for g in range(n_col):
                        lo = c * SWA_W + g * LANES
                        dei_scr[slot, g, pl.ds(c, tm // d, stride=d), :] = r[:, lo:lo + LANES]
                vals.append(jnp.concatenate([dei_scr[slot, g] for g in range(n_col)], axis=1))
        o_p, l_p = vals[:n_pat], vals[n_pat:]
        lmax = functools.reduce(jnp.maximum, l_p)
        w_p = [jnp.exp(l - lmax) for l in l_p]
        o_swa = sum(w * o for w, o in zip(w_p, o_p)) / sum(w_p)
    else:
        o_swa = swa_refs[0][...]

    x1 = x_ref[...] + _dot(og_ref[...], wout_ref[0:GLA_V_W, :]) + _dot(o_swa.astype(BF16), wout_ref[GLA_V_W:, :])

    q = _dot(_rms_rows(x1, gx_ref[...]).astype(BF16), wxq_ref[...])
    gxq = gxq_ref[...]
    seg = x1.shape[0] // n_seg
    for h in range(X_HEADS):
        sl = slice(h * X_HD, (h + 1) * X_HD)
        qn = _rms_rows(q[:, sl], gxq).astype(BF16)
        for s in range(n_seg):
            rows = slice(s * seg, (s + 1) * seg)
            sc = _dot_nt(qn[rows], mk_ref[s, :, sl].astype(BF16)) * (X_HD ** -0.5)
            p = jnp.exp(sc - jnp.max(sc, axis=-1, keepdims=True))
            p = p / jnp.sum(p, axis=-1, keepdims=True)
            ox_scr[rows, sl] = _dot(p.astype(BF16), mv_ref[s, :, sl].astype(BF16))
    x2 = x1 + _dot(ox_scr[...].astype(BF16), wxo_ref[...])
    x2_ref[...] = x2

    h3 = _rms_rows(x2, gffn_ref[...]).astype(BF16)
    h3_ref[...] = h3
    work = _dot(h3, wr_ref[...]) + br_ref[...]
    lane = lax.broadcasted_iota(jnp.int32, work.shape, 1)
    vals, idxs = [], []
    for _ in range(TOP_K):
        m = jnp.max(work, axis=-1, keepdims=True)
        idx = jnp.min(jnp.where(work == m, lane, LANES), axis=-1, keepdims=True)
        vals.append(m)
        idxs.append(idx)
        work = jnp.where(lane == idx, -jnp.inf, work)
    es = [jnp.exp(v - vals[0]) for v in vals]
    den = sum(es)
    sel = [lane == idx for idx in idxs]
    onehot = functools.reduce(jnp.logical_or, sel).astype(BF16)
    run = run_scr[...]
    rank = _dot(tri_ref[...], onehot) + run
    run_scr[...] = run + jnp.sum(onehot.astype(F32), axis=0, keepdims=True)
    cnt_ref[...] = run_scr[...]
    ti = jnp.zeros(work.shape, jnp.int32)
    tg = jnp.zeros(work.shape, F32)
    for j in range(TOP_K):
        pos = jnp.sum(jnp.where(sel[j], rank, 0.0), axis=-1, keepdims=True).astype(jnp.int32)
        ti = jnp.where(lane == j, idxs[j], ti)
        ti = jnp.where(lane == TOP_K + j, pos, ti)
        tg = jnp.where(lane == j, es[j] / den, tg)
    ti_ref[...] = ti
    tg_ref[...] = tg


def _post(x, og, swa, wout, gx, wxq, gxq, mk, mv, wxo, gffn, wr, br, cnt0, tm, n_seg, rows_per_mem, h3_rows, h3_row0,
          h3_prior):
    n = x.shape[0]
    n_pat = len(swa) // 2 if len(swa) > 1 else 1
    tri = jnp.asarray(np.tril(np.ones((tm, tm), np.float32), -1), BF16)
    row = lambda w: pl.BlockSpec((tm, w), lambda i: (i, 0))
    full = lambda a: pl.BlockSpec(a.shape, lambda i: (0,) * a.ndim)
    mem = pl.BlockSpec((n_seg, MEM_LEN, X_W), lambda i: ((i * tm) // (rows_per_mem * n_seg), 0, 0))
    swa_specs = [pl.BlockSpec((tm * SWA_W // a.shape[1], a.shape[1]), lambda i: (i, 0)) for a in swa]
    scratch = [pltpu.VMEM((tm, X_W), F32), pltpu.VMEM((1, LANES), F32)]
    if n_pat > 1:
        scratch.append(pltpu.VMEM((len(swa), SWA_W // LANES, tm, LANES), F32))
    return pl.pallas_call(
        functools.partial(_post_kernel, n_pat=n_pat, n_seg=n_seg),
        grid=(n // tm,),
        in_specs=[row(D_MODEL), row(GLA_V_W)] + swa_specs
        + [full(wout), full(gx), full(wxq), full(gxq), mem, mem, full(wxo), full(gffn), full(wr), full(br),
           full(tri), full(cnt0)] + [pl.BlockSpec(memory_space=pl.ANY)] * len(h3_prior),
        out_specs=[row(D_MODEL), pl.BlockSpec((tm, D_MODEL), lambda i: (i + h3_row0 // tm, 0)), row(LANES), row(LANES),
                   full(cnt0)],
        out_shape=[jax.ShapeDtypeStruct((n, D_MODEL), F32), jax.ShapeDtypeStruct((h3_rows, D_MODEL), BF16),
                   jax.ShapeDtypeStruct((n, LANES), jnp.int32), jax.ShapeDtypeStruct((n, LANES), F32),
                   jax.ShapeDtypeStruct((1, LANES), F32)],
        scratch_shapes=scratch,
        compiler_params=_cparams("arbitrary"),
        name="post",
        input_output_aliases={14 + len(swa): 1} if h3_prior else {},
    )(x, og, *swa, wout, gx, wxq, gxq, mk, mv, wxo, gffn, wr, br, tri, cnt0, *h3_prior)


def _moe_kernel(te_ref, nu_ref, x_ref, wg_ref, bg_ref, wu_ref, bu_ref, wd_ref, bd_ref, *rest):
    y_ref, wg_scr, wu_scr, wd_scr = rest[-4:]
    i = pl.program_id(0)
    live = i < nu_ref[0]

    @pl.when(live & ((i == 0) | (te_ref[i] != te_ref[jnp.maximum(i - 1, 0)])))
    def _():
        wg_scr[...] = wg_ref[...].astype(BF16)
        wu_scr[...] = wu_ref[...].astype(BF16)
        wd_scr[...] = wd_ref[...].astype(BF16)

    @pl.when(live)
    def _():
        x = x_ref[...]
        g = jnp.minimum(_dot(x, wg_scr[...]) + bg_ref[...], SWIGLU_LIMIT)
        u = jnp.clip(_dot(x, wu_scr[...]) + bu_ref[...], -SWIGLU_LIMIT, SWIGLU_LIMIT)
        a = g * _sigmoid(SWIGLU_ALPHA * g) * (u + 1.0)
        y_ref[...] = (_dot(a.astype(BF16), wd_scr[...]) + bd_ref[...]).astype(y_ref.dtype)

    @pl.when(jnp.logical_not(live))
    def _():
        y_ref[...] = jnp.zeros(y_ref.shape, y_ref.dtype)


MOE_CHUNK_ENDS = (0.05, 0.25, 1.0)


def _moe(tile_expert, n_used, h3, src_tok, wg, bg, wu, bu, wd, bd):
    P = src_tok.shape[0]
    tm = MOE_TM
    n_tiles = P // tm
    ys = None
    ends = [max(1, round(n_tiles * f)) for f in MOE_CHUNK_ENDS]
    for lo, hi in zip([0] + ends[:-1], ends):
        w_spec = pl.BlockSpec((None, D_MODEL, D_MODEL), lambda i, te, *_: (te[i], 0, 0))
        b_spec = pl.BlockSpec((None, 1, D_MODEL), lambda i, te, *_: (te[i], 0, 0))
        prior = [] if ys is None else [ys]
        ys = pl.pallas_call(
            _moe_kernel,
            grid_spec=pltpu.PrefetchScalarGridSpec(
                num_scalar_prefetch=2,
                grid=(hi - lo,),
                in_specs=[pl.BlockSpec((tm, D_MODEL), lambda i, *_: (i, 0)),
                          w_spec, b_spec, w_spec, b_spec, w_spec, b_spec]
                + [pl.BlockSpec(memory_space=pl.ANY)] * len(prior),
                out_specs=pl.BlockSpec((tm, D_MODEL), lambda i, *_, lo=lo: (i + lo, 0)),
                scratch_shapes=[pltpu.VMEM((D_MODEL, D_MODEL), BF16)] * 3,
            ),
            out_shape=jax.ShapeDtypeStruct((P, D_MODEL), BF16),
            input_output_aliases={9: 0} if prior else {},
            compiler_params=_cparams("arbitrary"),
            name="moe",
        )(tile_expert[lo:hi], n_used - lo, _sc_take(h3, src_tok[lo * tm:hi * tm]), wg, bg, wu, bu, wd, bd, *prior)
    return ys


def _route(top_i, rank, counts, tm):
    n = top_i.shape[0]
    a = n * TOP_K
    n_tiles = a // tm + N_EXPERTS
    tiles_e = (counts + tm - 1) // tm
    tile_end = jnp.cumsum(tiles_e)
    slot0 = (tile_end - tiles_e) * tm
    experts = jnp.arange(N_EXPERTS, dtype=jnp.int32)
    dest = rank + jnp.sum(jnp.where(top_i[..., None] == experts, slot0, 0), axis=-1)
    n_used = tile_end[-1:].astype(jnp.int32)
    tile_expert = jnp.minimum(
        jnp.sum((tile_end[None, :] <= jnp.arange(n_tiles, dtype=jnp.int32)[:, None]).astype(jnp.int32), axis=1),
        N_EXPERTS - 1)
    pad_tok = -1 - (jnp.arange(n_tiles * tm, dtype=jnp.int32) % n)
    src_tok = pad_tok.at[dest.reshape(a)].max(jnp.arange(a, dtype=jnp.int32) // TOP_K, unique_indices=True)
    src_tok = jnp.where(src_tok < 0, -1 - src_tok, src_tok)
    return tile_expert, n_used, src_tok, dest


def _combine_kernel(x_ref, y_ref, g_ref, *rest):
    o_ref = rest[-1]
    acc = x_ref[...]
    g = g_ref[...]
    for j in range(TOP_K):
        acc = acc + g[:, j:j + 1] * y_ref[j].astype(F32)
    o_ref[...] = acc


COMBINE_CHUNKS = 8


def _combine(x2, ysg, tg, row0, tm, prior):
    n = ysg.shape[1]
    off = row0 // tm
    return pl.pallas_call(
        _combine_kernel,
        grid=(n // tm,),
        in_specs=[pl.BlockSpec((tm, D_MODEL), lambda i: (i + off, 0)),
                  pl.BlockSpec((TOP_K, tm, D_MODEL), lambda i: (0, i, 0)),
                  pl.BlockSpec((tm, LANES), lambda i: (i + off, 0))] + [pl.BlockSpec(memory_space=pl.ANY)] * len(prior),
        out_specs=pl.BlockSpec((tm, D_MODEL), lambda i: (i + off, 0)),
        out_shape=jax.ShapeDtypeStruct(x2.shape, F32),
        input_output_aliases={3: 0} if prior else {},
        compiler_params=_cparams("parallel"),
        name="combine",
    )(x2, ysg, tg, *prior)


@compute_on("tpu_sparsecore")
@jax.jit
def _sc_take(x, idx):
    return jnp.take(x, idx, axis=0, mode="clip")


def _mixer_inputs(x2d, w, tm, classes, seq_len=0, win_len=0):
    return _in_proj(x2d, w["gmix"], w["w_in_p"], w["wa2_p"], w["ba"], w["gsq"], w["gsk"], w["bd"], tm, classes,
                    seq_len, win_len)


def kernel(x_prompt, x_sample, mem_prompt, state_gla, cache_swa_k, cache_swa_v, cache_mem_k, cache_mem_v, g_mix, w_in, w_gla_a2, b_gla_a, g_gla_out, g_swa_q, g_swa_k, w_out, g_mem, w_mk, w_mv, g_xk, g_xattn, w_xq, g_xq, w_xo, g_ffn, w_router, b_router, w_gate, b_gate, w_up, b_up, w_down, b_down):
    B, T, D = x_prompt.shape
    Bs, Ts, _ = x_sample.shape
    P = cache_swa_k.shape[2]
    l = 0

    wi = w_in[l]
    segs = np.cumsum((0, GLA_QK_W, GLA_QK_W, GLA_V_W, GLA_V_W, GLA_RANK, SWA_W, SWA_W, SWA_W))
    gq_c, gk_c, gv_c, gr_c, ga_c, sq_c, sk_c, sv_c = [wi[:, segs[j]:segs[j + 1]] for j in range(8)]
    w_in_p = jnp.concatenate(
        [gq_c, gk_c, gv_c, gr_c, sq_c, sk_c, sv_c, ga_c, jnp.zeros((D, D_IN_PAD - _C_GA - GLA_RANK), F32)],
        axis=1).astype(BF16)
    heads = np.arange(SWA_W) // SWA_HD
    w = dict(
        gmix=g_mix[l][None], w_in_p=w_in_p,
        wa2_p=jnp.concatenate([w_gla_a2[l], jnp.zeros((LANES - GLA_RANK, GLA_QK_W), F32)], axis=0).astype(BF16),
        ba=b_gla_a[l][None],
        gsq=jnp.tile(g_swa_q[l], SWA_HEADS)[None], gsk=jnp.tile(g_swa_k[l], SWA_HEADS)[None],
        bd=jnp.asarray(heads[:, None] == heads[None, :], BF16),
    )
    gout = g_gla_out[l][None]
    wout = w_out[l].astype(BF16)
    wxq, wxo = w_xq[l].astype(BF16), w_xo[l].astype(BF16)
    wr = jnp.concatenate([w_router[l], jnp.zeros((D, LANES - N_EXPERTS), F32)], axis=1).astype(BF16)
    br = jnp.concatenate([b_router[l], jnp.full((LANES - N_EXPERTS,), NEG, F32)])[None]
    post_w = (wout, g_xattn[l][None], wxq, g_xq[l][None])
    post_w2 = (wxo, g_ffn[l][None], wr, br)

    w_buf = min(SWA_WINDOW, T)
    gq, gk, gv, gr, la, sq, sk, sv, *cls, kT_p, vT_p = _mixer_inputs(x_prompt.reshape(B * T, D), w, 512, True, T, w_buf)
    r3 = lambda a: a.reshape(B, T, a.shape[-1])
    og_p, st_p = _gla(r3(gq), r3(gk), r3(gv), r3(la), r3(gr),
                      jnp.zeros((B, GLA_QK_W, GLA_DV), F32), gout, 512)
    o_pat, l_pat = [], []
    for _, dil in DILATED_PATTERNS:
        if dil == 1:
            qkv = [r3(sq), r3(sk), r3(sv)]
        else:
            c = CLASS_DILS.index(dil)
            qkv = [cls[j * len(CLASS_DILS) + c].reshape(B, T // dil, dil * SWA_W) for j in range(3)]
        o_c, l_c = _band_attn(*qkv, dil)
        o_pat.append(o_c)
        l_pat.append(l_c)
    mk_p, mv_p = _mem_kv(mem_prompt.reshape(B * MEM_LEN, D), g_mem[l][None], w_mk[l].astype(BF16),
                         w_mv[l].astype(BF16), g_xk[l][None])
    n_p, n_all = B * T, B * T + Bs * Ts
    x2_p, h3, ti_p, tg_p, cnt_p = _post(x_prompt.reshape(B * T, D), og_p.reshape(B * T, GLA_V_W), o_pat + l_pat,
                                        *post_w, mk_p.reshape(B, MEM_LEN, X_W), mv_p.reshape(B, MEM_LEN, X_W),
                                        *post_w2, jnp.zeros((1, LANES), F32), tm=256, n_seg=1, rows_per_mem=T,
                                        h3_rows=n_all, h3_row0=0, h3_prior=[])

    gq, gk, gv, gr, la, sq, sk, sv = _mixer_inputs(x_sample.reshape(Bs * Ts, D), w, Bs * Ts, False)
    pad = lambda a: jnp.pad(a.reshape(Bs, Ts, a.shape[-1]), ((0, 0), (0, GLA_CHUNK - Ts), (0, 0)))
    og_s, st_s = _gla(pad(gq), pad(gk), pad(gv), pad(la), pad(gr),
                      state_gla[l].reshape(Bs, GLA_QK_W, GLA_DV), gout, GLA_CHUNK)
    og_s = og_s[:, :Ts].reshape(Bs * Ts, GLA_V_W)
    rows_minor = lambda a: jnp.transpose(a.reshape(Bs, -1, SWA_HEADS, SWA_HD), (0, 2, 3, 1))
    o_swa_s, kb_s, vb_s = _samp_attn(sq.reshape(Bs, Ts, SWA_W), rows_minor(sk), rows_minor(sv),
                                     rows_minor(cache_swa_k[l]), rows_minor(cache_swa_v[l]))
    kb_s, vb_s = jnp.transpose(kb_s, (0, 3, 1, 2)), jnp.transpose(vb_s, (0, 3, 1, 2))
    seqs = 8
    x2_s, h3, ti_s, tg_s, cnt = _post(x_sample.reshape(Bs * Ts, D), og_s, [o_swa_s.reshape(Bs * Ts, SWA_W)],
                                      *post_w, cache_mem_k[l].reshape(Bs, MEM_LEN, X_W),
                                      cache_mem_v[l].reshape(Bs, MEM_LEN, X_W),
                                      *post_w2, cnt_p, tm=seqs * Ts, n_seg=seqs, rows_per_mem=Ts,
                                      h3_rows=n_all, h3_row0=n_p, h3_prior=[h3])

    ti = jnp.concatenate([ti_p[:, :2 * TOP_K], ti_s[:, :2 * TOP_K]], axis=0)
    tile_expert, n_used, src_tok, dest = _route(ti[:, :TOP_K], ti[:, TOP_K:], cnt[0, :N_EXPERTS].astype(jnp.int32),
                                                MOE_TM)
    ys = _moe(tile_expert, n_used, h3, src_tok, w_gate[l], b_gate[l][:, None, :], w_up[l],
              b_up[l][:, None, :], w_down[l], b_down[l][:, None, :])
    def combined(x2, tg, row0, first, n, tm, prior):
        ysg = _sc_take(ys, dest[first:first + n].T.reshape(-1)).reshape(TOP_K, n, D)
        return [_combine(x2, ysg, tg, row0, tm, prior)]

    y_p, n_c = [], n_p // COMBINE_CHUNKS
    for c in range(COMBINE_CHUNKS):
        y_p = combined(x2_p, tg_p, c * n_c, c * n_c, n_c, 256, y_p)
    y_p, y_s = y_p[0], combined(x2_s, tg_s, 0, n_p, Bs * Ts, Bs * Ts, [])[0]

    kv_p = lambda a: jnp.transpose(a.reshape(B, SWA_HEADS, SWA_HD, w_buf), (0, 3, 1, 2))[None]
    return (y_p.reshape(B, T, D), y_s.reshape(Bs, Ts, D),
            st_p.reshape(1, B, GLA_HEADS, GLA_DK, GLA_DV), kv_p(kT_p), kv_p(vT_p),
            mk_p.reshape(1, B, MEM_LEN, X_HEADS, X_HD), mv_p.reshape(1, B, MEM_LEN, X_HEADS, X_HD),
            st_s.reshape(1, Bs, GLA_HEADS, GLA_DK, GLA_DV),
            kb_s.reshape(1, Bs, P, SWA_HEADS, SWA_HD), vb_s.reshape(1, Bs, P, SWA_HEADS, SWA_HD))
```

```python
import functools

import numpy as np
import jax
import jax.numpy as jnp
from jax import lax
from jax.experimental import pallas as pl
from jax.experimental.compute_on import compute_on
from jax.experimental.pallas import tpu as pltpu

F32 = jnp.float32
BF16 = jnp.bfloat16

D_MODEL = 1024
GLA_HEADS = 4
GLA_DK = 64
GLA_DV = 128
GLA_RANK = 16
GLA_TAU = 16.0
GLA_CHUNK = 64
GLA_SUB = 16
SWA_HEADS = 8
SWA_HD = 64
DILATED_PATTERNS = ((128, 1), (512, 4), (2048, 16))
BAND = 128
SWA_WINDOW = 2048
MEM_LEN = 256
X_HEADS = 4
X_HD = 128
N_EXPERTS = 32
TOP_K = 4
SWIGLU_ALPHA = 1.702
SWIGLU_LIMIT = 7.0
EPS = 1e-6

GLA_QK_W = GLA_HEADS * GLA_DK
GLA_V_W = GLA_HEADS * GLA_DV
SWA_W = SWA_HEADS * SWA_HD
X_W = X_HEADS * X_HD
LANES = 128
NEG = -1e30
VMEM_LIMIT = 56 * 1024 * 1024

_C_GQ, _C_GK, _C_GV, _C_GR, _C_SQ, _C_SK, _C_SV, _C_GA = 0, 256, 512, 1024, 1536, 2048, 2560, 3072
D_IN_PAD = 3200

MOE_TM = 512


def _cparams(*sem):
    return pltpu.CompilerParams(dimension_semantics=sem, vmem_limit_bytes=VMEM_LIMIT)


def _dot(a, b):
    return jnp.dot(a, b, preferred_element_type=F32)


def _dot_nt(a, b):
    return lax.dot_general(a, b, (((1,), (1,)), ((), ())), preferred_element_type=F32)


def _rms_rows(x, g):
    return x * lax.rsqrt(jnp.mean(x * x, axis=-1, keepdims=True) + EPS) * g


def _split_bf16(x, n):
    out = []
    for _ in range(n - 1):
        hi = x.astype(BF16)
        out.append(hi)
        x = x - hi.astype(F32)
    out.append(x.astype(BF16))
    return out


def _group_rms(z, g, ones_bd, group):
    hi, lo = _split_bf16(z * z, 2)
    ss = _dot(hi, ones_bd) + _dot(lo, ones_bd)
    return z * lax.rsqrt(ss * (1.0 / group) + EPS) * g


def _log_sigmoid(x):
    return jnp.minimum(x, 0.0) - jnp.log1p(jnp.exp(-jnp.abs(x)))


def _sigmoid(x):
    return 1.0 / (1.0 + jnp.exp(-x))


CLASS_DILS = tuple(d for _, d in DILATED_PATTERNS if d > 1)


def _in_proj_kernel(x_ref, gmix_ref, w_ref, wa2_ref, ba_ref, gsq_ref, gsk_ref, bd_ref,
                    oq_ref, ok_ref, ov_ref, or_ref, ola_ref, osq_ref, osk_ref, osv_ref, *rest, classes, seq_tiles,
                    win_tiles):
    h = _rms_rows(x_ref[...], gmix_ref[...]).astype(BF16)

    def proj(lo, width):
        return _dot(h, w_ref[:, lo:lo + width])

    oq_ref[...] = proj(_C_GQ, GLA_QK_W) * (GLA_DK ** -0.5)
    ok_ref[...] = proj(_C_GK, GLA_QK_W)
    ov_ref[...] = proj(_C_GV, GLA_V_W)
    or_ref[...] = proj(_C_GR, GLA_V_W)
    osv_ref[...] = proj(_C_SV, SWA_W)
    ga = proj(_C_GA, LANES)
    xa = _dot(ga.astype(BF16), wa2_ref[...]) + ba_ref[...]
    ola_ref[...] = _log_sigmoid(xa) * (1.0 / GLA_TAU)
    bd = bd_ref[...]
    osq_ref[...] = _group_rms(proj(_C_SQ, SWA_W), gsq_ref[...], bd, SWA_HD)
    osk_ref[...] = _group_rms(proj(_C_SK, SWA_W), gsk_ref[...], bd, SWA_HD)

    if classes:
        kT_ref, vT_ref = rest[-3], rest[-2]
        rest = rest[:-3] + rest[-1:]

        @pl.when(pl.program_id(0) % seq_tiles >= seq_tiles - win_tiles)
        def _():
            kT_ref[...] = osk_ref[...].T
            vT_ref[...] = osv_ref[...].T

        cls_refs, col_scr = rest[:-1], rest[-1]
        tm = x_ref.shape[0]
        n_col = SWA_W // LANES
        for j, src in enumerate((osq_ref, osk_ref, osv_ref)):
            for g in range(n_col):
                col_scr[g] = src[:, g * LANES:(g + 1) * LANES]
            for c, d in enumerate(CLASS_DILS):
                dst = cls_refs[j * len(CLASS_DILS) + c]
                for r in range(d):
                    for g in range(n_col):
                        lo = r * SWA_W + g * LANES
                        dst[:, lo:lo + LANES] = col_scr[g, pl.ds(r, tm // d, stride=d), :].astype(BF16)


def _in_proj(x, gmix, w_in_p, wa2_p, ba, gsq, gsk, bd, tm, classes, seq_len=0, win_len=0):
    n = x.shape[0]
    row = lambda w: pl.BlockSpec((tm, w), lambda i: (i, 0))
    full = lambda a: pl.BlockSpec(a.shape, lambda i: (0,) * a.ndim)
    widths = (GLA_QK_W, GLA_QK_W, GLA_V_W, GLA_V_W, GLA_QK_W, SWA_W, SWA_W, SWA_W)
    out_specs = [row(w) for w in widths]
    out_shape = [jax.ShapeDtypeStruct((n, w), F32) for w in widths]
    seq_tiles, win_tiles = seq_len // tm, win_len // tm
    if classes:
        for _ in range(3):
            for d in CLASS_DILS:
                out_specs.append(pl.BlockSpec((tm // d, d * SWA_W), lambda i: (i, 0)))
                out_shape.append(jax.ShapeDtypeStruct((n // d, d * SWA_W), BF16))
        win = pl.BlockSpec((None, SWA_W, tm), lambda i: (i // seq_tiles, 0,
                                                         jnp.maximum(i % seq_tiles - (seq_tiles - win_tiles), 0)))
        out_specs += [win, win]
        out_shape += [jax.ShapeDtypeStruct((n // seq_len, SWA_W, win_len), F32)] * 2
    return pl.pallas_call(
        functools.partial(_in_proj_kernel, classes=classes, seq_tiles=seq_tiles, win_tiles=win_tiles),
        grid=(n // tm,),
        in_specs=[row(D_MODEL), full(gmix), full(w_in_p), full(wa2_p), full(ba), full(gsq), full(gsk), full(bd)],
        out_specs=out_specs,
        out_shape=out_shape,
        scratch_shapes=[pltpu.VMEM((SWA_W // LANES, tm, LANES), F32)] if classes else [],
        compiler_params=_cparams("arbitrary"),
        name="in_proj",
    )(x, gmix, w_in_p, wa2_p, ba, gsq, gsk, bd)


GLA_NB = 2


def _gla_kernel(q_ref, k_ref, v_ref, la_ref, r_ref, s0_ref, gout_ref, lcat_ref,
                o_ref, s_ref, s_scr, knt_scr, rhs_scr, *, n_chunks):
    tb = pl.program_id(1)
    C, S, NS, H, DK, DV = GLA_CHUNK, GLA_SUB, GLA_CHUNK // GLA_SUB, GLA_HEADS, GLA_DK, GLA_DV

    @pl.when(tb == 0)
    def _():
        s_scr[...] = s0_ref[...]
        knt_scr[...] = jnp.zeros(knt_scr.shape, BF16)
        rhs_scr[...] = jnp.zeros(rhs_scr.shape, BF16)

    lcat = lcat_ref[...]
    gout = gout_ref[...]
    row = lax.broadcasted_iota(jnp.int32, (C, H * DK), 0)
    tril = (lax.broadcasted_iota(jnp.int32, (C, H * C), 1) & (C - 1)) <= lax.broadcasted_iota(jnp.int32, (C, H * C), 0)
    ones_k = jnp.ones((3 * C, LANES), BF16)

    def chunk(c, carry):
        r0 = pl.multiple_of(c * C, C)
        rows = pl.ds(r0, C)
        for n in range(GLA_NB):
            g = la_ref[n, rows, :]
            b_r = _dot(lcat, jnp.concatenate(_split_bf16(g, 3), axis=1))
            b_r = b_r[:, :H * DK] + b_r[:, H * DK:2 * H * DK] + b_r[:, 2 * H * DK:]
            b = b_r[:C]
            ref = b_r[C:]
            b_end = b[C - 1:C, :]
            b_col = _dot(jnp.concatenate(_split_bf16(g.T, 3), axis=1), ones_k)
            q = q_ref[n, rows, :]
            k = k_ref[n, rows, :]
            v = v_ref[n, rows, :]
            q_hat = q * jnp.exp(b - ref)
            q_til = q * jnp.exp(b)
            k_dec = k * jnp.exp(b_end - b)
            for i in range(NS):
                e = ref[S * i:S * i + 1, :] - b
                piece = (k * jnp.exp(jnp.where(row < S * (i + 1), e, NEG))).astype(BF16)
                for h in range(H):
                    lo = i * H * DK + h * DK
                    knt_scr[n, h * C:(h + 1) * C, lo:lo + DK] = piece[:, h * DK:(h + 1) * DK]
            q_cat = jnp.concatenate([jnp.where((row >= S * i) & (row < S * (i + 1)), q_hat, 0.0) for i in range(NS)],
                                    axis=1).astype(BF16)
            sc = jnp.where(tril, _dot_nt(q_cat, knt_scr[n]), 0.0)
            st = s_scr[n]
            for h in range(H):
                vl = slice(h * DV, (h + 1) * DV)
                rhs_scr[n, h * C:(h + 1) * C, vl] = v[:, vl].astype(BF16)
                rhs_scr[n, H * C + h * DK:H * C + (h + 1) * DK, vl] = st[h * DK:(h + 1) * DK, :].astype(BF16)
            o = _dot(jnp.concatenate([sc, q_til], axis=1).astype(BF16), rhs_scr[n])
            kv = _dot(k_dec.T.astype(BF16), v.astype(BF16))
            s_scr[n] = st * jnp.exp(b_col) + jnp.concatenate(
                [kv[h * DK:(h + 1) * DK, h * DV:(h + 1) * DV] for h in range(H)], axis=0)
            gate = r_ref[n, rows, :]
            for h in range(H):
                vl = slice(h * DV, (h + 1) * DV)
                o_ref[n, rows, vl] = (_rms_rows(o[:, vl], gout) * (gate[:, vl] * _sigmoid(gate[:, vl]))).astype(o_ref.dtype)
        return carry

    lax.fori_loop(0, n_chunks, chunk, 0)

    @pl.when(tb == pl.num_programs(1) - 1)
    def _():
        s_ref[...] = s_scr[...]


def _gla_consts():
    C, S = GLA_CHUNK, GLA_SUB
    t = np.arange(C)
    incl = (t[None, :] <= t[:, None]).astype(np.float32)
    upto = (t[None, :] < (t[:, None] // S) * S).astype(np.float32)
    return jnp.asarray(np.concatenate([incl, upto], axis=0), BF16)


def _gla(q, k, v, la, r, s0, gout, tb):
    B, T, _ = q.shape
    H, C, DK, DV, NS = GLA_HEADS, GLA_CHUNK, GLA_DK, GLA_DV, GLA_CHUNK // GLA_SUB
    lcat = _gla_consts()
    blk = lambda w: pl.BlockSpec((GLA_NB, tb, w), lambda b, t: (b, t, 0))
    full = lambda a: pl.BlockSpec(a.shape, lambda b, t: (0,) * a.ndim)
    st_spec = pl.BlockSpec((GLA_NB, H * DK, DV), lambda b, t: (b, 0, 0))
    return pl.pallas_call(
        functools.partial(_gla_kernel, n_chunks=tb // C),
        grid=(B // GLA_NB, T // tb),
        in_specs=[blk(GLA_QK_W), blk(GLA_QK_W), blk(GLA_V_W), blk(GLA_QK_W), blk(GLA_V_W), st_spec,
                  full(gout), full(lcat)],
        out_specs=[blk(GLA_V_W), st_spec],
        out_shape=[jax.ShapeDtypeStruct((B, T, GLA_V_W), BF16),
                   jax.ShapeDtypeStruct((B, H * DK, DV), F32)],
        scratch_shapes=[pltpu.VMEM((GLA_NB, H * DK, DV), F32),
                        pltpu.VMEM((GLA_NB, H * C, NS * H * DK), BF16),
                        pltpu.VMEM((GLA_NB, H * C + H * DK, H * DV), BF16)],
        compiler_params=_cparams("parallel", "arbitrary"),
        name="gla",
    )(q, k, v, la, r, s0, gout, lcat)


def _band_attn_kernel(q_ref, kp_ref, kc_ref, vp_ref, vc_ref, o_ref, lse_ref, k_scr, v_scr, *, dil, n_sub, qs):
    i = pl.program_id(2)
    k_scr[0:BAND, :] = kp_ref[...].astype(BF16)
    k_scr[BAND:, :] = kc_ref[...].astype(BF16)
    v_scr[0:BAND, :] = vp_ref[...].astype(BF16)
    v_scr[BAND:, :] = vc_ref[...].astype(BF16)
    nk = qs + BAND
    t = lax.broadcasted_iota(jnp.int32, (qs, nk), 0)
    c = lax.broadcasted_iota(jnp.int32, (qs, nk), 1)
    dist = BAND + t - c
    in_band = (dist >= 0) & (dist <= BAND)
    distf = (dist * dil).astype(F32)
    first = lax.broadcasted_iota(jnp.int32, (nk, LANES), 1) < SWA_HD
    first_q = lax.broadcasted_iota(jnp.int32, (qs, LANES), 1) < SWA_HD
    zero = jnp.zeros((nk, LANES), BF16)
    ones_bd = jnp.concatenate([first, jnp.logical_not(first)], axis=0).astype(BF16)

    def sub_block(j, carry):
        r0 = pl.multiple_of(j * qs, qs)
        rows = pl.ds(r0, qs)
        valid = in_band & ((c >= BAND) | (i > 0) | (j > 0))
        for pr in range(SWA_HEADS // 2):
            cols = slice(pr * LANES, (pr + 1) * LANES)
            q2 = (q_ref[rows, cols].astype(F32) * (SWA_HD ** -0.5)).astype(BF16)
            k2 = k_scr[pl.ds(r0, nk), cols]
            v2 = v_scr[pl.ds(r0, nk), cols]
            k_bd = jnp.concatenate([jnp.where(first, k2, zero), jnp.where(first, zero, k2)], axis=0)
            v_bd = jnp.concatenate([jnp.where(first, v2, zero), jnp.where(first, zero, v2)], axis=0)
            s2 = _dot_nt(q2, k_bd)
            ps, ms = [], []
            for u in range(2):
                s = s2[:, u * nk:(u + 1) * nk] - (2.0 ** -(2 * pr + u + 1)) * distf
                s = jnp.where(valid, s, NEG)
                m = jnp.max(s, axis=-1, keepdims=True)
                ps.append(jnp.exp(s - m).astype(BF16))
                ms.append(m)
            od = _dot(jnp.concatenate(ps, axis=1), jnp.concatenate([v_bd, ones_bd], axis=1))
            den = od[:, LANES:]
            o_ref[rows, cols] = od[:, :LANES] / den
            lse_ref[rows, cols] = jnp.where(first_q, ms[0], ms[1]) + jnp.log(den)
        return carry

    lax.fori_loop(0, n_sub, sub_block, 0)


def _band_attn(q, k, v, dil):
    B, Tc, dw = q.shape
    W = dw // dil
    qb = min(4 * BAND, Tc)
    qs = min(2 * BAND, Tc)
    n_sub = qb // qs
    cur = pl.BlockSpec((None, qb, W), lambda b, r, i: (b, i, r))
    prev = pl.BlockSpec((None, BAND, W), lambda b, r, i: (b, jnp.maximum(i * (qb // BAND) - 1, 0), r))
    o, lse = pl.pallas_call(
        functools.partial(_band_attn_kernel, dil=dil, n_sub=n_sub, qs=qs),
        grid=(B, dil, Tc // qb),
        in_specs=[cur, prev, cur, prev, cur],
        out_specs=[cur, cur],
        out_shape=[jax.ShapeDtypeStruct((B, Tc, dw), F32)] * 2,
        scratch_shapes=[pltpu.VMEM((BAND + qb, W), BF16)] * 2,
        compiler_params=_cparams("parallel", "parallel", "parallel"),
        name="band_attn_d%d" % dil,
    )(q, k, k, v, v)
    return o.reshape(B * Tc, dw), lse.reshape(B * Tc, dw)


def _pattern_count(dist):
    cnt = jnp.zeros(dist.shape, F32)
    for window, dil in DILATED_PATTERNS:
        hit = (dist >= 0) & (dist <= window)
        if dil > 1:
            hit = hit & ((dist & (dil - 1)) == 0)
        cnt = cnt + hit.astype(F32)
    return cnt


def _samp_attn_kernel(q_ref, kn_ref, vn_ref, kp_ref, vp_ref, o_ref, ko_ref, vo_ref):
    P, Tn = kp_ref.shape[2], kn_ref.shape[2]
    qi = lax.broadcasted_iota(jnp.int32, (Tn, P), 0)
    dist_p = P + qi - lax.broadcasted_iota(jnp.int32, (Tn, P), 1)
    dist_n = lax.broadcasted_iota(jnp.int32, (Tn, Tn), 0) - lax.broadcasted_iota(jnp.int32, (Tn, Tn), 1)
    cnt_p, cnt_n = _pattern_count(dist_p), _pattern_count(dist_n)
    dpf, dnf = dist_p.astype(F32), dist_n.astype(F32)
    q = (q_ref[...] * (SWA_HD ** -0.5)).astype(BF16)
    outs = []
    for h in range(SWA_HEADS):
        kp, vp, kn, vn = kp_ref[h], vp_ref[h], kn_ref[h], vn_ref[h]
        ko_ref[h] = pltpu.roll(kp, P - Tn, axis=1)
        ko_ref[h, :, P - Tn:] = kn
        vo_ref[h] = pltpu.roll(vp, P - Tn, axis=1)
        vo_ref[h, :, P - Tn:] = vn
        qh = q[:, h * SWA_HD:(h + 1) * SWA_HD]
        slope = 2.0 ** -(h + 1)
        lp = jnp.where(cnt_p > 0, _dot(qh, kp.astype(BF16)) - slope * dpf, NEG)
        ln = jnp.where(cnt_n > 0, _dot(qh, kn.astype(BF16)) - slope * dnf, NEG)
        m = jnp.maximum(jnp.max(lp, axis=-1, keepdims=True), jnp.max(ln, axis=-1, keepdims=True))
        pp = cnt_p * jnp.exp(lp - m)
        pn = cnt_n * jnp.exp(ln - m)
        den = jnp.sum(pp, axis=-1, keepdims=True) + jnp.sum(pn, axis=-1, keepdims=True)
        outs.append((_dot_nt(pp.astype(BF16), vp.astype(BF16)) + _dot_nt(pn.astype(BF16), vn.astype(BF16))) / den)
    o_ref[...] = jnp.concatenate(outs, axis=1)


def _samp_attn(q, knT, vnT, kpT, vpT):
    B, Tn, W = q.shape
    P = kpT.shape[-1]
    qs = pl.BlockSpec((None, Tn, W), lambda b: (b, 0, 0))
    new = pl.BlockSpec((None, SWA_HEADS, SWA_HD, Tn), lambda b: (b, 0, 0, 0))
    past = pl.BlockSpec((None, SWA_HEADS, SWA_HD, P), lambda b: (b, 0, 0, 0))
    return pl.pallas_call(
        _samp_attn_kernel,
        grid=(B,),
        in_specs=[qs, new, new, past, past],
        out_specs=[qs, past, past],
        out_shape=[jax.ShapeDtypeStruct((B, Tn, W), F32), jax.ShapeDtypeStruct(kpT.shape, F32),
                   jax.ShapeDtypeStruct(kpT.shape, F32)],
        compiler_params=_cparams("parallel"),
        name="samp_attn",
    )(q, knT, vnT, kpT, vpT)


def _mem_kv_kernel(mem_ref, gmem_ref, wk_ref, wv_ref, gxk_ref, mk_ref, mv_ref):
    hm = _rms_rows(mem_ref[...], gmem_ref[...]).astype(BF16)
    mk = _dot(hm, wk_ref[...])
    gxk = gxk_ref[...]
    mv = _dot(hm, wv_ref[...])
    n = mem_ref.shape[0]
    for h in range(X_HEADS):
        sl = slice(h * X_HD, (h + 1) * X_HD)
        mk_ref[pl.ds(h, n, stride=X_HEADS), :] = _rms_rows(mk[:, sl], gxk)
        mv_ref[pl.ds(h, n, stride=X_HEADS), :] = mv[:, sl]


def _mem_kv(mem, gmem, wk, wv, gxk):
    n = mem.shape[0]
    return pl.pallas_call(
        _mem_kv_kernel,
        out_shape=[jax.ShapeDtypeStruct((n * X_HEADS, X_HD), F32)] * 2,
        compiler_params=pltpu.CompilerParams(vmem_limit_bytes=VMEM_LIMIT),
        name="mem_kv",
    )(mem, gmem, wk, wv, gxk)


def _post_kernel(*refs, n_pat, n_seg):
    n_swa = 2 * n_pat if n_pat > 1 else 1
    x_ref, og_ref = refs[0], refs[1]
    swa_refs = refs[2:2 + n_swa]
    (wout_ref, gx_ref, wxq_ref, gxq_ref, mk_ref, mv_ref, wxo_ref, gffn_ref, wr_ref, br_ref, tri_ref,
     cnt0_ref) = refs[2 + n_swa:14 + n_swa]
    n_scr = 3 if n_pat > 1 else 2
    x2_ref, h3_ref, ti_ref, tg_ref, cnt_ref = refs[len(refs) - n_scr - 5:len(refs) - n_scr]
    ox_scr, run_scr = refs[len(refs) - n_scr:len(refs) - n_scr + 2]

    @pl.when(pl.program_id(0) == 0)
    def _():
        run_scr[...] = cnt0_ref[...]

    if n_pat > 1:
        dei_scr = refs[-1]
        tm = x_ref.shape[0]
        vals = []
        for a, r in enumerate(swa_refs):
            d = DILATED_PATTERNS[a % n_pat][1]
            if d == 1:
                vals.append(r[...])
            else:
                slot, n_col = len(vals), SWA_W // LANES
                for c in range(d):
                    for g in range(n_col):
                        lo = c * SWA_W + g * LANES
                        dei_scr[slot, g, pl.ds(c, tm // d, stride=d), :] = r[:, lo:lo + LANES]
                vals.append(jnp.concatenate([dei_scr[slot, g] for g in range(n_col)], axis=1))
        o_p, l_p = vals[:n_pat], vals[n_pat:]
        lmax = functools.reduce(jnp.maximum, l_p)
        w_p = [jnp.exp(l - lmax) for l in l_p]
        o_swa = sum(w * o for w, o in zip(w_p, o_p)) / sum(w_p)
    else:
        o_swa = swa_refs[0][...]

    x1 = x_ref[...] + _dot(og_ref[...], wout_ref[0:GLA_V_W, :]) + _dot(o_swa.astype(BF16), wout_ref[GLA_V_W:, :])

    q = _dot(_rms_rows(x1, gx_ref[...]).astype(BF16), wxq_ref[...])
    gxq = gxq_ref[...]
    rows_t, seg = x1.shape[0], x1.shape[0] // n_seg
    if n_seg > 1:
        own = (lax.broadcasted_iota(jnp.int32, (rows_t, n_seg * MEM_LEN), 0) // seg
               == lax.broadcasted_iota(jnp.int32, (rows_t, n_seg * MEM_LEN), 1) // MEM_LEN)
    for h in range(X_HEADS):
        sl = slice(h * X_HD, (h + 1) * X_HD)
        qn = _rms_rows(q[:, sl], gxq).astype(BF16)
        mem_rows = pl.ds(h, MEM_LEN, stride=X_HEADS)
        mk_h = jnp.concatenate([mk_ref[s, mem_rows, :] for s in range(n_seg)], axis=0).astype(BF16)
        mv_h = jnp.concatenate([mv_ref[s, mem_rows, :] for s in range(n_seg)], axis=0).astype(BF16)
        sc = _dot_nt(qn, mk_h) * (X_HD ** -0.5)
        if n_seg > 1:
            sc = jnp.where(own, sc, NEG)
        p = jnp.exp(sc - jnp.max(sc, axis=-1, keepdims=True))
        p = p / jnp.sum(p, axis=-1, keepdims=True)
        ox_scr[:, sl] = _dot(p.astype(BF16), mv_h)
    x2 = x1 + _dot(ox_scr[...].astype(BF16), wxo_ref[...])
    x2_ref[...] = x2

    h3 = _rms_rows(x2, gffn_ref[...]).astype(BF16)
    h3_ref[...] = h3
    work = _dot(h3, wr_ref[...]) + br_ref[...]
    lane = lax.broadcasted_iota(jnp.int32, work.shape, 1)
    vals, idxs = [], []
    for _ in range(TOP_K):
        m = jnp.max(work, axis=-1, keepdims=True)
        idx = jnp.min(jnp.where(work == m, lane, LANES), axis=-1, keepdims=True)
        vals.append(m)
        idxs.append(idx)
        work = jnp.where(lane == idx, -jnp.inf, work)
    es = [jnp.exp(v - vals[0]) for v in vals]
    den = sum(es)
    sel = [lane == idx for idx in idxs]
    onehot = functools.reduce(jnp.logical_or, sel).astype(BF16)
    run = run_scr[...]
    rank = _dot(tri_ref[...], onehot) + run
    run_scr[...] = run + jnp.sum(onehot.astype(F32), axis=0, keepdims=True)
    cnt_ref[...] = run_scr[...]
    ti = jnp.zeros(work.shape, jnp.int32)
    tg = jnp.zeros(work.shape, F32)
    for j in range(TOP_K):
        pos = jnp.sum(jnp.where(sel[j], rank, 0.0), axis=-1, keepdims=True).astype(jnp.int32)
        ti = jnp.where(lane == j, idxs[j], ti)
        ti = jnp.where(lane == TOP_K + j, pos, ti)
        tg = jnp.where(lane == j, es[j] / den, tg)
    ti_ref[...] = ti
    tg_ref[...] = tg


def _post(x, og, swa, wout, gx, wxq, gxq, mk, mv, wxo, gffn, wr, br, cnt0, tm, n_seg, rows_per_mem, h3_rows, h3_row0,
          h3_prior):
    n = x.shape[0]
    n_pat = len(swa) // 2 if len(swa) > 1 else 1
    tri = jnp.asarray(np.tril(np.ones((tm, tm), np.float32), -1), BF16)
    row = lambda w: pl.BlockSpec((tm, w), lambda i: (i, 0))
    full = lambda a: pl.BlockSpec(a.shape, lambda i: (0,) * a.ndim)
    mem = pl.BlockSpec((n_seg, MEM_LEN * X_HEADS, X_HD), lambda i: ((i * tm) // (rows_per_mem * n_seg), 0, 0))
    swa_specs = [pl.BlockSpec((tm * SWA_W // a.shape[1], a.shape[1]), lambda i: (i, 0)) for a in swa]
    scratch = [pltpu.VMEM((tm, X_W), F32), pltpu.VMEM((1, LANES), F32)]
    if n_pat > 1:
        scratch.append(pltpu.VMEM((len(swa), SWA_W // LANES, tm, LANES), F32))
    return pl.pallas_call(
        functools.partial(_post_kernel, n_pat=n_pat, n_seg=n_seg),
        grid=(n // tm,),
        in_specs=[row(D_MODEL), row(GLA_V_W)] + swa_specs
        + [full(wout), full(gx), full(wxq), full(gxq), mem, mem, full(wxo), full(gffn), full(wr), full(br),
           full(tri), full(cnt0)] + [pl.BlockSpec(memory_space=pl.ANY)] * len(h3_prior),
        out_specs=[row(D_MODEL), pl.BlockSpec((tm, D_MODEL), lambda i: (i + h3_row0 // tm, 0)), row(LANES), row(LANES),
                   full(cnt0)],
        out_shape=[jax.ShapeDtypeStruct((n, D_MODEL), F32), jax.ShapeDtypeStruct((h3_rows, D_MODEL), BF16),
                   jax.ShapeDtypeStruct((n, LANES), jnp.int32), jax.ShapeDtypeStruct((n, LANES), F32),
                   jax.ShapeDtypeStruct((1, LANES), F32)],
        scratch_shapes=scratch,
        compiler_params=_cparams("arbitrary"),
        name="post",
        input_output_aliases={14 + len(swa): 1} if h3_prior else {},
    )(x, og, *swa, wout, gx, wxq, gxq, mk, mv, wxo, gffn, wr, br, tri, cnt0, *h3_prior)


def _moe_kernel(te_ref, nu_ref, x_ref, wg_ref, bg_ref, wu_ref, bu_ref, wd_ref, bd_ref, *rest):
    y_ref, wg_scr, wu_scr, wd_scr = rest[-4:]
    i = pl.program_id(0)
    live = i < nu_ref[0]

    @pl.when(live & ((i == 0) | (te_ref[i] != te_ref[jnp.maximum(i - 1, 0)])))
    def _():
        wg_scr[...] = wg_ref[...].astype(BF16)
        wu_scr[...] = wu_ref[...].astype(BF16)
        wd_scr[...] = wd_ref[...].astype(BF16)

    @pl.when(live)
    def _():
        x = x_ref[...]
        g = jnp.minimum(_dot(x, wg_scr[...]) + bg_ref[...], SWIGLU_LIMIT)
        u = jnp.clip(_dot(x, wu_scr[...]) + bu_ref[...], -SWIGLU_LIMIT, SWIGLU_LIMIT)
        a = g * _sigmoid(SWIGLU_ALPHA * g) * (u + 1.0)
        y_ref[...] = (_dot(a.astype(BF16), wd_scr[...]) + bd_ref[...]).astype(y_ref.dtype)

    @pl.when(jnp.logical_not(live))
    def _():
        y_ref[...] = jnp.zeros(y_ref.shape, y_ref.dtype)


MOE_CHUNK_ENDS = (0.05, 0.25, 1.0)


def _moe(tile_expert, n_used, h3, src_tok, wg, bg, wu, bu, wd, bd):
    P = src_tok.shape[0]
    tm = MOE_TM
    n_tiles = P // tm
    ys = None
    ends = [max(1, round(n_tiles * f)) for f in MOE_CHUNK_ENDS]
    for lo, hi in zip([0] + ends[:-1], ends):
        w_spec = pl.BlockSpec((None, D_MODEL, D_MODEL), lambda i, te, *_: (te[i], 0, 0))
        b_spec = pl.BlockSpec((None, 1, D_MODEL), lambda i, te, *_: (te[i], 0, 0))
        prior = [] if ys is None else [ys]
        ys = pl.pallas_call(
            _moe_kernel,
            grid_spec=pltpu.PrefetchScalarGridSpec(
                num_scalar_prefetch=2,
                grid=(hi - lo,),
                in_specs=[pl.BlockSpec((tm, D_MODEL), lambda i, *_: (i, 0)),
                          w_spec, b_spec, w_spec, b_spec, w_spec, b_spec]
                + [pl.BlockSpec(memory_space=pl.ANY)] * len(prior),
                out_specs=pl.BlockSpec((tm, D_MODEL), lambda i, *_, lo=lo: (i + lo, 0)),
                scratch_shapes=[pltpu.VMEM((D_MODEL, D_MODEL), BF16)] * 3,
            ),
            out_shape=jax.ShapeDtypeStruct((P, D_MODEL), BF16),
            input_output_aliases={9: 0} if prior else {},
            compiler_params=_cparams("arbitrary"),
            name="moe",
        )(tile_expert[lo:hi], n_used - lo, _sc_take(h3, src_tok[lo * tm:hi * tm]), wg, bg, wu, bu, wd, bd, *prior)
    return ys


def _route(top_i, rank, counts, tm):
    n = top_i.shape[0]
    a = n * TOP_K
    n_tiles = a // tm + N_EXPERTS
    tiles_e = (counts + tm - 1) // tm
    tile_end = jnp.cumsum(tiles_e)
    slot0 = (tile_end - tiles_e) * tm
    experts = jnp.arange(N_EXPERTS, dtype=jnp.int32)
    dest = rank + jnp.sum(jnp.where(top_i[..., None] == experts, slot0, 0), axis=-1)
    n_used = tile_end[-1:].astype(jnp.int32)
    tile_expert = jnp.minimum(
        jnp.sum((tile_end[None, :] <= jnp.arange(n_tiles, dtype=jnp.int32)[:, None]).astype(jnp.int32), axis=1),
        N_EXPERTS - 1)
    pad_tok = -1 - (jnp.arange(n_tiles * tm, dtype=jnp.int32) % n)
    src_tok = pad_tok.at[dest.reshape(a)].max(jnp.arange(a, dtype=jnp.int32) // TOP_K, unique_indices=True)
    src_tok = jnp.where(src_tok < 0, -1 - src_tok, src_tok)
    return tile_expert, n_used, src_tok, dest


def _combine_kernel(x_ref, y_ref, g_ref, *rest):
    o_ref = rest[-1]
    acc = x_ref[...]
    g = g_ref[...]
    for j in range(TOP_K):
        acc = acc + g[:, j:j + 1] * y_ref[j].astype(F32)
    o_ref[...] = acc


COMBINE_CHUNKS = 8


def _combine(x2, ysg, tg, row0, tm, prior):
    n = ysg.shape[1]
    off = row0 // tm
    return pl.pallas_call(
        _combine_kernel,
        grid=(n // tm,),
        in_specs=[pl.BlockSpec((tm, D_MODEL), lambda i: (i + off, 0)),
                  pl.BlockSpec((TOP_K, tm, D_MODEL), lambda i: (0, i, 0)),
                  pl.BlockSpec((tm, LANES), lambda i: (i + off, 0))] + [pl.BlockSpec(memory_space=pl.ANY)] * len(prior),
        out_specs=pl.BlockSpec((tm, D_MODEL), lambda i: (i + off, 0)),
        out_shape=jax.ShapeDtypeStruct(x2.shape, F32),
        input_output_aliases={3: 0} if prior else {},
        compiler_params=_cparams("parallel"),
        name="combine",
    )(x2, ysg, tg, *prior)


@compute_on("tpu_sparsecore")
@jax.jit
def _sc_take(x, idx):
    return jnp.take(x, idx, axis=0, mode="clip")


def _mixer_inputs(x2d, w, tm, classes, seq_len=0, win_len=0):
    return _in_proj(x2d, w["gmix"], w["w_in_p"], w["wa2_p"], w["ba"], w["gsq"], w["gsk"], w["bd"], tm, classes,
                    seq_len, win_len)


def kernel(x_prompt, x_sample, mem_prompt, state_gla, cache_swa_k, cache_swa_v, cache_mem_k, cache_mem_v, g_mix, w_in, w_gla_a2, b_gla_a, g_gla_out, g_swa_q, g_swa_k, w_out, g_mem, w_mk, w_mv, g_xk, g_xattn, w_xq, g_xq, w_xo, g_ffn, w_router, b_router, w_gate, b_gate, w_up, b_up, w_down, b_down):
    B, T, D = x_prompt.shape
    Bs, Ts, _ = x_sample.shape
    P = cache_swa_k.shape[2]
    l = 0

    wi = w_in[l]
    segs = np.cumsum((0, GLA_QK_W, GLA_QK_W, GLA_V_W, GLA_V_W, GLA_RANK, SWA_W, SWA_W, SWA_W))
    gq_c, gk_c, gv_c, gr_c, ga_c, sq_c, sk_c, sv_c = [wi[:, segs[j]:segs[j + 1]] for j in range(8)]
    w_in_p = jnp.concatenate(
        [gq_c, gk_c, gv_c, gr_c, sq_c, sk_c, sv_c, ga_c, jnp.zeros((D, D_IN_PAD - _C_GA - GLA_RANK), F32)],
        axis=1).astype(BF16)
    heads = np.arange(SWA_W) // SWA_HD
    w = dict(
        gmix=g_mix[l][None], w_in_p=w_in_p,
        wa2_p=jnp.concatenate([w_gla_a2[l], jnp.zeros((LANES - GLA_RANK, GLA_QK_W), F32)], axis=0).astype(BF16),
        ba=b_gla_a[l][None],
        gsq=jnp.tile(g_swa_q[l], SWA_HEADS)[None], gsk=jnp.tile(g_swa_k[l], SWA_HEADS)[None],
        bd=jnp.asarray(heads[:, None] == heads[None, :], BF16),
    )
    gout = g_gla_out[l][None]
    wout = w_out[l].astype(BF16)
    wxq, wxo = w_xq[l].astype(BF16), w_xo[l].astype(BF16)
    wr = jnp.concatenate([w_router[l], jnp.zeros((D, LANES - N_EXPERTS), F32)], axis=1).astype(BF16)
    br = jnp.concatenate([b_router[l], jnp.full((LANES - N_EXPERTS,), NEG, F32)])[None]
    post_w = (wout, g_xattn[l][None], wxq, g_xq[l][None])
    post_w2 = (wxo, g_ffn[l][None], wr, br)

    w_buf = min(SWA_WINDOW, T)
    gq, gk, gv, gr, la, sq, sk, sv, *cls, kT_p, vT_p = _mixer_inputs(x_prompt.reshape(B * T, D), w, 512, True, T, w_buf)
    r3 = lambda a: a.reshape(B, T, a.shape[-1])
    og_p, st_p = _gla(r3(gq), r3(gk), r3(gv), r3(la), r3(gr),
                      jnp.zeros((B, GLA_QK_W, GLA_DV), F32), gout, 512)
    o_pat, l_pat = [], []
    for _, dil in DILATED_PATTERNS:
        if dil == 1:
            qkv = [r3(sq), r3(sk), r3(sv)]
        else:
            c = CLASS_DILS.index(dil)
            qkv = [cls[j * len(CLASS_DILS) + c].reshape(B, T // dil, dil * SWA_W) for j in range(3)]
        o_c, l_c = _band_attn(*qkv, dil)
        o_pat.append(o_c)
        l_pat.append(l_c)
    mk_p, mv_p = _mem_kv(mem_prompt.reshape(B * MEM_LEN, D), g_mem[l][None], w_mk[l].astype(BF16),
                         w_mv[l].astype(BF16), g_xk[l][None])
    n_p, n_all = B * T, B * T + Bs * Ts
    x2_p, h3, ti_p, tg_p, cnt_p = _post(x_prompt.reshape(B * T, D), og_p.reshape(B * T, GLA_V_W), o_pat + l_pat,
                                        *post_w, mk_p.reshape(B, MEM_LEN * X_HEADS, X_HD),
                                        mv_p.reshape(B, MEM_LEN * X_HEADS, X_HD),
                                        *post_w2, jnp.zeros((1, LANES), F32), tm=256, n_seg=1, rows_per_mem=T,
                                        h3_rows=n_all, h3_row0=0, h3_prior=[])

    gq, gk, gv, gr, la, sq, sk, sv = _mixer_inputs(x_sample.reshape(Bs * Ts, D), w, Bs * Ts, False)
    pad = lambda a: jnp.pad(a.reshape(Bs, Ts, a.shape[-1]), ((0, 0), (0, GLA_CHUNK - Ts), (0, 0)))
    og_s, st_s = _gla(pad(gq), pad(gk), pad(gv), pad(la), pad(gr),
                      state_gla[l].reshape(Bs, GLA_QK_W, GLA_DV), gout, GLA_CHUNK)
    og_s = og_s[:, :Ts].reshape(Bs * Ts, GLA_V_W)
    rows_minor = lambda a: jnp.transpose(a.reshape(Bs, -1, SWA_HEADS, SWA_HD), (0, 2, 3, 1))
    o_swa_s, kb_s, vb_s = _samp_attn(sq.reshape(Bs, Ts, SWA_W), rows_minor(sk), rows_minor(sv),
                                     rows_minor(cache_swa_k[l]), rows_minor(cache_swa_v[l]))
    kb_s, vb_s = jnp.transpose(kb_s, (0, 3, 1, 2)), jnp.transpose(vb_s, (0, 3, 1, 2))
    seqs = 8
    x2_s, h3, ti_s, tg_s, cnt = _post(x_sample.reshape(Bs * Ts, D), og_s, [o_swa_s.reshape(Bs * Ts, SWA_W)],
                                      *post_w, cache_mem_k[l].reshape(Bs, MEM_LEN * X_HEADS, X_HD),
                                      cache_mem_v[l].reshape(Bs, MEM_LEN * X_HEADS, X_HD),
                                      *post_w2, cnt_p, tm=seqs * Ts, n_seg=seqs, rows_per_mem=Ts,
                                      h3_rows=n_all, h3_row0=n_p, h3_prior=[h3])

    ti = jnp.concatenate([ti_p[:, :2 * TOP_K], ti_s[:, :2 * TOP_K]], axis=0)
    tile_expert, n_used, src_tok, dest = _route(ti[:, :TOP_K], ti[:, TOP_K:], cnt[0, :N_EXPERTS].astype(jnp.int32),
                                                MOE_TM)
    ys = _moe(tile_expert, n_used, h3, src_tok, w_gate[l], b_gate[l][:, None, :], w_up[l],
              b_up[l][:, None, :], w_down[l], b_down[l][:, None, :])
    def combined(x2, tg, row0, first, n, tm, prior):
        ysg = _sc_take(ys, dest[first:first + n].T.reshape(-1)).reshape(TOP_K, n, D)
        return [_combine(x2, ysg, tg, row0, tm, prior)]

    y_p, n_c = [], n_p // COMBINE_CHUNKS
    for c in range(COMBINE_CHUNKS):
        y_p = combined(x2_p, tg_p, c * n_c, c * n_c, n_c, 256, y_p)
    y_p, y_s = y_p[0], combined(x2_s, tg_s, 0, n_p, Bs * Ts, Bs * Ts, [])[0]

    kv_p = lambda a: jnp.transpose(a.reshape(B, SWA_HEADS, SWA_HD, w_buf), (0, 3, 1, 2))[None]
    return (y_p.reshape(B, T, D), y_s.reshape(Bs, Ts, D),
            st_p.reshape(1, B, GLA_HEADS, GLA_DK, GLA_DV), kv_p(kT_p), kv_p(vT_p),
            mk_p.reshape(1, B, MEM_LEN, X_HEADS, X_HD), mv_p.reshape(1, B, MEM_LEN, X_HEADS, X_HD),
            st_s.reshape(1, Bs, GLA_HEADS, GLA_DK, GLA_DV),
            kb_s.reshape(1, Bs, P, SWA_HEADS, SWA_HD), vb_s.reshape(1, Bs, P, SWA_HEADS, SWA_HD))
```

```python
import functools

import numpy as np
import jax
import jax.numpy as jnp
from jax import lax
from jax.experimental import pallas as pl
from jax.experimental.compute_on import compute_on
from jax.experimental.pallas import tpu as pltpu

F32 = jnp.float32
BF16 = jnp.bfloat16

D_MODEL = 1024
GLA_HEADS = 4
GLA_DK = 64
GLA_DV = 128
GLA_RANK = 16
GLA_TAU = 16.0
GLA_CHUNK = 128
GLA_SUB = 32
SWA_HEADS = 8
SWA_HD = 64
DILATED_PATTERNS = ((128, 1), (512, 4), (2048, 16))
BAND = 128
SWA_WINDOW = 2048
MEM_LEN = 256
X_HEADS = 4
X_HD = 128
N_EXPERTS = 32
TOP_K = 4
SWIGLU_ALPHA = 1.702
SWIGLU_LIMIT = 7.0
EPS = 1e-6

GLA_QK_W = GLA_HEADS * GLA_DK
GLA_V_W = GLA_HEADS * GLA_DV
SWA_W = SWA_HEADS * SWA_HD
X_W = X_HEADS * X_HD
LANES = 128
NEG = -1e30
VMEM_LIMIT = 56 * 1024 * 1024

_C_GQ, _C_GK, _C_GV, _C_GR, _C_SQ, _C_SK, _C_SV, _C_GA = 0, 256, 512, 1024, 1536, 2048, 2560, 3072
D_IN_PAD = 3200

MOE_TM = 512


def _cparams(*sem):
    return pltpu.CompilerParams(dimension_semantics=sem, vmem_limit_bytes=VMEM_LIMIT)


def _dot(a, b):
    return jnp.dot(a, b, preferred_element_type=F32)


def _dot_nt(a, b):
    return lax.dot_general(a, b, (((1,), (1,)), ((), ())), preferred_element_type=F32)


def _rms_rows(x, g):
    return x * lax.rsqrt(jnp.mean(x * x, axis=-1, keepdims=True) + EPS) * g


def _split_bf16(x, n):
    out = []
    for _ in range(n - 1):
        hi = x.astype(BF16)
        out.append(hi)
        x = x - hi.astype(F32)
    out.append(x.astype(BF16))
    return out


def _group_rms(z, g, ones_bd, group):
    hi, lo = _split_bf16(z * z, 2)
    ss = _dot(hi, ones_bd) + _dot(lo, ones_bd)
    return z * lax.rsqrt(ss * (1.0 / group) + EPS) * g


def _log_sigmoid(x):
    return jnp.minimum(x, 0.0) - jnp.log1p(jnp.exp(-jnp.abs(x)))


def _sigmoid(x):
    return 1.0 / (1.0 + jnp.exp(-x))


CLASS_DILS = tuple(d for _, d in DILATED_PATTERNS if d > 1)


def _in_proj_kernel(x_ref, gmix_ref, w_ref, wa2_ref, ba_ref, gsq_ref, gsk_ref, bd_ref,
                    oq_ref, ok_ref, ov_ref, or_ref, ola_ref, osq_ref, osk_ref, osv_ref, *rest, classes, seq_tiles,
                    win_tiles):
    h = _rms_rows(x_ref[...], gmix_ref[...]).astype(BF16)

    def proj(lo, width):
        return _dot(h, w_ref[:, lo:lo + width])

    oq_ref[...] = proj(_C_GQ, GLA_QK_W) * (GLA_DK ** -0.5)
    ok_ref[...] = proj(_C_GK, GLA_QK_W)
    ov_ref[...] = proj(_C_GV, GLA_V_W)
    or_ref[...] = proj(_C_GR, GLA_V_W)
    osv_ref[...] = proj(_C_SV, SWA_W)
    ga = proj(_C_GA, LANES)
    xa = _dot(ga.astype(BF16), wa2_ref[...]) + ba_ref[...]
    ola_ref[...] = _log_sigmoid(xa) * (1.0 / GLA_TAU)
    bd = bd_ref[...]
    osq_ref[...] = _group_rms(proj(_C_SQ, SWA_W), gsq_ref[...], bd, SWA_HD)
    osk_ref[...] = _group_rms(proj(_C_SK, SWA_W), gsk_ref[...], bd, SWA_HD)

    if classes:
        kT_ref, vT_ref = rest[-3], rest[-2]
        rest = rest[:-3] + rest[-1:]

        @pl.when(pl.program_id(0) % seq_tiles >= seq_tiles - win_tiles)
        def _():
            kT_ref[...] = osk_ref[...].T
            vT_ref[...] = osv_ref[...].T

        cls_refs, col_scr = rest[:-1], rest[-1]
        tm = x_ref.shape[0]
        n_col = SWA_W // LANES
        for j, src in enumerate((osq_ref, osk_ref, osv_ref)):
            for g in range(n_col):
                col_scr[g] = src[:, g * LANES:(g + 1) * LANES]
            for c, d in enumerate(CLASS_DILS):
                dst = cls_refs[j * len(CLASS_DILS) + c]
                for r in range(d):
                    for g in range(n_col):
                        lo = r * SWA_W + g * LANES
                        dst[:, lo:lo + LANES] = col_scr[g, pl.ds(r, tm // d, stride=d), :].astype(BF16)


def _in_proj(x, gmix, w_in_p, wa2_p, ba, gsq, gsk, bd, tm, classes, seq_len=0, win_len=0):
    n = x.shape[0]
    row = lambda w: pl.BlockSpec((tm, w), lambda i: (i, 0))
    full = lambda a: pl.BlockSpec(a.shape, lambda i: (0,) * a.ndim)
    widths = (GLA_QK_W, GLA_QK_W, GLA_V_W, GLA_V_W, GLA_QK_W, SWA_W, SWA_W, SWA_W)
    out_specs = [row(w) for w in widths]
    out_shape = [jax.ShapeDtypeStruct((n, w), F32) for w in widths]
    seq_tiles, win_tiles = seq_len // tm, win_len // tm
    if classes:
        for _ in range(3):
            for d in CLASS_DILS:
                out_specs.append(pl.BlockSpec((tm // d, d * SWA_W), lambda i: (i, 0)))
                out_shape.append(jax.ShapeDtypeStruct((n // d, d * SWA_W), BF16))
        win = pl.BlockSpec((None, SWA_W, tm), lambda i: (i // seq_tiles, 0,
                                                         jnp.maximum(i % seq_tiles - (seq_tiles - win_tiles), 0)))
        out_specs += [win, win]
        out_shape += [jax.ShapeDtypeStruct((n // seq_len, SWA_W, win_len), F32)] * 2
    return pl.pallas_call(
        functools.partial(_in_proj_kernel, classes=classes, seq_tiles=seq_tiles, win_tiles=win_tiles),
        grid=(n // tm,),
        in_specs=[row(D_MODEL), full(gmix), full(w_in_p), full(wa2_p), full(ba), full(gsq), full(gsk), full(bd)],
        out_specs=out_specs,
        out_shape=out_shape,
        scratch_shapes=[pltpu.VMEM((SWA_W // LANES, tm, LANES), F32)] if classes else [],
        compiler_params=_cparams("arbitrary"),
        name="in_proj",
    )(x, gmix, w_in_p, wa2_p, ba, gsq, gsk, bd)


GLA_NB = 2


def _gla_kernel(q_ref, k_ref, v_ref, la_ref, r_ref, s0_ref, gout_ref, lcat_ref,
                o_ref, s_ref, s_scr, knt_scr, rhs_scr, *, n_chunks):
    tb = pl.program_id(1)
    C, S, NS, H, DK, DV = GLA_CHUNK, GLA_SUB, GLA_CHUNK // GLA_SUB, GLA_HEADS, GLA_DK, GLA_DV

    @pl.when(tb == 0)
    def _():
        s_scr[...] = s0_ref[...]
        knt_scr[...] = jnp.zeros(knt_scr.shape, BF16)
        rhs_scr[...] = jnp.zeros(rhs_scr.shape, BF16)

    lcat = lcat_ref[...]
    gout = gout_ref[...]
    row = lax.broadcasted_iota(jnp.int32, (C, H * DK), 0)
    tril = (lax.broadcasted_iota(jnp.int32, (C, H * C), 1) & (C - 1)) <= lax.broadcasted_iota(jnp.int32, (C, H * C), 0)
    ones_k = jnp.ones((3 * C, LANES), BF16)

    def chunk(c, carry):
        r0 = pl.multiple_of(c * C, C)
        rows = pl.ds(r0, C)
        for n in range(GLA_NB):
            g = la_ref[n, rows, :]
            b_r = _dot(lcat, jnp.concatenate(_split_bf16(g, 3), axis=1))
            b_r = b_r[:, :H * DK] + b_r[:, H * DK:2 * H * DK] + b_r[:, 2 * H * DK:]
            b = b_r[:C]
            ref = b_r[C:]
            b_end = b[C - 1:C, :]
            b_col = _dot(jnp.concatenate(_split_bf16(g.T, 3), axis=1), ones_k)
            q = q_ref[n, rows, :]
            k = k_ref[n, rows, :]
            v = v_ref[n, rows, :]
            q_hat = q * jnp.exp(b - ref)
            q_til = q * jnp.exp(b)
            k_dec = k * jnp.exp(b_end - b)
            for i in range(NS):
                e = ref[S * i:S * i + 1, :] - b
                piece = (k * jnp.exp(jnp.where(row < S * (i + 1), e, NEG))).astype(BF16)
                for h in range(H):
                    lo = i * H * DK + h * DK
                    knt_scr[n, h * C:(h + 1) * C, lo:lo + DK] = piece[:, h * DK:(h + 1) * DK]
            q_cat = jnp.concatenate([jnp.where((row >= S * i) & (row < S * (i + 1)), q_hat, 0.0) for i in range(NS)],
                                    axis=1).astype(BF16)
            sc = jnp.where(tril, _dot_nt(q_cat, knt_scr[n]), 0.0)
            st = s_scr[n]
            for h in range(H):
                vl = slice(h * DV, (h + 1) * DV)
                rhs_scr[n, h * C:(h + 1) * C, vl] = v[:, vl].astype(BF16)
                rhs_scr[n, H * C + h * DK:H * C + (h + 1) * DK, vl] = st[h * DK:(h + 1) * DK, :].astype(BF16)
            o = _dot(jnp.concatenate([sc, q_til], axis=1).astype(BF16), rhs_scr[n])
            kv = _dot(k_dec.T.astype(BF16), v.astype(BF16))
            s_scr[n] = st * jnp.exp(b_col) + jnp.concatenate(
                [kv[h * DK:(h + 1) * DK, h * DV:(h + 1) * DV] for h in range(H)], axis=0)
            gate = r_ref[n, rows, :]
            for h in range(H):
                vl = slice(h * DV, (h + 1) * DV)
                o_ref[n, rows, vl] = (_rms_rows(o[:, vl], gout) * (gate[:, vl] * _sigmoid(gate[:, vl]))).astype(o_ref.dtype)
        return carry

    lax.fori_loop(0, n_chunks, chunk, 0)

    @pl.when(tb == pl.num_programs(1) - 1)
    def _():
        s_ref[...] = s_scr[...]


def _gla_consts():
    C, S = GLA_CHUNK, GLA_SUB
    t = np.arange(C)
    incl = (t[None, :] <= t[:, None]).astype(np.float32)
    upto = (t[None, :] < (t[:, None] // S) * S).astype(np.float32)
    return jnp.asarray(np.concatenate([incl, upto], axis=0), BF16)


def _gla(q, k, v, la, r, s0, gout, tb):
    B, T, _ = q.shape
    H, C, DK, DV, NS = GLA_HEADS, GLA_CHUNK, GLA_DK, GLA_DV, GLA_CHUNK // GLA_SUB
    lcat = _gla_consts()
    blk = lambda w: pl.BlockSpec((GLA_NB, tb, w), lambda b, t: (b, t, 0))
    full = lambda a: pl.BlockSpec(a.shape, lambda b, t: (0,) * a.ndim)
    st_spec = pl.BlockSpec((GLA_NB, H * DK, DV), lambda b, t: (b, 0, 0))
    return pl.pallas_call(
        functools.partial(_gla_kernel, n_chunks=tb // C),
        grid=(B // GLA_NB, T // tb),
        in_specs=[blk(GLA_QK_W), blk(GLA_QK_W), blk(GLA_V_W), blk(GLA_QK_W), blk(GLA_V_W), st_spec,
                  full(gout), full(lcat)],
        out_specs=[blk(GLA_V_W), st_spec],
        out_shape=[jax.ShapeDtypeStruct((B, T, GLA_V_W), BF16),
                   jax.ShapeDtypeStruct((B, H * DK, DV), F32)],
        scratch_shapes=[pltpu.VMEM((GLA_NB, H * DK, DV), F32),
                        pltpu.VMEM((GLA_NB, H * C, NS * H * DK), BF16),
                        pltpu.VMEM((GLA_NB, H * C + H * DK, H * DV), BF16)],
        compiler_params=_cparams("parallel", "arbitrary"),
        name="gla",
    )(q, k, v, la, r, s0, gout, lcat)


def _band_attn_kernel(q_ref, kp_ref, kc_ref, vp_ref, vc_ref, o_ref, lse_ref, k_scr, v_scr, *, dil, n_sub, qs):
    i = pl.program_id(2)
    k_scr[0:BAND, :] = kp_ref[...].astype(BF16)
    k_scr[BAND:, :] = kc_ref[...].astype(BF16)
    v_scr[0:BAND, :] = vp_ref[...].astype(BF16)
    v_scr[BAND:, :] = vc_ref[...].astype(BF16)
    nk = qs + BAND
    t = lax.broadcasted_iota(jnp.int32, (qs, nk), 0)
    c = lax.broadcasted_iota(jnp.int32, (qs, nk), 1)
    dist = BAND + t - c
    in_band = (dist >= 0) & (dist <= BAND)
    distf = (dist * dil).astype(F32)
    first = lax.broadcasted_iota(jnp.int32, (nk, LANES), 1) < SWA_HD
    first_q = lax.broadcasted_iota(jnp.int32, (qs, LANES), 1) < SWA_HD
    zero = jnp.zeros((nk, LANES), BF16)
    ones_bd = jnp.concatenate([first, jnp.logical_not(first)], axis=0).astype(BF16)

    def sub_block(j, carry):
        r0 = pl.multiple_of(j * qs, qs)
        rows = pl.ds(r0, qs)
        valid = in_band & ((c >= BAND) | (i > 0) | (j > 0))
        for pr in range(SWA_HEADS // 2):
            cols = slice(pr * LANES, (pr + 1) * LANES)
            q2 = (q_ref[rows, cols].astype(F32) * (SWA_HD ** -0.5)).astype(BF16)
            k2 = k_scr[pl.ds(r0, nk), cols]
            v2 = v_scr[pl.ds(r0, nk), cols]
            k_bd = jnp.concatenate([jnp.where(first, k2, zero), jnp.where(first, zero, k2)], axis=0)
            v_bd = jnp.concatenate([jnp.where(first, v2, zero), jnp.where(first, zero, v2)], axis=0)
            s2 = _dot_nt(q2, k_bd)
            ps, ms = [], []
            for u in range(2):
                s = s2[:, u * nk:(u + 1) * nk] - (2.0 ** -(2 * pr + u + 1)) * distf
                s = jnp.where(valid, s, NEG)
                m = jnp.max(s, axis=-1, keepdims=True)
                ps.append(jnp.exp(s - m).astype(BF16))
                ms.append(m)
            od = _dot(jnp.concatenate(ps, axis=1), jnp.concatenate([v_bd, ones_bd], axis=1))
            den = od[:, LANES:]
            o_ref[rows, cols] = od[:, :LANES] / den
            lse_ref[rows, cols] = jnp.where(first_q, ms[0], ms[1]) + jnp.log(den)
        return carry

    lax.fori_loop(0, n_sub, sub_block, 0)


def _band_attn(q, k, v, dil):
    B, Tc, dw = q.shape
    W = dw // dil
    qb = min(4 * BAND, Tc)
    qs = min(2 * BAND, Tc)
    n_sub = qb // qs
    cur = pl.BlockSpec((None, qb, W), lambda b, r, i: (b, i, r))
    prev = pl.BlockSpec((None, BAND, W), lambda b, r, i: (b, jnp.maximum(i * (qb // BAND) - 1, 0), r))
    o, lse = pl.pallas_call(
        functools.partial(_band_attn_kernel, dil=dil, n_sub=n_sub, qs=qs),
        grid=(B, dil, Tc // qb),
        in_specs=[cur, prev, cur, prev, cur],
        out_specs=[cur, cur],
        out_shape=[jax.ShapeDtypeStruct((B, Tc, dw), F32)] * 2,
        scratch_shapes=[pltpu.VMEM((BAND + qb, W), BF16)] * 2,
        compiler_params=_cparams("parallel", "parallel", "parallel"),
        name="band_attn_d%d" % dil,
    )(q, k, k, v, v)
    return o.reshape(B * Tc, dw), lse.reshape(B * Tc, dw)


def _pattern_count(dist):
    cnt = jnp.zeros(dist.shape, F32)
    for window, dil in DILATED_PATTERNS:
        hit = (dist >= 0) & (dist <= window)
        if dil > 1:
            hit = hit & ((dist & (dil - 1)) == 0)
        cnt = cnt + hit.astype(F32)
    return cnt


def _samp_attn_kernel(q_ref, kn_ref, vn_ref, kp_ref, vp_ref, o_ref, ko_ref, vo_ref):
    P, Tn = kp_ref.shape[2], kn_ref.shape[2]
    qi = lax.broadcasted_iota(jnp.int32, (Tn, P), 0)
    dist_p = P + qi - lax.broadcasted_iota(jnp.int32, (Tn, P), 1)
    dist_n = lax.broadcasted_iota(jnp.int32, (Tn, Tn), 0) - lax.broadcasted_iota(jnp.int32, (Tn, Tn), 1)
    cnt_p, cnt_n = _pattern_count(dist_p), _pattern_count(dist_n)
    dpf, dnf = dist_p.astype(F32), dist_n.astype(F32)
    q = (q_ref[...] * (SWA_HD ** -0.5)).astype(BF16)
    outs = []
    for h in range(SWA_HEADS):
        kp, vp, kn, vn = kp_ref[h], vp_ref[h], kn_ref[h], vn_ref[h]
        ko_ref[h] = pltpu.roll(kp, P - Tn, axis=1)
        ko_ref[h, :, P - Tn:] = kn
        vo_ref[h] = pltpu.roll(vp, P - Tn, axis=1)
        vo_ref[h, :, P - Tn:] = vn
        qh = q[:, h * SWA_HD:(h + 1) * SWA_HD]
        slope = 2.0 ** -(h + 1)
        lp = jnp.where(cnt_p > 0, _dot(qh, kp.astype(BF16)) - slope * dpf, NEG)
        ln = jnp.where(cnt_n > 0, _dot(qh, kn.astype(BF16)) - slope * dnf, NEG)
        m = jnp.maximum(jnp.max(lp, axis=-1, keepdims=True), jnp.max(ln, axis=-1, keepdims=True))
        pp = cnt_p * jnp.exp(lp - m)
        pn = cnt_n * jnp.exp(ln - m)
        den = jnp.sum(pp, axis=-1, keepdims=True) + jnp.sum(pn, axis=-1, keepdims=True)
        outs.append((_dot_nt(pp.astype(BF16), vp.astype(BF16)) + _dot_nt(pn.astype(BF16), vn.astype(BF16))) / den)
    o_ref[...] = jnp.concatenate(outs, axis=1)


def _samp_attn(q, knT, vnT, kpT, vpT):
    B, Tn, W = q.shape
    P = kpT.shape[-1]
    qs = pl.BlockSpec((None, Tn, W), lambda b: (b, 0, 0))
    new = pl.BlockSpec((None, SWA_HEADS, SWA_HD, Tn), lambda b: (b, 0, 0, 0))
    past = pl.BlockSpec((None, SWA_HEADS, SWA_HD, P), lambda b: (b, 0, 0, 0))
    return pl.pallas_call(
        _samp_attn_kernel,
        grid=(B,),
        in_specs=[qs, new, new, past, past],
        out_specs=[qs, past, past],
        out_shape=[jax.ShapeDtypeStruct((B, Tn, W), F32), jax.ShapeDtypeStruct(kpT.shape, F32),
                   jax.ShapeDtypeStruct(kpT.shape, F32)],
        compiler_params=_cparams("parallel"),
        name="samp_attn",
    )(q, knT, vnT, kpT, vpT)


def _mem_kv_kernel(mem_ref, gmem_ref, wk_ref, wv_ref, gxk_ref, mk_ref, mv_ref):
    hm = _rms_rows(mem_ref[...], gmem_ref[...]).astype(BF16)
    mk = _dot(hm, wk_ref[...])
    gxk = gxk_ref[...]
    mv = _dot(hm, wv_ref[...])
    n = mem_ref.shape[0]
    for h in range(X_HEADS):
        sl = slice(h * X_HD, (h + 1) * X_HD)
        mk_ref[pl.ds(h, n, stride=X_HEADS), :] = _rms_rows(mk[:, sl], gxk)
        mv_ref[pl.ds(h, n, stride=X_HEADS), :] = mv[:, sl]


def _mem_kv(mem, gmem, wk, wv, gxk):
    n = mem.shape[0]
    return pl.pallas_call(
        _mem_kv_kernel,
        out_shape=[jax.ShapeDtypeStruct((n * X_HEADS, X_HD), F32)] * 2,
        compiler_params=pltpu.CompilerParams(vmem_limit_bytes=VMEM_LIMIT),
        name="mem_kv",
    )(mem, gmem, wk, wv, gxk)


def _post_kernel(*refs, n_pat, n_seg):
    n_swa = 2 * n_pat if n_pat > 1 else 1
    x_ref, og_ref = refs[0], refs[1]
    swa_refs = refs[2:2 + n_swa]
    (wout_ref, gx_ref, wxq_ref, gxq_ref, mk_ref, mv_ref, wxo_ref, gffn_ref, wr_ref, br_ref, tri_ref,
     cnt0_ref) = refs[2 + n_swa:14 + n_swa]
    n_scr = 3 if n_pat > 1 else 2
    x2_ref, h3_ref, ti_ref, tg_ref, cnt_ref = refs[len(refs) - n_scr - 5:len(refs) - n_scr]
    ox_scr, run_scr = refs[len(refs) - n_scr:len(refs) - n_scr + 2]

    @pl.when(pl.program_id(0) == 0)
    def _():
        run_scr[...] = cnt0_ref[...]

    if n_pat > 1:
        dei_scr = refs[-1]
        tm = x_ref.shape[0]
        vals = []
        for a, r in enumerate(swa_refs):
            d = DILATED_PATTERNS[a % n_pat][1]
            if d == 1:
                vals.append(r[...])
            else:
                slot, n_col = len(vals), SWA_W // LANES
                for c in range(d):
                    for g in range(n_col):
                        lo = c * SWA_W + g * LANES
                        dei_scr[slot, g, pl.ds(c, tm // d, stride=d), :] = r[:, lo:lo + LANES]
                vals.append(jnp.concatenate([dei_scr[slot, g] for g in range(n_col)], axis=1))
        o_p, l_p = vals[:n_pat], vals[n_pat:]
        lmax = functools.reduce(jnp.maximum, l_p)
        w_p = [jnp.exp(l - lmax) for l in l_p]
        o_swa = sum(w * o for w, o in zip(w_p, o_p)) / sum(w_p)
    else:
        o_swa = swa_refs[0][...]

    x1 = x_ref[...] + _dot(og_ref[...], wout_ref[0:GLA_V_W, :]) + _dot(o_swa.astype(BF16), wout_ref[GLA_V_W:, :])

    q = _dot(_rms_rows(x1, gx_ref[...]).astype(BF16), wxq_ref[...])
    gxq = gxq_ref[...]
    rows_t, seg = x1.shape[0], x1.shape[0] // n_seg
    if n_seg > 1:
        own = (lax.broadcasted_iota(jnp.int32, (rows_t, n_seg * MEM_LEN), 0) // seg
               == lax.broadcasted_iota(jnp.int32, (rows_t, n_seg * MEM_LEN), 1) // MEM_LEN)
    for h in range(X_HEADS):
        sl = slice(h * X_HD, (h + 1) * X_HD)
        qn = _rms_rows(q[:, sl], gxq).astype(BF16)
        mem_rows = pl.ds(h, MEM_LEN, stride=X_HEADS)
        mk_h = jnp.concatenate([mk_ref[s, mem_rows, :] for s in range(n_seg)], axis=0).astype(BF16)
        mv_h = jnp.concatenate([mv_ref[s, mem_rows, :] for s in range(n_seg)], axis=0).astype(BF16)
        sc = _dot_nt(qn, mk_h) * (X_HD ** -0.5)
        if n_seg > 1:
            sc = jnp.where(own, sc, NEG)
        p = jnp.exp(sc - jnp.max(sc, axis=-1, keepdims=True))
        p = p / jnp.sum(p, axis=-1, keepdims=True)
        ox_scr[:, sl] = _dot(p.astype(BF16), mv_h)
    x2 = x1 + _dot(ox_scr[...].astype(BF16), wxo_ref[...])
    x2_ref[...] = x2

    h3 = _rms_rows(x2, gffn_ref[...]).astype(BF16)
    h3_ref[...] = h3
    work = _dot(h3, wr_ref[...]) + br_ref[...]
    lane = lax.broadcasted_iota(jnp.int32, work.shape, 1)
    vals, idxs = [], []
    for _ in range(TOP_K):
        m = jnp.max(work, axis=-1, keepdims=True)
        idx = jnp.min(jnp.where(work == m, lane, LANES), axis=-1, keepdims=True)
        vals.append(m)
        idxs.append(idx)
        work = jnp.where(lane == idx, -jnp.inf, work)
    es = [jnp.exp(v - vals[0]) for v in vals]
    den = sum(es)
    sel = [lane == idx for idx in idxs]
    onehot = functools.reduce(jnp.logical_or, sel).astype(BF16)
    run = run_scr[...]
    rank = _dot(tri_ref[...], onehot) + run
    run_scr[...] = run + jnp.sum(onehot.astype(F32), axis=0, keepdims=True)
    cnt_ref[...] = run_scr[...]
    ti = jnp.zeros(work.shape, jnp.int32)
    tg = jnp.zeros(work.shape, F32)
    for j in range(TOP_K):
        pos = jnp.sum(jnp.where(sel[j], rank, 0.0), axis=-1, keepdims=True).astype(jnp.int32)
        ti = jnp.where(lane == j, idxs[j], ti)
        ti = jnp.where(lane == TOP_K + j, pos, ti)
        tg = jnp.where(lane == j, es[j] / den, tg)
    ti_ref[...] = ti
    tg_ref[...] = tg


def _post(x, og, swa, wout, gx, wxq, gxq, mk, mv, wxo, gffn, wr, br, cnt0, tm, n_seg, rows_per_mem, h3_rows, h3_row0,
          h3_prior):
    n = x.shape[0]
    n_pat = len(swa) // 2 if len(swa) > 1 else 1
    tri = jnp.asarray(np.tril(np.ones((tm, tm), np.float32), -1), BF16)
    row = lambda w: pl.BlockSpec((tm, w), lambda i: (i, 0))
    full = lambda a: pl.BlockSpec(a.shape, lambda i: (0,) * a.ndim)
    mem = pl.BlockSpec((n_seg, MEM_LEN * X_HEADS, X_HD), lambda i: ((i * tm) // (rows_per_mem * n_seg), 0, 0))
    swa_specs = [pl.BlockSpec((tm * SWA_W // a.shape[1], a.shape[1]), lambda i: (i, 0)) for a in swa]
    scratch = [pltpu.VMEM((tm, X_W), F32), pltpu.VMEM((1, LANES), F32)]
    if n_pat > 1:
        scratch.append(pltpu.VMEM((len(swa), SWA_W // LANES, tm, LANES), F32))
    return pl.pallas_call(
        functools.partial(_post_kernel, n_pat=n_pat, n_seg=n_seg),
        grid=(n // tm,),
        in_specs=[row(D_MODEL), row(GLA_V_W)] + swa_specs
        + [full(wout), full(gx), full(wxq), full(gxq), mem, mem, full(wxo), full(gffn), full(wr), full(br),
           full(tri), full(cnt0)] + [pl.BlockSpec(memory_space=pl.ANY)] * len(h3_prior),
        out_specs=[row(D_MODEL), pl.BlockSpec((tm, D_MODEL), lambda i: (i + h3_row0 // tm, 0)), row(LANES), row(LANES),
                   full(cnt0)],
        out_shape=[jax.ShapeDtypeStruct((n, D_MODEL), F32), jax.ShapeDtypeStruct((h3_rows, D_MODEL), BF16),
                   jax.ShapeDtypeStruct((n, LANES), jnp.int32), jax.ShapeDtypeStruct((n, LANES), F32),
                   jax.ShapeDtypeStruct((1, LANES), F32)],
        scratch_shapes=scratch,
        compiler_params=_cparams("arbitrary"),
        name="post",
        input_output_aliases={14 + len(swa): 1} if h3_prior else {},
    )(x, og, *swa, wout, gx, wxq, gxq, mk, mv, wxo, gffn, wr, br, tri, cnt0, *h3_prior)


def _moe_kernel(te_ref, nu_ref, x_ref, wg_ref, bg_ref, wu_ref, bu_ref, wd_ref, bd_ref, *rest):
    y_ref, wg_scr, wu_scr, wd_scr = rest[-4:]
    i = pl.program_id(0)
    live = i < nu_ref[0]

    @pl.when(live & ((i == 0) | (te_ref[i] != te_ref[jnp.maximum(i - 1, 0)])))
    def _():
        wg_scr[...] = wg_ref[...].astype(BF16)
        wu_scr[...] = wu_ref[...].astype(BF16)
        wd_scr[...] = wd_ref[...].astype(BF16)

    @pl.when(live)
    def _():
        x = x_ref[...]
        g = jnp.minimum(_dot(x, wg_scr[...]) + bg_ref[...], SWIGLU_LIMIT)
        u = jnp.clip(_dot(x, wu_scr[...]) + bu_ref[...], -SWIGLU_LIMIT, SWIGLU_LIMIT)
        a = g * _sigmoid(SWIGLU_ALPHA * g) * (u + 1.0)
        y_ref[...] = (_dot(a.astype(BF16), wd_scr[...]) + bd_ref[...]).astype(y_ref.dtype)

    @pl.when(jnp.logical_not(live))
    def _():
        y_ref[...] = jnp.zeros(y_ref.shape, y_ref.dtype)


MOE_CHUNK_ENDS = (0.05, 0.25, 1.0)


def _moe(tile_expert, n_used, h3, src_tok, wg, bg, wu, bu, wd, bd):
    P = src_tok.shape[0]
    tm = MOE_TM
    n_tiles = P // tm
    ys = None
    ends = [max(1, round(n_tiles * f)) for f in MOE_CHUNK_ENDS]
    for lo, hi in zip([0] + ends[:-1], ends):
        w_spec = pl.BlockSpec((None, D_MODEL, D_MODEL), lambda i, te, *_: (te[i], 0, 0))
        b_spec = pl.BlockSpec((None, 1, D_MODEL), lambda i, te, *_: (te[i], 0, 0))
        prior = [] if ys is None else [ys]
        ys = pl.pallas_call(
            _moe_kernel,
            grid_spec=pltpu.PrefetchScalarGridSpec(
                num_scalar_prefetch=2,
                grid=(hi - lo,),
                in_specs=[pl.BlockSpec((tm, D_MODEL), lambda i, *_: (i, 0)),
                          w_spec, b_spec, w_spec, b_spec, w_spec, b_spec]
                + [pl.BlockSpec(memory_space=pl.ANY)] * len(prior),
                out_specs=pl.BlockSpec((tm, D_MODEL), lambda i, *_, lo=lo: (i + lo, 0)),
                scratch_shapes=[pltpu.VMEM((D_MODEL, D_MODEL), BF16)] * 3,
            ),
            out_shape=jax.ShapeDtypeStruct((P, D_MODEL), BF16),
            input_output_aliases={9: 0} if prior else {},
            compiler_params=_cparams("arbitrary"),
            name="moe",
        )(tile_expert[lo:hi], n_used - lo, _sc_take(h3, src_tok[lo * tm:hi * tm]), wg, bg, wu, bu, wd, bd, *prior)
    return ys


def _route(top_i, rank, counts, tm):
    n = top_i.shape[0]
    a = n * TOP_K
    n_tiles = a // tm + N_EXPERTS
    tiles_e = (counts + tm - 1) // tm
    tile_end = jnp.cumsum(tiles_e)
    slot0 = (tile_end - tiles_e) * tm
    experts = jnp.arange(N_EXPERTS, dtype=jnp.int32)
    dest = rank + jnp.sum(jnp.where(top_i[..., None] == experts, slot0, 0), axis=-1)
    n_used = tile_end[-1:].astype(jnp.int32)
    tile_expert = jnp.minimum(
        jnp.sum((tile_end[None, :] <= jnp.arange(n_tiles, dtype=jnp.int32)[:, None]).astype(jnp.int32), axis=1),
        N_EXPERTS - 1)
    pad_tok = -1 - (jnp.arange(n_tiles * tm, dtype=jnp.int32) % n)
    src_tok = pad_tok.at[dest.reshape(a)].max(jnp.arange(a, dtype=jnp.int32) // TOP_K, unique_indices=True)
    src_tok = jnp.where(src_tok < 0, -1 - src_tok, src_tok)
    return tile_expert, n_used, src_tok, dest


def _combine_kernel(x_ref, y_ref, g_ref, *rest):
    o_ref = rest[-1]
    acc = x_ref[...]
    g = g_ref[...]
    for j in range(TOP_K):
        acc = acc + g[:, j:j + 1] * y_ref[j].astype(F32)
    o_ref[...] = acc


COMBINE_CHUNKS = 8


def _combine(x2, ysg, tg, row0, tm, prior):
    n = ysg.shape[1]
    off = row0 // tm
    return pl.pallas_call(
        _combine_kernel,
        grid=(n // tm,),
        in_specs=[pl.BlockSpec((tm, D_MODEL), lambda i: (i + off, 0)),
                  pl.BlockSpec((TOP_K, tm, D_MODEL), lambda i: (0, i, 0)),
                  pl.BlockSpec((tm, LANES), lambda i: (i + off, 0))] + [pl.BlockSpec(memory_space=pl.ANY)] * len(prior),
        out_specs=pl.BlockSpec((tm, D_MODEL), lambda i: (i + off, 0)),
        out_shape=jax.ShapeDtypeStruct(x2.shape, F32),
        input_output_aliases={3: 0} if prior else {},
        compiler_params=_cparams("parallel"),
        name="combine",
    )(x2, ysg, tg, *prior)


@compute_on("tpu_sparsecore")
@jax.jit
def _sc_take(x, idx):
    return jnp.take(x, idx, axis=0, mode="clip")


def _mixer_inputs(x2d, w, tm, classes, seq_len=0, win_len=0):
    return _in_proj(x2d, w["gmix"], w["w_in_p"], w["wa2_p"], w["ba"], w["gsq"], w["gsk"], w["bd"], tm, classes,
                    seq_len, win_len)


def kernel(x_prompt, x_sample, mem_prompt, state_gla, cache_swa_k, cache_swa_v, cache_mem_k, cache_mem_v, g_mix, w_in, w_gla_a2, b_gla_a, g_gla_out, g_swa_q, g_swa_k, w_out, g_mem, w_mk, w_mv, g_xk, g_xattn, w_xq, g_xq, w_xo, g_ffn, w_router, b_router, w_gate, b_gate, w_up, b_up, w_down, b_down):
    B, T, D = x_prompt.shape
    Bs, Ts, _ = x_sample.shape
    P = cache_swa_k.shape[2]
    l = 0

    wi = w_in[l]
    segs = np.cumsum((0, GLA_QK_W, GLA_QK_W, GLA_V_W, GLA_V_W, GLA_RANK, SWA_W, SWA_W, SWA_W))
    gq_c, gk_c, gv_c, gr_c, ga_c, sq_c, sk_c, sv_c = [wi[:, segs[j]:segs[j + 1]] for j in range(8)]
    w_in_p = jnp.concatenate(
        [gq_c, gk_c, gv_c, gr_c, sq_c, sk_c, sv_c, ga_c, jnp.zeros((D, D_IN_PAD - _C_GA - GLA_RANK), F32)],
        axis=1).astype(BF16)
    heads = np.arange(SWA_W) // SWA_HD
    w = dict(
        gmix=g_mix[l][None], w_in_p=w_in_p,
        wa2_p=jnp.concatenate([w_gla_a2[l], jnp.zeros((LANES - GLA_RANK, GLA_QK_W), F32)], axis=0).astype(BF16),
        ba=b_gla_a[l][None],
        gsq=jnp.tile(g_swa_q[l], SWA_HEADS)[None], gsk=jnp.tile(g_swa_k[l], SWA_HEADS)[None],
        bd=jnp.asarray(heads[:, None] == heads[None, :], BF16),
    )
    gout = g_gla_out[l][None]
    wout = w_out[l].astype(BF16)
    wxq, wxo = w_xq[l].astype(BF16), w_xo[l].astype(BF16)
    wr = jnp.concatenate([w_router[l], jnp.zeros((D, LANES - N_EXPERTS), F32)], axis=1).astype(BF16)
    br = jnp.concatenate([b_router[l], jnp.full((LANES - N_EXPERTS,), NEG, F32)])[None]
    post_w = (wout, g_xattn[l][None], wxq, g_xq[l][None])
    post_w2 = (wxo, g_ffn[l][None], wr, br)

    w_buf = min(SWA_WINDOW, T)
    gq, gk, gv, gr, la, sq, sk, sv, *cls, kT_p, vT_p = _mixer_inputs(x_prompt.reshape(B * T, D), w, 512, True, T, w_buf)
    r3 = lambda a: a.reshape(B, T, a.shape[-1])
    og_p, st_p = _gla(r3(gq), r3(gk), r3(gv), r3(la), r3(gr),
                      jnp.zeros((B, GLA_QK_W, GLA_DV), F32), gout, 512)
    o_pat, l_pat = [], []
    for _, dil in DILATED_PATTERNS:
        if dil == 1:
            qkv = [r3(sq), r3(sk), r3(sv)]
        else:
            c = CLASS_DILS.index(dil)
            qkv = [cls[j * len(CLASS_DILS) + c].reshape(B, T // dil, dil * SWA_W) for j in range(3)]
        o_c, l_c = _band_attn(*qkv, dil)
        o_pat.append(o_c)
        l_pat.append(l_c)
    mk_p, mv_p = _mem_kv(mem_prompt.reshape(B * MEM_LEN, D), g_mem[l][None], w_mk[l].astype(BF16),
                         w_mv[l].astype(BF16), g_xk[l][None])
    n_p, n_all = B * T, B * T + Bs * Ts
    x2_p, h3, ti_p, tg_p, cnt_p = _post(x_prompt.reshape(B * T, D), og_p.reshape(B * T, GLA_V_W), o_pat + l_pat,
                                        *post_w, mk_p.reshape(B, MEM_LEN * X_HEADS, X_HD),
                                        mv_p.reshape(B, MEM_LEN * X_HEADS, X_HD),
                                        *post_w2, jnp.zeros((1, LANES), F32), tm=256, n_seg=1, rows_per_mem=T,
                                        h3_rows=n_all, h3_row0=0, h3_prior=[])

    gq, gk, gv, gr, la, sq, sk, sv = _mixer_inputs(x_sample.reshape(Bs * Ts, D), w, Bs * Ts, False)
    pad = lambda a: jnp.pad(a.reshape(Bs, Ts, a.shape[-1]), ((0, 0), (0, GLA_CHUNK - Ts), (0, 0)))
    og_s, st_s = _gla(pad(gq), pad(gk), pad(gv), pad(la), pad(gr),
                      state_gla[l].reshape(Bs, GLA_QK_W, GLA_DV), gout, GLA_CHUNK)
    og_s = og_s[:, :Ts].reshape(Bs * Ts, GLA_V_W)
    rows_minor = lambda a: jnp.transpose(a.reshape(Bs, -1, SWA_HEADS, SWA_HD), (0, 2, 3, 1))
    o_swa_s, kb_s, vb_s = _samp_attn(sq.reshape(Bs, Ts, SWA_W), rows_minor(sk), rows_minor(sv),
                                     rows_minor(cache_swa_k[l]), rows_minor(cache_swa_v[l]))
    kb_s, vb_s = jnp.transpose(kb_s, (0, 3, 1, 2)), jnp.transpose(vb_s, (0, 3, 1, 2))
    seqs = 8
    x2_s, h3, ti_s, tg_s, cnt = _post(x_sample.reshape(Bs * Ts, D), og_s, [o_swa_s.reshape(Bs * Ts, SWA_W)],
                                      *post_w, cache_mem_k[l].reshape(Bs, MEM_LEN * X_HEADS, X_HD),
                                      cache_mem_v[l].reshape(Bs, MEM_LEN * X_HEADS, X_HD),
                                      *post_w2, cnt_p, tm=seqs * Ts, n_seg=seqs, rows_per_mem=Ts,
                                      h3_rows=n_all, h3_row0=n_p, h3_prior=[h3])

    ti = jnp.concatenate([ti_p[:, :2 * TOP_K], ti_s[:, :2 * TOP_K]], axis=0)
    tile_expert, n_used, src_tok, dest = _route(ti[:, :TOP_K], ti[:, TOP_K:], cnt[0, :N_EXPERTS].astype(jnp.int32),
                                                MOE_TM)
    ys = _moe(tile_expert, n_used, h3, src_tok, w_gate[l], b_gate[l][:, None, :], w_up[l],
              b_up[l][:, None, :], w_down[l], b_down[l][:, None, :])
    def combined(x2, tg, row0, first, n, tm, prior):
        ysg = _sc_take(ys, dest[first:first + n].T.reshape(-1)).reshape(TOP_K, n, D)
        return [_combine(x2, ysg, tg, row0, tm, prior)]

    y_p, n_c = [], n_p // COMBINE_CHUNKS
    for c in range(COMBINE_CHUNKS):
        y_p = combined(x2_p, tg_p, c * n_c, c * n_c, n_c, 256, y_p)
    y_p, y_s = y_p[0], combined(x2_s, tg_s, 0, n_p, Bs * Ts, Bs * Ts, [])[0]

    kv_p = lambda a: jnp.transpose(a.reshape(B, SWA_HEADS, SWA_HD, w_buf), (0, 3, 1, 2))[None]
    return (y_p.reshape(B, T, D), y_s.reshape(Bs, Ts, D),
            st_p.reshape(1, B, GLA_HEADS, GLA_DK, GLA_DV), kv_p(kT_p), kv_p(vT_p),
            mk_p.reshape(1, B, MEM_LEN, X_HEADS, X_HD), mv_p.reshape(1, B, MEM_LEN, X_HEADS, X_HD),
            st_s.reshape(1, Bs, GLA_HEADS, GLA_DK, GLA_DV),
            kb_s.reshape(1, Bs, P, SWA_HEADS, SWA_HD), vb_s.reshape(1, Bs, P, SWA_HEADS, SWA_HD))
```

```python
import functools

import numpy as np
import jax
import jax.numpy as jnp
from jax import lax
from jax.experimental import pallas as pl
from jax.experimental.compute_on import compute_on
from jax.experimental.pallas import tpu as pltpu

F32 = jnp.float32
BF16 = jnp.bfloat16

D_MODEL = 1024
GLA_HEADS = 4
GLA_DK = 64
GLA_DV = 128
GLA_RANK = 16
GLA_TAU = 16.0
GLA_CHUNK = 128
GLA_SUB = 32
SWA_HEADS = 8
SWA_HD = 64
DILATED_PATTERNS = ((128, 1), (512, 4), (2048, 16))
BAND = 128
SWA_WINDOW = 2048
MEM_LEN = 256
X_HEADS = 4
X_HD = 128
N_EXPERTS = 32
TOP_K = 4
SWIGLU_ALPHA = 1.702
SWIGLU_LIMIT = 7.0
EPS = 1e-6

GLA_QK_W = GLA_HEADS * GLA_DK
GLA_V_W = GLA_HEADS * GLA_DV
SWA_W = SWA_HEADS * SWA_HD
X_W = X_HEADS * X_HD
LANES = 128
NEG = -1e30
VMEM_LIMIT = 56 * 1024 * 1024

_C_GQ, _C_GK, _C_GV, _C_GR, _C_SQ, _C_SK, _C_SV, _C_GA = 0, 256, 512, 1024, 1536, 2048, 2560, 3072
D_IN_PAD = 3200

IN_PROJ_TM = 512
GLA_TB = 512
POST_TM = 256
COMBINE_TM = 256
MOE_TM = 512
SAMPLE_SEQS = 8


def _cparams(*sem):
    return pltpu.CompilerParams(dimension_semantics=sem, vmem_limit_bytes=VMEM_LIMIT)


def _dot(a, b):
    return jnp.dot(a, b, preferred_element_type=F32)


def _dot_nt(a, b):
    return lax.dot_general(a, b, (((1,), (1,)), ((), ())), preferred_element_type=F32)


def _rms_rows(x, g):
    return x * lax.rsqrt(jnp.mean(x * x, axis=-1, keepdims=True) + EPS) * g


def _split_bf16(x, n):
    out = []
    for _ in range(n - 1):
        hi = x.astype(BF16)
        out.append(hi)
        x = x - hi.astype(F32)
    out.append(x.astype(BF16))
    return out


def _group_rms(z, g, ones_bd, group):
    hi, lo = _split_bf16(z * z, 2)
    ss = _dot(hi, ones_bd) + _dot(lo, ones_bd)
    return z * lax.rsqrt(ss * (1.0 / group) + EPS) * g


def _log_sigmoid(x):
    return jnp.minimum(x, 0.0) - jnp.log1p(jnp.exp(-jnp.abs(x)))


def _sigmoid(x):
    return 1.0 / (1.0 + jnp.exp(-x))


CLASS_DILS = tuple(d for _, d in DILATED_PATTERNS if d > 1)


def _in_proj_kernel(x_ref, gmix_ref, w_ref, wa2_ref, ba_ref, gsq_ref, gsk_ref, bd_ref,
                    oq_ref, ok_ref, ov_ref, or_ref, ola_ref, osq_ref, osk_ref, osv_ref, *rest, classes, seq_tiles,
                    win_tiles):
    h = _rms_rows(x_ref[...], gmix_ref[...]).astype(BF16)

    def proj(lo, width):
        return _dot(h, w_ref[:, lo:lo + width])

    oq_ref[...] = proj(_C_GQ, GLA_QK_W) * (GLA_DK ** -0.5)
    ok_ref[...] = proj(_C_GK, GLA_QK_W)
    ov_ref[...] = proj(_C_GV, GLA_V_W)
    or_ref[...] = proj(_C_GR, GLA_V_W)
    osv_ref[...] = proj(_C_SV, SWA_W)
    ga = proj(_C_GA, LANES)
    xa = _dot(ga.astype(BF16), wa2_ref[...]) + ba_ref[...]
    ola_ref[...] = _log_sigmoid(xa) * (1.0 / GLA_TAU)
    bd = bd_ref[...]
    osq_ref[...] = _group_rms(proj(_C_SQ, SWA_W), gsq_ref[...], bd, SWA_HD)
    osk_ref[...] = _group_rms(proj(_C_SK, SWA_W), gsk_ref[...], bd, SWA_HD)

    if classes:
        kT_ref, vT_ref = rest[-3], rest[-2]
        rest = rest[:-3] + rest[-1:]

        @pl.when(pl.program_id(0) % seq_tiles >= seq_tiles - win_tiles)
        def _():
            kT_ref[...] = osk_ref[...].T
            vT_ref[...] = osv_ref[...].T

        cls_refs, col_scr = rest[:-1], rest[-1]
        tm = x_ref.shape[0]
        n_col = SWA_W // LANES
        for j, src in enumerate((osq_ref, osk_ref, osv_ref)):
            for g in range(n_col):
                col_scr[g] = src[:, g * LANES:(g + 1) * LANES]
            for c, d in enumerate(CLASS_DILS):
                dst = cls_refs[j * len(CLASS_DILS) + c]
                for r in range(d):
                    for g in range(n_col):
                        lo = r * SWA_W + g * LANES
                        dst[:, lo:lo + LANES] = col_scr[g, pl.ds(r, tm // d, stride=d), :].astype(BF16)


def _in_proj(x, gmix, w_in_p, wa2_p, ba, gsq, gsk, bd, tm, classes, seq_len=0, win_len=0):
    n = x.shape[0]
    row = lambda w: pl.BlockSpec((tm, w), lambda i: (i, 0))
    full = lambda a: pl.BlockSpec(a.shape, lambda i: (0,) * a.ndim)
    widths = (GLA_QK_W, GLA_QK_W, GLA_V_W, GLA_V_W, GLA_QK_W, SWA_W, SWA_W, SWA_W)
    out_specs = [row(w) for w in widths]
    out_shape = [jax.ShapeDtypeStruct((n, w), F32) for w in widths]
    seq_tiles, win_tiles = seq_len // tm, win_len // tm
    if classes:
        for _ in range(3):
            for d in CLASS_DILS:
                out_specs.append(pl.BlockSpec((tm // d, d * SWA_W), lambda i: (i, 0)))
                out_shape.append(jax.ShapeDtypeStruct((n // d, d * SWA_W), BF16))
        win = pl.BlockSpec((None, SWA_W, tm), lambda i: (i // seq_tiles, 0,
                                                         jnp.maximum(i % seq_tiles - (seq_tiles - win_tiles), 0)))
        out_specs += [win, win]
        out_shape += [jax.ShapeDtypeStruct((n // seq_len, SWA_W, win_len), F32)] * 2
    return pl.pallas_call(
        functools.partial(_in_proj_kernel, classes=classes, seq_tiles=seq_tiles, win_tiles=win_tiles),
        grid=(n // tm,),
        in_specs=[row(D_MODEL), full(gmix), full(w_in_p), full(wa2_p), full(ba), full(gsq), full(gsk), full(bd)],
        out_specs=out_specs,
        out_shape=out_shape,
        scratch_shapes=[pltpu.VMEM((SWA_W // LANES, tm, LANES), F32)] if classes else [],
        compiler_params=_cparams("arbitrary"),
        name="in_proj",
    )(x, gmix, w_in_p, wa2_p, ba, gsq, gsk, bd)


GLA_NB = 2


def _gla_kernel(q_ref, k_ref, v_ref, la_ref, r_ref, s0_ref, gout_ref, lcat_ref,
                o_ref, s_ref, s_scr, knt_scr, rhs_scr, *, n_chunks):
    tb = pl.program_id(1)
    C, S, NS, H, DK, DV = GLA_CHUNK, GLA_SUB, GLA_CHUNK // GLA_SUB, GLA_HEADS, GLA_DK, GLA_DV

    @pl.when(tb == 0)
    def _():
        s_scr[...] = s0_ref[...]
        knt_scr[...] = jnp.zeros(knt_scr.shape, BF16)
        rhs_scr[...] = jnp.zeros(rhs_scr.shape, BF16)

    lcat = lcat_ref[...]
    gout = gout_ref[...]
    row = lax.broadcasted_iota(jnp.int32, (C, H * DK), 0)
    tril = (lax.broadcasted_iota(jnp.int32, (C, H * C), 1) & (C - 1)) <= lax.broadcasted_iota(jnp.int32, (C, H * C), 0)
    ones_k = jnp.ones((3 * C, LANES), BF16)

    def chunk(c, carry):
        r0 = pl.multiple_of(c * C, C)
        rows = pl.ds(r0, C)
        for n in range(GLA_NB):
            g = la_ref[n, rows, :]
            b_r = _dot(lcat, jnp.concatenate(_split_bf16(g, 3), axis=1))
            b_r = b_r[:, :H * DK] + b_r[:, H * DK:2 * H * DK] + b_r[:, 2 * H * DK:]
            b = b_r[:C]
            ref = b_r[C:]
            b_end = b[C - 1:C, :]
            b_col = _dot(jnp.concatenate(_split_bf16(g.T, 3), axis=1), ones_k)
            q = q_ref[n, rows, :]
            k = k_ref[n, rows, :]
            v = v_ref[n, rows, :]
            q_hat = q * jnp.exp(b - ref)
            q_til = q * jnp.exp(b)
            k_dec = k * jnp.exp(b_end - b)
            for i in range(NS):
                e = ref[S * i:S * i + 1, :] - b
                piece = (k * jnp.exp(jnp.where(row < S * (i + 1), e, NEG))).astype(BF16)
                for h in range(H):
                    lo = i * H * DK + h * DK
                    knt_scr[n, h * C:(h + 1) * C, lo:lo + DK] = piece[:, h * DK:(h + 1) * DK]
            q_cat = jnp.concatenate([jnp.where((row >= S * i) & (row < S * (i + 1)), q_hat, 0.0) for i in range(NS)],
                                    axis=1).astype(BF16)
            sc = jnp.where(tril, _dot_nt(q_cat, knt_scr[n]), 0.0)
            st = s_scr[n]
            for h in range(H):
                vl = slice(h * DV, (h + 1) * DV)
                rhs_scr[n, h * C:(h + 1) * C, vl] = v[:, vl].astype(BF16)
                rhs_scr[n, H * C + h * DK:H * C + (h + 1) * DK, vl] = st[h * DK:(h + 1) * DK, :].astype(BF16)
            o = _dot(jnp.concatenate([sc, q_til], axis=1).astype(BF16), rhs_scr[n])
            kv = _dot(k_dec.T.astype(BF16), v.astype(BF16))
            s_scr[n] = st * jnp.exp(b_col) + jnp.concatenate(
                [kv[h * DK:(h + 1) * DK, h * DV:(h + 1) * DV] for h in range(H)], axis=0)
            gate = r_ref[n, rows, :]
            for h in range(H):
                vl = slice(h * DV, (h + 1) * DV)
                o_ref[n, rows, vl] = (_rms_rows(o[:, vl], gout) * (gate[:, vl] * _sigmoid(gate[:, vl]))).astype(o_ref.dtype)
        return carry

    lax.fori_loop(0, n_chunks, chunk, 0)

    @pl.when(tb == pl.num_programs(1) - 1)
    def _():
        s_ref[...] = s_scr[...]


def _gla_consts():
    C, S = GLA_CHUNK, GLA_SUB
    t = np.arange(C)
    incl = (t[None, :] <= t[:, None]).astype(np.float32)
    upto = (t[None, :] < (t[:, None] // S) * S).astype(np.float32)
    return jnp.asarray(np.concatenate([incl, upto], axis=0), BF16)


def _gla(q, k, v, la, r, s0, gout, tb):
    B, T, _ = q.shape
    H, C, DK, DV, NS = GLA_HEADS, GLA_CHUNK, GLA_DK, GLA_DV, GLA_CHUNK // GLA_SUB
    lcat = _gla_consts()
    blk = lambda w: pl.BlockSpec((GLA_NB, tb, w), lambda b, t: (b, t, 0))
    full = lambda a: pl.BlockSpec(a.shape, lambda b, t: (0,) * a.ndim)
    st_spec = pl.BlockSpec((GLA_NB, H * DK, DV), lambda b, t: (b, 0, 0))
    return pl.pallas_call(
        functools.partial(_gla_kernel, n_chunks=tb // C),
        grid=(B // GLA_NB, T // tb),
        in_specs=[blk(GLA_QK_W), blk(GLA_QK_W), blk(GLA_V_W), blk(GLA_QK_W), blk(GLA_V_W), st_spec,
                  full(gout), full(lcat)],
        out_specs=[blk(GLA_V_W), st_spec],
        out_shape=[jax.ShapeDtypeStruct((B, T, GLA_V_W), BF16),
                   jax.ShapeDtypeStruct((B, H * DK, DV), F32)],
        scratch_shapes=[pltpu.VMEM((GLA_NB, H * DK, DV), F32),
                        pltpu.VMEM((GLA_NB, H * C, NS * H * DK), BF16),
                        pltpu.VMEM((GLA_NB, H * C + H * DK, H * DV), BF16)],
        compiler_params=_cparams("parallel", "arbitrary"),
        name="gla",
    )(q, k, v, la, r, s0, gout, lcat)


def _band_attn_kernel(q_ref, kp_ref, kc_ref, vp_ref, vc_ref, o_ref, lse_ref, k_scr, v_scr, *, dil, n_sub, qs):
    i = pl.program_id(2)
    k_scr[0:BAND, :] = kp_ref[...].astype(BF16)
    k_scr[BAND:, :] = kc_ref[...].astype(BF16)
    v_scr[0:BAND, :] = vp_ref[...].astype(BF16)
    v_scr[BAND:, :] = vc_ref[...].astype(BF16)
    nk = qs + BAND
    t = lax.broadcasted_iota(jnp.int32, (qs, nk), 0)
    c = lax.broadcasted_iota(jnp.int32, (qs, nk), 1)
    dist = BAND + t - c
    in_band = (dist >= 0) & (dist <= BAND)
    distf = (dist * dil).astype(F32)
    first = lax.broadcasted_iota(jnp.int32, (nk, LANES), 1) < SWA_HD
    first_q = lax.broadcasted_iota(jnp.int32, (qs, LANES), 1) < SWA_HD
    zero = jnp.zeros((nk, LANES), BF16)
    ones_bd = jnp.concatenate([first, jnp.logical_not(first)], axis=0).astype(BF16)

    def sub_block(j, carry):
        r0 = pl.multiple_of(j * qs, qs)
        rows = pl.ds(r0, qs)
        valid = in_band & ((c >= BAND) | (i > 0) | (j > 0))
        for pr in range(SWA_HEADS // 2):
            cols = slice(pr * LANES, (pr + 1) * LANES)
            q2 = (q_ref[rows, cols].astype(F32) * (SWA_HD ** -0.5)).astype(BF16)
            k2 = k_scr[pl.ds(r0, nk), cols]
            v2 = v_scr[pl.ds(r0, nk), cols]
            k_bd = jnp.concatenate([jnp.where(first, k2, zero), jnp.where(first, zero, k2)], axis=0)
            v_bd = jnp.concatenate([jnp.where(first, v2, zero), jnp.where(first, zero, v2)], axis=0)
            s2 = _dot_nt(q2, k_bd)
            ps, ms = [], []
            for u in range(2):
                s = s2[:, u * nk:(u + 1) * nk] - (2.0 ** -(2 * pr + u + 1)) * distf
                s = jnp.where(valid, s, NEG)
                m = jnp.max(s, axis=-1, keepdims=True)
                ps.append(jnp.exp(s - m).astype(BF16))
                ms.append(m)
            od = _dot(jnp.concatenate(ps, axis=1), jnp.concatenate([v_bd, ones_bd], axis=1))
            den = od[:, LANES:]
            o_ref[rows, cols] = od[:, :LANES] / den
            lse_ref[rows, cols] = jnp.where(first_q, ms[0], ms[1]) + jnp.log(den)
        return carry

    lax.fori_loop(0, n_sub, sub_block, 0)


def _band_attn(q, k, v, dil):
    B, Tc, dw = q.shape
    W = dw // dil
    qb = min(4 * BAND, Tc)
    qs = min(2 * BAND, Tc)
    n_sub = qb // qs
    cur = pl.BlockSpec((None, qb, W), lambda b, r, i: (b, i, r))
    prev = pl.BlockSpec((None, BAND, W), lambda b, r, i: (b, jnp.maximum(i * (qb // BAND) - 1, 0), r))
    o, lse = pl.pallas_call(
        functools.partial(_band_attn_kernel, dil=dil, n_sub=n_sub, qs=qs),
        grid=(B, dil, Tc // qb),
        in_specs=[cur, prev, cur, prev, cur],
        out_specs=[cur, cur],
        out_shape=[jax.ShapeDtypeStruct((B, Tc, dw), F32)] * 2,
        scratch_shapes=[pltpu.VMEM((BAND + qb, W), BF16)] * 2,
        compiler_params=_cparams("parallel", "parallel", "parallel"),
        name="band_attn_d%d" % dil,
    )(q, k, k, v, v)
    return o.reshape(B * Tc, dw), lse.reshape(B * Tc, dw)


def _pattern_count(dist):
    cnt = jnp.zeros(dist.shape, F32)
    for window, dil in DILATED_PATTERNS:
        hit = (dist >= 0) & (dist <= window)
        if dil > 1:
            hit = hit & ((dist & (dil - 1)) == 0)
        cnt = cnt + hit.astype(F32)
    return cnt


def _samp_attn_kernel(q_ref, kn_ref, vn_ref, kp_ref, vp_ref, o_ref, ko_ref, vo_ref):
    P, Tn = kp_ref.shape[2], kn_ref.shape[2]
    qi = lax.broadcasted_iota(jnp.int32, (Tn, P), 0)
    dist_p = P + qi - lax.broadcasted_iota(jnp.int32, (Tn, P), 1)
    dist_n = lax.broadcasted_iota(jnp.int32, (Tn, Tn), 0) - lax.broadcasted_iota(jnp.int32, (Tn, Tn), 1)
    cnt_p, cnt_n = _pattern_count(dist_p), _pattern_count(dist_n)
    dpf, dnf = dist_p.astype(F32), dist_n.astype(F32)
    q = (q_ref[...] * (SWA_HD ** -0.5)).astype(BF16)
    outs = []
    for h in range(SWA_HEADS):
        kp, vp, kn, vn = kp_ref[h], vp_ref[h], kn_ref[h], vn_ref[h]
        ko_ref[h] = pltpu.roll(kp, P - Tn, axis=1)
        ko_ref[h, :, P - Tn:] = kn
        vo_ref[h] = pltpu.roll(vp, P - Tn, axis=1)
        vo_ref[h, :, P - Tn:] = vn
        qh = q[:, h * SWA_HD:(h + 1) * SWA_HD]
        slope = 2.0 ** -(h + 1)
        lp = jnp.where(cnt_p > 0, _dot(qh, kp.astype(BF16)) - slope * dpf, NEG)
        ln = jnp.where(cnt_n > 0, _dot(qh, kn.astype(BF16)) - slope * dnf, NEG)
        m = jnp.maximum(jnp.max(lp, axis=-1, keepdims=True), jnp.max(ln, axis=-1, keepdims=True))
        pp = cnt_p * jnp.exp(lp - m)
        pn = cnt_n * jnp.exp(ln - m)
        den = jnp.sum(pp, axis=-1, keepdims=True) + jnp.sum(pn, axis=-1, keepdims=True)
        outs.append((_dot_nt(pp.astype(BF16), vp.astype(BF16)) + _dot_nt(pn.astype(BF16), vn.astype(BF16))) / den)
    o_ref[...] = jnp.concatenate(outs, axis=1)


def _samp_attn(q, knT, vnT, kpT, vpT):
    B, Tn, W = q.shape
    P = kpT.shape[-1]
    qs = pl.BlockSpec((None, Tn, W), lambda b: (b, 0, 0))
    new = pl.BlockSpec((None, SWA_HEADS, SWA_HD, Tn), lambda b: (b, 0, 0, 0))
    past = pl.BlockSpec((None, SWA_HEADS, SWA_HD, P), lambda b: (b, 0, 0, 0))
    return pl.pallas_call(
        _samp_attn_kernel,
        grid=(B,),
        in_specs=[qs, new, new, past, past],
        out_specs=[qs, past, past],
        out_shape=[jax.ShapeDtypeStruct((B, Tn, W), F32), jax.ShapeDtypeStruct(kpT.shape, F32),
                   jax.ShapeDtypeStruct(kpT.shape, F32)],
        compiler_params=_cparams("parallel"),
        name="samp_attn",
    )(q, knT, vnT, kpT, vpT)


def _mem_kv_kernel(mem_ref, gmem_ref, wk_ref, wv_ref, gxk_ref, mk_ref, mv_ref):
    hm = _rms_rows(mem_ref[...], gmem_ref[...]).astype(BF16)
    mk = _dot(hm, wk_ref[...])
    gxk = gxk_ref[...]
    mv = _dot(hm, wv_ref[...])
    n = mem_ref.shape[0]
    for h in range(X_HEADS):
        sl = slice(h * X_HD, (h + 1) * X_HD)
        mk_ref[pl.ds(h, n, stride=X_HEADS), :] = _rms_rows(mk[:, sl], gxk)
        mv_ref[pl.ds(h, n, stride=X_HEADS), :] = mv[:, sl]


def _mem_kv(mem, gmem, wk, wv, gxk):
    n = mem.shape[0]
    return pl.pallas_call(
        _mem_kv_kernel,
        out_shape=[jax.ShapeDtypeStruct((n * X_HEADS, X_HD), F32)] * 2,
        compiler_params=pltpu.CompilerParams(vmem_limit_bytes=VMEM_LIMIT),
        name="mem_kv",
    )(mem, gmem, wk, wv, gxk)


def _post_kernel(*refs, n_pat, n_seg):
    n_swa = 2 * n_pat if n_pat > 1 else 1
    x_ref, og_ref = refs[0], refs[1]
    swa_refs = refs[2:2 + n_swa]
    (wout_ref, gx_ref, wxq_ref, gxq_ref, mk_ref, mv_ref, wxo_ref, gffn_ref, wr_ref, br_ref, tri_ref,
     cnt0_ref) = refs[2 + n_swa:14 + n_swa]
    n_scr = 3 if n_pat > 1 else 2
    x2_ref, h3_ref, ti_ref, tg_ref, cnt_ref = refs[len(refs) - n_scr - 5:len(refs) - n_scr]
    ox_scr, run_scr = refs[len(refs) - n_scr:len(refs) - n_scr + 2]

    @pl.when(pl.program_id(0) == 0)
    def _():
        run_scr[...] = cnt0_ref[...]

    if n_pat > 1:
        dei_scr = refs[-1]
        tm = x_ref.shape[0]
        vals = []
        for a, r in enumerate(swa_refs):
            d = DILATED_PATTERNS[a % n_pat][1]
            if d == 1:
                vals.append(r[...])
            else:
                slot, n_col = len(vals), SWA_W // LANES
                for c in range(d):
                    for g in range(n_col):
                        lo = c * SWA_W + g * LANES
                        dei_scr[slot, g, pl.ds(c, tm // d, stride=d), :] = r[:, lo:lo + LANES]
                vals.append(jnp.concatenate([dei_scr[slot, g] for g in range(n_col)], axis=1))
        o_p, l_p = vals[:n_pat], vals[n_pat:]
        lmax = functools.reduce(jnp.maximum, l_p)
        w_p = [jnp.exp(l - lmax) for l in l_p]
        o_swa = sum(w * o for w, o in zip(w_p, o_p)) / sum(w_p)
    else:
        o_swa = swa_refs[0][...]

    x1 = x_ref[...] + _dot(og_ref[...], wout_ref[0:GLA_V_W, :]) + _dot(o_swa.astype(BF16), wout_ref[GLA_V_W:, :])

    q = _dot(_rms_rows(x1, gx_ref[...]).astype(BF16), wxq_ref[...])
    gxq = gxq_ref[...]
    rows_t, seg = x1.shape[0], x1.shape[0] // n_seg
    if n_seg > 1:
        own = (lax.broadcasted_iota(jnp.int32, (rows_t, n_seg * MEM_LEN), 0) // seg
               == lax.broadcasted_iota(jnp.int32, (rows_t, n_seg * MEM_LEN), 1) // MEM_LEN)
    for h in range(X_HEADS):
        sl = slice(h * X_HD, (h + 1) * X_HD)
        qn = _rms_rows(q[:, sl], gxq).astype(BF16)
        mem_rows = pl.ds(h, MEM_LEN, stride=X_HEADS)
        mk_h = jnp.concatenate([mk_ref[s, mem_rows, :] for s in range(n_seg)], axis=0).astype(BF16)
        mv_h = jnp.concatenate([mv_ref[s, mem_rows, :] for s in range(n_seg)], axis=0).astype(BF16)
        sc = _dot_nt(qn, mk_h) * (X_HD ** -0.5)
        if n_seg > 1:
            sc = jnp.where(own, sc, NEG)
        p = jnp.exp(sc - jnp.max(sc, axis=-1, keepdims=True))
        p = p / jnp.sum(p, axis=-1, keepdims=True)
        ox_scr[:, sl] = _dot(p.astype(BF16), mv_h)
    x2 = x1 + _dot(ox_scr[...].astype(BF16), wxo_ref[...])
    x2_ref[...] = x2

    h3 = _rms_rows(x2, gffn_ref[...]).astype(BF16)
    h3_ref[...] = h3
    work = _dot(h3, wr_ref[...]) + br_ref[...]
    lane = lax.broadcasted_iota(jnp.int32, work.shape, 1)
    vals, idxs = [], []
    for _ in range(TOP_K):
        m = jnp.max(work, axis=-1, keepdims=True)
        idx = jnp.min(jnp.where(work == m, lane, LANES), axis=-1, keepdims=True)
        vals.append(m)
        idxs.append(idx)
        work = jnp.where(lane == idx, -jnp.inf, work)
    es = [jnp.exp(v - vals[0]) for v in vals]
    den = sum(es)
    sel = [lane == idx for idx in idxs]
    onehot = functools.reduce(jnp.logical_or, sel).astype(BF16)
    run = run_scr[...]
    rank = _dot(tri_ref[...], onehot) + run
    run_scr[...] = run + jnp.sum(onehot.astype(F32), axis=0, keepdims=True)
    cnt_ref[...] = run_scr[...]
    ti = jnp.zeros(work.shape, jnp.int32)
    tg = jnp.zeros(work.shape, F32)
    for j in range(TOP_K):
        pos = jnp.sum(jnp.where(sel[j], rank, 0.0), axis=-1, keepdims=True).astype(jnp.int32)
        ti = jnp.where(lane == j, idxs[j], ti)
        ti = jnp.where(lane == TOP_K + j, pos, ti)
        tg = jnp.where(lane == j, es[j] / den, tg)
    ti_ref[...] = ti
    tg_ref[...] = tg


def _post(x, og, swa, wout, gx, wxq, gxq, mk, mv, wxo, gffn, wr, br, cnt0, tm, n_seg, rows_per_mem, h3_rows, h3_row0,
          h3_prior):
    n = x.shape[0]
    n_pat = len(swa) // 2 if len(swa) > 1 else 1
    tri = jnp.asarray(np.tril(np.ones((tm, tm), np.float32), -1), BF16)
    row = lambda w: pl.BlockSpec((tm, w), lambda i: (i, 0))
    full = lambda a: pl.BlockSpec(a.shape, lambda i: (0,) * a.ndim)
    mem = pl.BlockSpec((n_seg, MEM_LEN * X_HEADS, X_HD), lambda i: ((i * tm) // (rows_per_mem * n_seg), 0, 0))
    swa_specs = [pl.BlockSpec((tm * SWA_W // a.shape[1], a.shape[1]), lambda i: (i, 0)) for a in swa]
    scratch = [pltpu.VMEM((tm, X_W), F32), pltpu.VMEM((1, LANES), F32)]
    if n_pat > 1:
        scratch.append(pltpu.VMEM((len(swa), SWA_W // LANES, tm, LANES), F32))
    return pl.pallas_call(
        functools.partial(_post_kernel, n_pat=n_pat, n_seg=n_seg),
        grid=(n // tm,),
        in_specs=[row(D_MODEL), row(GLA_V_W)] + swa_specs
        + [full(wout), full(gx), full(wxq), full(gxq), mem, mem, full(wxo), full(gffn), full(wr), full(br),
           full(tri), full(cnt0)] + [pl.BlockSpec(memory_space=pl.ANY)] * len(h3_prior),
        out_specs=[row(D_MODEL), pl.BlockSpec((tm, D_MODEL), lambda i: (i + h3_row0 // tm, 0)), row(LANES), row(LANES),
                   full(cnt0)],
        out_shape=[jax.ShapeDtypeStruct((n, D_MODEL), F32), jax.ShapeDtypeStruct((h3_rows, D_MODEL), BF16),
                   jax.ShapeDtypeStruct((n, LANES), jnp.int32), jax.ShapeDtypeStruct((n, LANES), F32),
                   jax.ShapeDtypeStruct((1, LANES), F32)],
        scratch_shapes=scratch,
        compiler_params=_cparams("arbitrary"),
        name="post",
        input_output_aliases={14 + len(swa): 1} if h3_prior else {},
    )(x, og, *swa, wout, gx, wxq, gxq, mk, mv, wxo, gffn, wr, br, tri, cnt0, *h3_prior)


def _moe_kernel(te_ref, nu_ref, x_ref, wg_ref, bg_ref, wu_ref, bu_ref, wd_ref, bd_ref, *rest):
    y_ref, wg_scr, wu_scr, wd_scr = rest[-4:]
    i = pl.program_id(0)
    live = i < nu_ref[0]

    @pl.when(live & ((i == 0) | (te_ref[i] != te_ref[jnp.maximum(i - 1, 0)])))
    def _():
        wg_scr[...] = wg_ref[...].astype(BF16)
        wu_scr[...] = wu_ref[...].astype(BF16)
        wd_scr[...] = wd_ref[...].astype(BF16)

    @pl.when(live)
    def _():
        x = x_ref[...]
        g = jnp.minimum(_dot(x, wg_scr[...]) + bg_ref[...], SWIGLU_LIMIT)
        u = jnp.clip(_dot(x, wu_scr[...]) + bu_ref[...], -SWIGLU_LIMIT, SWIGLU_LIMIT)
        a = g * _sigmoid(SWIGLU_ALPHA * g) * (u + 1.0)
        y_ref[...] = (_dot(a.astype(BF16), wd_scr[...]) + bd_ref[...]).astype(y_ref.dtype)

    @pl.when(jnp.logical_not(live))
    def _():
        y_ref[...] = jnp.zeros(y_ref.shape, y_ref.dtype)


MOE_CHUNK_ENDS = (0.05, 0.25, 1.0)


def _moe(tile_expert, n_used, h3, src_tok, wg, bg, wu, bu, wd, bd):
    P = src_tok.shape[0]
    tm = MOE_TM
    n_tiles = P // tm
    ys = None
    ends = [max(1, round(n_tiles * f)) for f in MOE_CHUNK_ENDS]
    for lo, hi in zip([0] + ends[:-1], ends):
        w_spec = pl.BlockSpec((None, D_MODEL, D_MODEL), lambda i, te, *_: (te[i], 0, 0))
        b_spec = pl.BlockSpec((None, 1, D_MODEL), lambda i, te, *_: (te[i], 0, 0))
        prior = [] if ys is None else [ys]
        ys = pl.pallas_call(
            _moe_kernel,
            grid_spec=pltpu.PrefetchScalarGridSpec(
                num_scalar_prefetch=2,
                grid=(hi - lo,),
                in_specs=[pl.BlockSpec((tm, D_MODEL), lambda i, *_: (i, 0)),
                          w_spec, b_spec, w_spec, b_spec, w_spec, b_spec]
                + [pl.BlockSpec(memory_space=pl.ANY)] * len(prior),
                out_specs=pl.BlockSpec((tm, D_MODEL), lambda i, *_, lo=lo: (i + lo, 0)),
                scratch_shapes=[pltpu.VMEM((D_MODEL, D_MODEL), BF16)] * 3,
            ),
            out_shape=jax.ShapeDtypeStruct((P, D_MODEL), BF16),
            input_output_aliases={9: 0} if prior else {},
            compiler_params=_cparams("arbitrary"),
            name="moe",
        )(tile_expert[lo:hi], n_used - lo, _sc_take(h3, src_tok[lo * tm:hi * tm]), wg, bg, wu, bu, wd, bd, *prior)
    return ys


def _route(top_i, rank, counts, tm):
    n = top_i.shape[0]
    a = n * TOP_K
    n_tiles = a // tm + N_EXPERTS
    tiles_e = (counts + tm - 1) // tm
    tile_end = jnp.cumsum(tiles_e)
    slot0 = (tile_end - tiles_e) * tm
    experts = jnp.arange(N_EXPERTS, dtype=jnp.int32)
    dest = rank + jnp.sum(jnp.where(top_i[..., None] == experts, slot0, 0), axis=-1)
    n_used = tile_end[-1:].astype(jnp.int32)
    tile_expert = jnp.minimum(
        jnp.sum((tile_end[None, :] <= jnp.arange(n_tiles, dtype=jnp.int32)[:, None]).astype(jnp.int32), axis=1),
        N_EXPERTS - 1)
    pad_tok = -1 - (jnp.arange(n_tiles * tm, dtype=jnp.int32) % n)
    src_tok = pad_tok.at[dest.reshape(a)].max(jnp.arange(a, dtype=jnp.int32) // TOP_K, unique_indices=True)
    src_tok = jnp.where(src_tok < 0, -1 - src_tok, src_tok)
    return tile_expert, n_used, src_tok, dest


def _combine_kernel(x_ref, y_ref, g_ref, *rest):
    o_ref = rest[-1]
    acc = x_ref[...]
    g = g_ref[...]
    for j in range(TOP_K):
        acc = acc + g[:, j:j + 1] * y_ref[j].astype(F32)
    o_ref[...] = acc


COMBINE_CHUNKS = 8


def _combine(x2, ysg, tg, row0, tm, prior):
    n = ysg.shape[1]
    off = row0 // tm
    return pl.pallas_call(
        _combine_kernel,
        grid=(n // tm,),
        in_specs=[pl.BlockSpec((tm, D_MODEL), lambda i: (i + off, 0)),
                  pl.BlockSpec((TOP_K, tm, D_MODEL), lambda i: (0, i, 0)),
                  pl.BlockSpec((tm, LANES), lambda i: (i + off, 0))] + [pl.BlockSpec(memory_space=pl.ANY)] * len(prior),
        out_specs=pl.BlockSpec((tm, D_MODEL), lambda i: (i + off, 0)),
        out_shape=jax.ShapeDtypeStruct(x2.shape, F32),
        input_output_aliases={3: 0} if prior else {},
        compiler_params=_cparams("parallel"),
        name="combine",
    )(x2, ysg, tg, *prior)


@compute_on("tpu_sparsecore")
@jax.jit
def _sc_take(x, idx):
    return jnp.take(x, idx, axis=0, mode="clip")


def _mixer_inputs(x2d, w, tm, classes, seq_len=0, win_len=0):
    return _in_proj(x2d, w["gmix"], w["w_in_p"], w["wa2_p"], w["ba"], w["gsq"], w["gsk"], w["bd"], tm, classes,
                    seq_len, win_len)


def kernel(x_prompt, x_sample, mem_prompt, state_gla, cache_swa_k, cache_swa_v, cache_mem_k, cache_mem_v, g_mix, w_in, w_gla_a2, b_gla_a, g_gla_out, g_swa_q, g_swa_k, w_out, g_mem, w_mk, w_mv, g_xk, g_xattn, w_xq, g_xq, w_xo, g_ffn, w_router, b_router, w_gate, b_gate, w_up, b_up, w_down, b_down):
    B, T, D = x_prompt.shape
    Bs, Ts, _ = x_sample.shape
    P = cache_swa_k.shape[2]
    l = 0

    wi = w_in[l]
    segs = np.cumsum((0, GLA_QK_W, GLA_QK_W, GLA_V_W, GLA_V_W, GLA_RANK, SWA_W, SWA_W, SWA_W))
    gq_c, gk_c, gv_c, gr_c, ga_c, sq_c, sk_c, sv_c = [wi[:, segs[j]:segs[j + 1]] for j in range(8)]
    w_in_p = jnp.concatenate(
        [gq_c, gk_c, gv_c, gr_c, sq_c, sk_c, sv_c, ga_c, jnp.zeros((D, D_IN_PAD - _C_GA - GLA_RANK), F32)],
        axis=1).astype(BF16)
    heads = np.arange(SWA_W) // SWA_HD
    w = dict(
        gmix=g_mix[l][None], w_in_p=w_in_p,
        wa2_p=jnp.concatenate([w_gla_a2[l], jnp.zeros((LANES - GLA_RANK, GLA_QK_W), F32)], axis=0).astype(BF16),
        ba=b_gla_a[l][None],
        gsq=jnp.tile(g_swa_q[l], SWA_HEADS)[None], gsk=jnp.tile(g_swa_k[l], SWA_HEADS)[None],
        bd=jnp.asarray(heads[:, None] == heads[None, :], BF16),
    )
    gout = g_gla_out[l][None]
    wout = w_out[l].astype(BF16)
    wxq, wxo = w_xq[l].astype(BF16), w_xo[l].astype(BF16)
    wr = jnp.concatenate([w_router[l], jnp.zeros((D, LANES - N_EXPERTS), F32)], axis=1).astype(BF16)
    br = jnp.concatenate([b_router[l], jnp.full((LANES - N_EXPERTS,), NEG, F32)])[None]
    post_w = (wout, g_xattn[l][None], wxq, g_xq[l][None])
    post_w2 = (wxo, g_ffn[l][None], wr, br)

    w_buf = min(SWA_WINDOW, T)
    gq, gk, gv, gr, la, sq, sk, sv, *cls, kT_p, vT_p = _mixer_inputs(x_prompt.reshape(B * T, D), w, IN_PROJ_TM, True,
                                                                     T, w_buf)
    r3 = lambda a: a.reshape(B, T, a.shape[-1])
    og_p, st_p = _gla(r3(gq), r3(gk), r3(gv), r3(la), r3(gr),
                      jnp.zeros((B, GLA_QK_W, GLA_DV), F32), gout, GLA_TB)
    o_pat, l_pat = [], []
    for _, dil in DILATED_PATTERNS:
        if dil == 1:
            qkv = [r3(sq), r3(sk), r3(sv)]
        else:
            c = CLASS_DILS.index(dil)
            qkv = [cls[j * len(CLASS_DILS) + c].reshape(B, T // dil, dil * SWA_W) for j in range(3)]
        o_c, l_c = _band_attn(*qkv, dil)
        o_pat.append(o_c)
        l_pat.append(l_c)
    mk_p, mv_p = _mem_kv(mem_prompt.reshape(B * MEM_LEN, D), g_mem[l][None], w_mk[l].astype(BF16),
                         w_mv[l].astype(BF16), g_xk[l][None])
    n_p, n_all = B * T, B * T + Bs * Ts
    x2_p, h3, ti_p, tg_p, cnt_p = _post(x_prompt.reshape(B * T, D), og_p.reshape(B * T, GLA_V_W), o_pat + l_pat,
                                        *post_w, mk_p.reshape(B, MEM_LEN * X_HEADS, X_HD),
                                        mv_p.reshape(B, MEM_LEN * X_HEADS, X_HD),
                                        *post_w2, jnp.zeros((1, LANES), F32), tm=POST_TM, n_seg=1, rows_per_mem=T,
                                        h3_rows=n_all, h3_row0=0, h3_prior=[])

    gq, gk, gv, gr, la, sq, sk, sv = _mixer_inputs(x_sample.reshape(Bs * Ts, D), w, Bs * Ts, False)
    pad = lambda a: jnp.pad(a.reshape(Bs, Ts, a.shape[-1]), ((0, 0), (0, GLA_CHUNK - Ts), (0, 0)))
    og_s, st_s = _gla(pad(gq), pad(gk), pad(gv), pad(la), pad(gr),
                      state_gla[l].reshape(Bs, GLA_QK_W, GLA_DV), gout, GLA_CHUNK)
    og_s = og_s[:, :Ts].reshape(Bs * Ts, GLA_V_W)
    rows_minor = lambda a: jnp.transpose(a.reshape(Bs, -1, SWA_HEADS, SWA_HD), (0, 2, 3, 1))
    o_swa_s, kb_s, vb_s = _samp_attn(sq.reshape(Bs, Ts, SWA_W), rows_minor(sk), rows_minor(sv),
                                     rows_minor(cache_swa_k[l]), rows_minor(cache_swa_v[l]))
    kb_s, vb_s = jnp.transpose(kb_s, (0, 3, 1, 2)), jnp.transpose(vb_s, (0, 3, 1, 2))
    x2_s, h3, ti_s, tg_s, cnt = _post(x_sample.reshape(Bs * Ts, D), og_s, [o_swa_s.reshape(Bs * Ts, SWA_W)],
                                      *post_w, cache_mem_k[l].reshape(Bs, MEM_LEN * X_HEADS, X_HD),
                                      cache_mem_v[l].reshape(Bs, MEM_LEN * X_HEADS, X_HD),
                                      *post_w2, cnt_p, tm=SAMPLE_SEQS * Ts, n_seg=SAMPLE_SEQS, rows_per_mem=Ts,
                                      h3_rows=n_all, h3_row0=n_p, h3_prior=[h3])

    ti = jnp.concatenate([ti_p[:, :2 * TOP_K], ti_s[:, :2 * TOP_K]], axis=0)
    tile_expert, n_used, src_tok, dest = _route(ti[:, :TOP_K], ti[:, TOP_K:], cnt[0, :N_EXPERTS].astype(jnp.int32),
                                                MOE_TM)
    ys = _moe(tile_expert, n_used, h3, src_tok, w_gate[l], b_gate[l][:, None, :], w_up[l],
              b_up[l][:, None, :], w_down[l], b_down[l][:, None, :])
    def combined(x2, tg, row0, first, n, tm, prior):
        ysg = _sc_take(ys, dest[first:first + n].T.reshape(-1)).reshape(TOP_K, n, D)
        return [_combine(x2, ysg, tg, row0, tm, prior)]

    y_p, n_c = [], n_p // COMBINE_CHUNKS
    for c in range(COMBINE_CHUNKS):
        y_p = combined(x2_p, tg_p, c * n_c, c * n_c, n_c, COMBINE_TM, y_p)
    y_p, y_s = y_p[0], combined(x2_s, tg_s, 0, n_p, Bs * Ts, Bs * Ts, [])[0]

    kv_p = lambda a: jnp.transpose(a.reshape(B, SWA_HEADS, SWA_HD, w_buf), (0, 3, 1, 2))[None]
    return (y_p.reshape(B, T, D), y_s.reshape(Bs, Ts, D),
            st_p.reshape(1, B, GLA_HEADS, GLA_DK, GLA_DV), kv_p(kT_p), kv_p(vT_p),
            mk_p.reshape(1, B, MEM_LEN, X_HEADS, X_HD), mv_p.reshape(1, B, MEM_LEN, X_HEADS, X_HD),
            st_s.reshape(1, Bs, GLA_HEADS, GLA_DK, GLA_DV),
            kb_s.reshape(1, Bs, P, SWA_HEADS, SWA_HD), vb_s.reshape(1, Bs, P, SWA_HEADS, SWA_HD))
```

```python
import functools

import numpy as np
import jax
import jax.numpy as jnp
from jax import lax
from jax.experimental import pallas as pl
from jax.experimental.compute_on import compute_on
from jax.experimental.pallas import tpu as pltpu

F32 = jnp.float32
BF16 = jnp.bfloat16

D_MODEL = 1024
GLA_HEADS = 4
GLA_DK = 64
GLA_DV = 128
GLA_RANK = 16
GLA_TAU = 16.0
GLA_CHUNK = 128
GLA_SUB = 32
SWA_HEADS = 8
SWA_HD = 64
DILATED_PATTERNS = ((128, 1), (512, 4), (2048, 16))
BAND = 128
SWA_WINDOW = 2048
MEM_LEN = 256
X_HEADS = 4
X_HD = 128
N_EXPERTS = 32
TOP_K = 4
SWIGLU_ALPHA = 1.702
SWIGLU_LIMIT = 7.0
EPS = 1e-6

GLA_QK_W = GLA_HEADS * GLA_DK
GLA_V_W = GLA_HEADS * GLA_DV
SWA_W = SWA_HEADS * SWA_HD
X_W = X_HEADS * X_HD
LANES = 128
NEG = -1e30
VMEM_LIMIT = 56 * 1024 * 1024

_C_GQ, _C_GK, _C_GV, _C_GR, _C_SQ, _C_SK, _C_SV, _C_GA = 0, 256, 512, 1024, 1536, 2048, 2560, 3072
D_IN_PAD = 3200

IN_PROJ_TM = 512
GLA_TB = 512
POST_TM = 512
COMBINE_TM = 256
MOE_TM = 512
SAMPLE_SEQS = 8


def _cparams(*sem):
    return pltpu.CompilerParams(dimension_semantics=sem, vmem_limit_bytes=VMEM_LIMIT)


def _dot(a, b):
    return jnp.dot(a, b, preferred_element_type=F32)


def _dot_nt(a, b):
    return lax.dot_general(a, b, (((1,), (1,)), ((), ())), preferred_element_type=F32)


def _rms_rows(x, g):
    return x * lax.rsqrt(jnp.mean(x * x, axis=-1, keepdims=True) + EPS) * g


def _split_bf16(x, n):
    out = []
    for _ in range(n - 1):
        hi = x.astype(BF16)
        out.append(hi)
        x = x - hi.astype(F32)
    out.append(x.astype(BF16))
    return out


def _group_rms(z, g, ones_bd, group):
    hi, lo = _split_bf16(z * z, 2)
    ss = _dot(hi, ones_bd) + _dot(lo, ones_bd)
    return z * lax.rsqrt(ss * (1.0 / group) + EPS) * g


def _log_sigmoid(x):
    return jnp.minimum(x, 0.0) - jnp.log1p(jnp.exp(-jnp.abs(x)))


def _sigmoid(x):
    return 1.0 / (1.0 + jnp.exp(-x))


CLASS_DILS = tuple(d for _, d in DILATED_PATTERNS if d > 1)


def _in_proj_kernel(x_ref, gmix_ref, w_ref, wa2_ref, ba_ref, gsq_ref, gsk_ref, bd_ref,
                    oq_ref, ok_ref, ov_ref, or_ref, ola_ref, osq_ref, osk_ref, osv_ref, *rest, classes, seq_tiles,
                    win_tiles):
    h = _rms_rows(x_ref[...], gmix_ref[...]).astype(BF16)

    def proj(lo, width):
        return _dot(h, w_ref[:, lo:lo + width])

    oq_ref[...] = proj(_C_GQ, GLA_QK_W) * (GLA_DK ** -0.5)
    ok_ref[...] = proj(_C_GK, GLA_QK_W)
    ov_ref[...] = proj(_C_GV, GLA_V_W)
    or_ref[...] = proj(_C_GR, GLA_V_W)
    osv_ref[...] = proj(_C_SV, SWA_W)
    ga = proj(_C_GA, LANES)
    xa = _dot(ga.astype(BF16), wa2_ref[...]) + ba_ref[...]
    ola_ref[...] = _log_sigmoid(xa) * (1.0 / GLA_TAU)
    bd = bd_ref[...]
    osq_ref[...] = _group_rms(proj(_C_SQ, SWA_W), gsq_ref[...], bd, SWA_HD)
    osk_ref[...] = _group_rms(proj(_C_SK, SWA_W), gsk_ref[...], bd, SWA_HD)

    if classes:
        kT_ref, vT_ref = rest[-3], rest[-2]
        rest = rest[:-3] + rest[-1:]

        @pl.when(pl.program_id(0) % seq_tiles >= seq_tiles - win_tiles)
        def _():
            kT_ref[...] = osk_ref[...].T
            vT_ref[...] = osv_ref[...].T

        cls_refs, col_scr = rest[:-1], rest[-1]
        tm = x_ref.shape[0]
        n_col = SWA_W // LANES
        for j, src in enumerate((osq_ref, osk_ref, osv_ref)):
            for g in range(n_col):
                col_scr[g] = src[:, g * LANES:(g + 1) * LANES]
            for c, d in enumerate(CLASS_DILS):
                dst = cls_refs[j * len(CLASS_DILS) + c]
                for r in range(d):
                    for g in range(n_col):
                        lo = r * SWA_W + g * LANES
                        dst[:, lo:lo + LANES] = col_scr[g, pl.ds(r, tm // d, stride=d), :].astype(BF16)


def _in_proj(x, gmix, w_in_p, wa2_p, ba, gsq, gsk, bd, tm, classes, seq_len=0, win_len=0):
    n = x.shape[0]
    row = lambda w: pl.BlockSpec((tm, w), lambda i: (i, 0))
    full = lambda a: pl.BlockSpec(a.shape, lambda i: (0,) * a.ndim)
    widths = (GLA_QK_W, GLA_QK_W, GLA_V_W, GLA_V_W, GLA_QK_W, SWA_W, SWA_W, SWA_W)
    out_specs = [row(w) for w in widths]
    out_shape = [jax.ShapeDtypeStruct((n, w), F32) for w in widths]
    seq_tiles, win_tiles = seq_len // tm, win_len // tm
    if classes:
        for _ in range(3):
            for d in CLASS_DILS:
                out_specs.append(pl.BlockSpec((tm // d, d * SWA_W), lambda i: (i, 0)))
                out_shape.append(jax.ShapeDtypeStruct((n // d, d * SWA_W), BF16))
        win = pl.BlockSpec((None, SWA_W, tm), lambda i: (i // seq_tiles, 0,
                                                         jnp.maximum(i % seq_tiles - (seq_tiles - win_tiles), 0)))
        out_specs += [win, win]
        out_shape += [jax.ShapeDtypeStruct((n // seq_len, SWA_W, win_len), F32)] * 2
    return pl.pallas_call(
        functools.partial(_in_proj_kernel, classes=classes, seq_tiles=seq_tiles, win_tiles=win_tiles),
        grid=(n // tm,),
        in_specs=[row(D_MODEL), full(gmix), full(w_in_p), full(wa2_p), full(ba), full(gsq), full(gsk), full(bd)],
        out_specs=out_specs,
        out_shape=out_shape,
        scratch_shapes=[pltpu.VMEM((SWA_W // LANES, tm, LANES), F32)] if classes else [],
        compiler_params=_cparams("arbitrary"),
        name="in_proj",
    )(x, gmix, w_in_p, wa2_p, ba, gsq, gsk, bd)


GLA_NB = 2


def _gla_kernel(q_ref, k_ref, v_ref, la_ref, r_ref, s0_ref, gout_ref, lcat_ref,
                o_ref, s_ref, s_scr, knt_scr, rhs_scr, *, n_chunks):
    tb = pl.program_id(1)
    C, S, NS, H, DK, DV = GLA_CHUNK, GLA_SUB, GLA_CHUNK // GLA_SUB, GLA_HEADS, GLA_DK, GLA_DV

    @pl.when(tb == 0)
    def _():
        s_scr[...] = s0_ref[...]
        knt_scr[...] = jnp.zeros(knt_scr.shape, BF16)
        rhs_scr[...] = jnp.zeros(rhs_scr.shape, BF16)

    lcat = lcat_ref[...]
    gout = gout_ref[...]
    row = lax.broadcasted_iota(jnp.int32, (C, H * DK), 0)
    tril = (lax.broadcasted_iota(jnp.int32, (C, H * C), 1) & (C - 1)) <= lax.broadcasted_iota(jnp.int32, (C, H * C), 0)
    ones_k = jnp.ones((3 * C, LANES), BF16)

    def chunk(c, carry):
        r0 = pl.multiple_of(c * C, C)
        rows = pl.ds(r0, C)
        for n in range(GLA_NB):
            g = la_ref[n, rows, :]
            b_r = _dot(lcat, jnp.concatenate(_split_bf16(g, 3), axis=1))
            b_r = b_r[:, :H * DK] + b_r[:, H * DK:2 * H * DK] + b_r[:, 2 * H * DK:]
            b = b_r[:C]
            ref = b_r[C:]
            b_end = b[C - 1:C, :]
            b_col = _dot(jnp.concatenate(_split_bf16(g.T, 3), axis=1), ones_k)
            q = q_ref[n, rows, :]
            k = k_ref[n, rows, :]
            v = v_ref[n, rows, :]
            q_hat = q * jnp.exp(b - ref)
            q_til = q * jnp.exp(b)
            k_dec = k * jnp.exp(b_end - b)
            for i in range(NS):
                e = ref[S * i:S * i + 1, :] - b
                piece = (k * jnp.exp(jnp.where(row < S * (i + 1), e, NEG))).astype(BF16)
                for h in range(H):
                    lo = i * H * DK + h * DK
                    knt_scr[n, h * C:(h + 1) * C, lo:lo + DK] = piece[:, h * DK:(h + 1) * DK]
            q_cat = jnp.concatenate([jnp.where((row >= S * i) & (row < S * (i + 1)), q_hat, 0.0) for i in range(NS)],
                                    axis=1).astype(BF16)
            sc = jnp.where(tril, _dot_nt(q_cat, knt_scr[n]), 0.0)
            st = s_scr[n]
            for h in range(H):
                vl = slice(h * DV, (h + 1) * DV)
                rhs_scr[n, h * C:(h + 1) * C, vl] = v[:, vl].astype(BF16)
                rhs_scr[n, H * C + h * DK:H * C + (h + 1) * DK, vl] = st[h * DK:(h + 1) * DK, :].astype(BF16)
            o = _dot(jnp.concatenate([sc, q_til], axis=1).astype(BF16), rhs_scr[n])
            kv = _dot(k_dec.T.astype(BF16), v.astype(BF16))
            s_scr[n] = st * jnp.exp(b_col) + jnp.concatenate(
                [kv[h * DK:(h + 1) * DK, h * DV:(h + 1) * DV] for h in range(H)], axis=0)
            gate = r_ref[n, rows, :]
            for h in range(H):
                vl = slice(h * DV, (h + 1) * DV)
                o_ref[n, rows, vl] = (_rms_rows(o[:, vl], gout) * (gate[:, vl] * _sigmoid(gate[:, vl]))).astype(o_ref.dtype)
        return carry

    lax.fori_loop(0, n_chunks, chunk, 0)

    @pl.when(tb == pl.num_programs(1) - 1)
    def _():
        s_ref[...] = s_scr[...]


def _gla_consts():
    C, S = GLA_CHUNK, GLA_SUB
    t = np.arange(C)
    incl = (t[None, :] <= t[:, None]).astype(np.float32)
    upto = (t[None, :] < (t[:, None] // S) * S).astype(np.float32)
    return jnp.asarray(np.concatenate([incl, upto], axis=0), BF16)


def _gla(q, k, v, la, r, s0, gout, tb):
    B, T, _ = q.shape
    H, C, DK, DV, NS = GLA_HEADS, GLA_CHUNK, GLA_DK, GLA_DV, GLA_CHUNK // GLA_SUB
    lcat = _gla_consts()
    blk = lambda w: pl.BlockSpec((GLA_NB, tb, w), lambda b, t: (b, t, 0))
    full = lambda a: pl.BlockSpec(a.shape, lambda b, t: (0,) * a.ndim)
    st_spec = pl.BlockSpec((GLA_NB, H * DK, DV), lambda b, t: (b, 0, 0))
    return pl.pallas_call(
        functools.partial(_gla_kernel, n_chunks=tb // C),
        grid=(B // GLA_NB, T // tb),
        in_specs=[blk(GLA_QK_W), blk(GLA_QK_W), blk(GLA_V_W), blk(GLA_QK_W), blk(GLA_V_W), st_spec,
                  full(gout), full(lcat)],
        out_specs=[blk(GLA_V_W), st_spec],
        out_shape=[jax.ShapeDtypeStruct((B, T, GLA_V_W), BF16),
                   jax.ShapeDtypeStruct((B, H * DK, DV), F32)],
        scratch_shapes=[pltpu.VMEM((GLA_NB, H * DK, DV), F32),
                        pltpu.VMEM((GLA_NB, H * C, NS * H * DK), BF16),
                        pltpu.VMEM((GLA_NB, H * C + H * DK, H * DV), BF16)],
        compiler_params=_cparams("parallel", "arbitrary"),
        name="gla",
    )(q, k, v, la, r, s0, gout, lcat)


def _band_attn_kernel(q_ref, kp_ref, kc_ref, vp_ref, vc_ref, o_ref, lse_ref, k_scr, v_scr, *, dil, n_sub, qs):
    i = pl.program_id(2)
    k_scr[0:BAND, :] = kp_ref[...].astype(BF16)
    k_scr[BAND:, :] = kc_ref[...].astype(BF16)
    v_scr[0:BAND, :] = vp_ref[...].astype(BF16)
    v_scr[BAND:, :] = vc_ref[...].astype(BF16)
    nk = qs + BAND
    t = lax.broadcasted_iota(jnp.int32, (qs, nk), 0)
    c = lax.broadcasted_iota(jnp.int32, (qs, nk), 1)
    dist = BAND + t - c
    in_band = (dist >= 0) & (dist <= BAND)
    distf = (dist * dil).astype(F32)
    first = lax.broadcasted_iota(jnp.int32, (nk, LANES), 1) < SWA_HD
    first_q = lax.broadcasted_iota(jnp.int32, (qs, LANES), 1) < SWA_HD
    zero = jnp.zeros((nk, LANES), BF16)
    ones_bd = jnp.concatenate([first, jnp.logical_not(first)], axis=0).astype(BF16)

    def sub_block(j, carry):
        r0 = pl.multiple_of(j * qs, qs)
        rows = pl.ds(r0, qs)
        valid = in_band & ((c >= BAND) | (i > 0) | (j > 0))
        for pr in range(SWA_HEADS // 2):
            cols = slice(pr * LANES, (pr + 1) * LANES)
            q2 = (q_ref[rows, cols].astype(F32) * (SWA_HD ** -0.5)).astype(BF16)
            k2 = k_scr[pl.ds(r0, nk), cols]
            v2 = v_scr[pl.ds(r0, nk), cols]
            k_bd = jnp.concatenate([jnp.where(first, k2, zero), jnp.where(first, zero, k2)], axis=0)
            v_bd = jnp.concatenate([jnp.where(first, v2, zero), jnp.where(first, zero, v2)], axis=0)
            s2 = _dot_nt(q2, k_bd)
            ps, ms = [], []
            for u in range(2):
                s = s2[:, u * nk:(u + 1) * nk] - (2.0 ** -(2 * pr + u + 1)) * distf
                s = jnp.where(valid, s, NEG)
                m = jnp.max(s, axis=-1, keepdims=True)
                ps.append(jnp.exp(s - m).astype(BF16))
                ms.append(m)
            od = _dot(jnp.concatenate(ps, axis=1), jnp.concatenate([v_bd, ones_bd], axis=1))
            den = od[:, LANES:]
            o_ref[rows, cols] = od[:, :LANES] / den
            lse_ref[rows, cols] = jnp.where(first_q, ms[0], ms[1]) + jnp.log(den)
        return carry

    lax.fori_loop(0, n_sub, sub_block, 0)


def _band_attn(q, k, v, dil):
    B, Tc, dw = q.shape
    W = dw // dil
    qb = min(4 * BAND, Tc)
    qs = min(2 * BAND, Tc)
    n_sub = qb // qs
    cur = pl.BlockSpec((None, qb, W), lambda b, r, i: (b, i, r))
    prev = pl.BlockSpec((None, BAND, W), lambda b, r, i: (b, jnp.maximum(i * (qb // BAND) - 1, 0), r))
    o, lse = pl.pallas_call(
        functools.partial(_band_attn_kernel, dil=dil, n_sub=n_sub, qs=qs),
        grid=(B, dil, Tc // qb),
        in_specs=[cur, prev, cur, prev, cur],
        out_specs=[cur, cur],
        out_shape=[jax.ShapeDtypeStruct((B, Tc, dw), F32)] * 2,
        scratch_shapes=[pltpu.VMEM((BAND + qb, W), BF16)] * 2,
        compiler_params=_cparams("parallel", "parallel", "parallel"),
        name="band_attn_d%d" % dil,
    )(q, k, k, v, v)
    return o.reshape(B * Tc, dw), lse.reshape(B * Tc, dw)


def _pattern_count(dist):
    cnt = jnp.zeros(dist.shape, F32)
    for window, dil in DILATED_PATTERNS:
        hit = (dist >= 0) & (dist <= window)
        if dil > 1:
            hit = hit & ((dist & (dil - 1)) == 0)
        cnt = cnt + hit.astype(F32)
    return cnt


def _samp_attn_kernel(q_ref, kn_ref, vn_ref, kp_ref, vp_ref, o_ref, ko_ref, vo_ref):
    P, Tn = kp_ref.shape[2], kn_ref.shape[2]
    qi = lax.broadcasted_iota(jnp.int32, (Tn, P), 0)
    dist_p = P + qi - lax.broadcasted_iota(jnp.int32, (Tn, P), 1)
    dist_n = lax.broadcasted_iota(jnp.int32, (Tn, Tn), 0) - lax.broadcasted_iota(jnp.int32, (Tn, Tn), 1)
    cnt_p, cnt_n = _pattern_count(dist_p), _pattern_count(dist_n)
    dpf, dnf = dist_p.astype(F32), dist_n.astype(F32)
    q = (q_ref[...] * (SWA_HD ** -0.5)).astype(BF16)
    outs = []
    for h in range(SWA_HEADS):
        kp, vp, kn, vn = kp_ref[h], vp_ref[h], kn_ref[h], vn_ref[h]
        ko_ref[h] = pltpu.roll(kp, P - Tn, axis=1)
        ko_ref[h, :, P - Tn:] = kn
        vo_ref[h] = pltpu.roll(vp, P - Tn, axis=1)
        vo_ref[h, :, P - Tn:] = vn
        qh = q[:, h * SWA_HD:(h + 1) * SWA_HD]
        slope = 2.0 ** -(h + 1)
        lp = jnp.where(cnt_p > 0, _dot(qh, kp.astype(BF16)) - slope * dpf, NEG)
        ln = jnp.where(cnt_n > 0, _dot(qh, kn.astype(BF16)) - slope * dnf, NEG)
        m = jnp.maximum(jnp.max(lp, axis=-1, keepdims=True), jnp.max(ln, axis=-1, keepdims=True))
        pp = cnt_p * jnp.exp(lp - m)
        pn = cnt_n * jnp.exp(ln - m)
        den = jnp.sum(pp, axis=-1, keepdims=True) + jnp.sum(pn, axis=-1, keepdims=True)
        outs.append((_dot_nt(pp.astype(BF16), vp.astype(BF16)) + _dot_nt(pn.astype(BF16), vn.astype(BF16))) / den)
    o_ref[...] = jnp.concatenate(outs, axis=1)


def _samp_attn(q, knT, vnT, kpT, vpT):
    B, Tn, W = q.shape
    P = kpT.shape[-1]
    qs = pl.BlockSpec((None, Tn, W), lambda b: (b, 0, 0))
    new = pl.BlockSpec((None, SWA_HEADS, SWA_HD, Tn), lambda b: (b, 0, 0, 0))
    past = pl.BlockSpec((None, SWA_HEADS, SWA_HD, P), lambda b: (b, 0, 0, 0))
    return pl.pallas_call(
        _samp_attn_kernel,
        grid=(B,),
        in_specs=[qs, new, new, past, past],
        out_specs=[qs, past, past],
        out_shape=[jax.ShapeDtypeStruct((B, Tn, W), F32), jax.ShapeDtypeStruct(kpT.shape, F32),
                   jax.ShapeDtypeStruct(kpT.shape, F32)],
        compiler_params=_cparams("parallel"),
        name="samp_attn",
    )(q, knT, vnT, kpT, vpT)


def _mem_kv_kernel(mem_ref, gmem_ref, wk_ref, wv_ref, gxk_ref, mk_ref, mv_ref):
    hm = _rms_rows(mem_ref[...], gmem_ref[...]).astype(BF16)
    mk = _dot(hm, wk_ref[...])
    gxk = gxk_ref[...]
    mv = _dot(hm, wv_ref[...])
    n = mem_ref.shape[0]
    for h in range(X_HEADS):
        sl = slice(h * X_HD, (h + 1) * X_HD)
        mk_ref[pl.ds(h, n, stride=X_HEADS), :] = _rms_rows(mk[:, sl], gxk)
        mv_ref[pl.ds(h, n, stride=X_HEADS), :] = mv[:, sl]


def _mem_kv(mem, gmem, wk, wv, gxk):
    n = mem.shape[0]
    return pl.pallas_call(
        _mem_kv_kernel,
        out_shape=[jax.ShapeDtypeStruct((n * X_HEADS, X_HD), F32)] * 2,
        compiler_params=pltpu.CompilerParams(vmem_limit_bytes=VMEM_LIMIT),
        name="mem_kv",
    )(mem, gmem, wk, wv, gxk)


def _post_kernel(*refs, n_pat, n_seg):
    n_swa = 2 * n_pat if n_pat > 1 else 1
    x_ref, og_ref = refs[0], refs[1]
    swa_refs = refs[2:2 + n_swa]
    (wout_ref, gx_ref, wxq_ref, gxq_ref, mk_ref, mv_ref, wxo_ref, gffn_ref, wr_ref, br_ref, tri_ref,
     cnt0_ref) = refs[2 + n_swa:14 + n_swa]
    n_scr = 3 if n_pat > 1 else 2
    x2_ref, h3_ref, ti_ref, tg_ref, cnt_ref = refs[len(refs) - n_scr - 5:len(refs) - n_scr]
    ox_scr, run_scr = refs[len(refs) - n_scr:len(refs) - n_scr + 2]

    @pl.when(pl.program_id(0) == 0)
    def _():
        run_scr[...] = cnt0_ref[...]

    if n_pat > 1:
        dei_scr = refs[-1]
        tm = x_ref.shape[0]
        vals = []
        for a, r in enumerate(swa_refs):
            d = DILATED_PATTERNS[a % n_pat][1]
            if d == 1:
                vals.append(r[...])
            else:
                slot, n_col = len(vals), SWA_W // LANES
                for c in range(d):
                    for g in range(n_col):
                        lo = c * SWA_W + g * LANES
                        dei_scr[slot, g, pl.ds(c, tm // d, stride=d), :] = r[:, lo:lo + LANES]
                vals.append(jnp.concatenate([dei_scr[slot, g] for g in range(n_col)], axis=1))
        o_p, l_p = vals[:n_pat], vals[n_pat:]
        lmax = functools.reduce(jnp.maximum, l_p)
        w_p = [jnp.exp(l - lmax) for l in l_p]
        o_swa = sum(w * o for w, o in zip(w_p, o_p)) / sum(w_p)
    else:
        o_swa = swa_refs[0][...]

    x1 = x_ref[...] + _dot(og_ref[...], wout_ref[0:GLA_V_W, :]) + _dot(o_swa.astype(BF16), wout_ref[GLA_V_W:, :])

    q = _dot(_rms_rows(x1, gx_ref[...]).astype(BF16), wxq_ref[...])
    gxq = gxq_ref[...]
    rows_t, seg = x1.shape[0], x1.shape[0] // n_seg
    if n_seg > 1:
        own = (lax.broadcasted_iota(jnp.int32, (rows_t, n_seg * MEM_LEN), 0) // seg
               == lax.broadcasted_iota(jnp.int32, (rows_t, n_seg * MEM_LEN), 1) // MEM_LEN)
    for h in range(X_HEADS):
        sl = slice(h * X_HD, (h + 1) * X_HD)
        qn = _rms_rows(q[:, sl], gxq).astype(BF16)
        mem_rows = pl.ds(h, MEM_LEN, stride=X_HEADS)
        mk_h = jnp.concatenate([mk_ref[s, mem_rows, :] for s in range(n_seg)], axis=0).astype(BF16)
        mv_h = jnp.concatenate([mv_ref[s, mem_rows, :] for s in range(n_seg)], axis=0).astype(BF16)
        sc = _dot_nt(qn, mk_h) * (X_HD ** -0.5)
        if n_seg > 1:
            sc = jnp.where(own, sc, NEG)
        p = jnp.exp(sc - jnp.max(sc, axis=-1, keepdims=True))
        p = p / jnp.sum(p, axis=-1, keepdims=True)
        ox_scr[:, sl] = _dot(p.astype(BF16), mv_h)
    x2 = x1 + _dot(ox_scr[...].astype(BF16), wxo_ref[...])
    x2_ref[...] = x2

    h3 = _rms_rows(x2, gffn_ref[...]).astype(BF16)
    h3_ref[...] = h3
    work = _dot(h3, wr_ref[...]) + br_ref[...]
    lane = lax.broadcasted_iota(jnp.int32, work.shape, 1)
    vals, idxs = [], []
    for _ in range(TOP_K):
        m = jnp.max(work, axis=-1, keepdims=True)
        idx = jnp.min(jnp.where(work == m, lane, LANES), axis=-1, keepdims=True)
        vals.append(m)
        idxs.append(idx)
        work = jnp.where(lane == idx, -jnp.inf, work)
    es = [jnp.exp(v - vals[0]) for v in vals]
    den = sum(es)
    sel = [lane == idx for idx in idxs]
    onehot = functools.reduce(jnp.logical_or, sel).astype(BF16)
    run = run_scr[...]
    rank = _dot(tri_ref[...], onehot) + run
    run_scr[...] = run + jnp.sum(onehot.astype(F32), axis=0, keepdims=True)
    cnt_ref[...] = run_scr[...]
    ti = jnp.zeros(work.shape, jnp.int32)
    tg = jnp.zeros(work.shape, F32)
    for j in range(TOP_K):
        pos = jnp.sum(jnp.where(sel[j], rank, 0.0), axis=-1, keepdims=True).astype(jnp.int32)
        ti = jnp.where(lane == j, idxs[j], ti)
        ti = jnp.where(lane == TOP_K + j, pos, ti)
        tg = jnp.where(lane == j, es[j] / den, tg)
    ti_ref[...] = ti
    tg_ref[...] = tg


def _post(x, og, swa, wout, gx, wxq, gxq, mk, mv, wxo, gffn, wr, br, cnt0, tm, n_seg, rows_per_mem, h3_rows, h3_row0,
          h3_prior):
    n = x.shape[0]
    n_pat = len(swa) // 2 if len(swa) > 1 else 1
    tri = jnp.asarray(np.tril(np.ones((tm, tm), np.float32), -1), BF16)
    row = lambda w: pl.BlockSpec((tm, w), lambda i: (i, 0))
    full = lambda a: pl.BlockSpec(a.shape, lambda i: (0,) * a.ndim)
    mem = pl.BlockSpec((n_seg, MEM_LEN * X_HEADS, X_HD), lambda i: ((i * tm) // (rows_per_mem * n_seg), 0, 0))
    swa_specs = [pl.BlockSpec((tm * SWA_W // a.shape[1], a.shape[1]), lambda i: (i, 0)) for a in swa]
    scratch = [pltpu.VMEM((tm, X_W), F32), pltpu.VMEM((1, LANES), F32)]
    if n_pat > 1:
        scratch.append(pltpu.VMEM((len(swa), SWA_W // LANES, tm, LANES), F32))
    return pl.pallas_call(
        functools.partial(_post_kernel, n_pat=n_pat, n_seg=n_seg),
        grid=(n // tm,),
        in_specs=[row(D_MODEL), row(GLA_V_W)] + swa_specs
        + [full(wout), full(gx), full(wxq), full(gxq), mem, mem, full(wxo), full(gffn), full(wr), full(br),
           full(tri), full(cnt0)] + [pl.BlockSpec(memory_space=pl.ANY)] * len(h3_prior),
        out_specs=[row(D_MODEL), pl.BlockSpec((tm, D_MODEL), lambda i: (i + h3_row0 // tm, 0)), row(LANES), row(LANES),
                   full(cnt0)],
        out_shape=[jax.ShapeDtypeStruct((n, D_MODEL), F32), jax.ShapeDtypeStruct((h3_rows, D_MODEL), BF16),
                   jax.ShapeDtypeStruct((n, LANES), jnp.int32), jax.ShapeDtypeStruct((n, LANES), F32),
                   jax.ShapeDtypeStruct((1, LANES), F32)],
        scratch_shapes=scratch,
        compiler_params=_cparams("arbitrary"),
        name="post",
        input_output_aliases={14 + len(swa): 1} if h3_prior else {},
    )(x, og, *swa, wout, gx, wxq, gxq, mk, mv, wxo, gffn, wr, br, tri, cnt0, *h3_prior)


def _moe_kernel(te_ref, nu_ref, x_ref, wg_ref, bg_ref, wu_ref, bu_ref, wd_ref, bd_ref, *rest):
    y_ref, wg_scr, wu_scr, wd_scr = rest[-4:]
    i = pl.program_id(0)
    live = i < nu_ref[0]

    @pl.when(live & ((i == 0) | (te_ref[i] != te_ref[jnp.maximum(i - 1, 0)])))
    def _():
        wg_scr[...] = wg_ref[...].astype(BF16)
        wu_scr[...] = wu_ref[...].astype(BF16)
        wd_scr[...] = wd_ref[...].astype(BF16)

    @pl.when(live)
    def _():
        x = x_ref[...]
        g = jnp.minimum(_dot(x, wg_scr[...]) + bg_ref[...], SWIGLU_LIMIT)
        u = jnp.clip(_dot(x, wu_scr[...]) + bu_ref[...], -SWIGLU_LIMIT, SWIGLU_LIMIT)
        a = g * _sigmoid(SWIGLU_ALPHA * g) * (u + 1.0)
        y_ref[...] = (_dot(a.astype(BF16), wd_scr[...]) + bd_ref[...]).astype(y_ref.dtype)

    @pl.when(jnp.logical_not(live))
    def _():
        y_ref[...] = jnp.zeros(y_ref.shape, y_ref.dtype)


MOE_CHUNK_ENDS = (0.05, 0.25, 1.0)


def _moe(tile_expert, n_used, h3, src_tok, wg, bg, wu, bu, wd, bd):
    P = src_tok.shape[0]
    tm = MOE_TM
    n_tiles = P // tm
    ys = None
    ends = [max(1, round(n_tiles * f)) for f in MOE_CHUNK_ENDS]
    for lo, hi in zip([0] + ends[:-1], ends):
        w_spec = pl.BlockSpec((None, D_MODEL, D_MODEL), lambda i, te, *_: (te[i], 0, 0))
        b_spec = pl.BlockSpec((None, 1, D_MODEL), lambda i, te, *_: (te[i], 0, 0))
        prior = [] if ys is None else [ys]
        ys = pl.pallas_call(
            _moe_kernel,
            grid_spec=pltpu.PrefetchScalarGridSpec(
                num_scalar_prefetch=2,
                grid=(hi - lo,),
                in_specs=[pl.BlockSpec((tm, D_MODEL), lambda i, *_: (i, 0)),
                          w_spec, b_spec, w_spec, b_spec, w_spec, b_spec]
                + [pl.BlockSpec(memory_space=pl.ANY)] * len(prior),
                out_specs=pl.BlockSpec((tm, D_MODEL), lambda i, *_, lo=lo: (i + lo, 0)),
                scratch_shapes=[pltpu.VMEM((D_MODEL, D_MODEL), BF16)] * 3,
            ),
            out_shape=jax.ShapeDtypeStruct((P, D_MODEL), BF16),
            input_output_aliases={9: 0} if prior else {},
            compiler_params=_cparams("arbitrary"),
            name="moe",
        )(tile_expert[lo:hi], n_used - lo, _sc_take(h3, src_tok[lo * tm:hi * tm]), wg, bg, wu, bu, wd, bd, *prior)
    return ys


def _route(top_i, rank, counts, tm):
    n = top_i.shape[0]
    a = n * TOP_K
    n_tiles = a // tm + N_EXPERTS
    tiles_e = (counts + tm - 1) // tm
    tile_end = jnp.cumsum(tiles_e)
    slot0 = (tile_end - tiles_e) * tm
    experts = jnp.arange(N_EXPERTS, dtype=jnp.int32)
    dest = rank + jnp.sum(jnp.where(top_i[..., None] == experts, slot0, 0), axis=-1)
    n_used = tile_end[-1:].astype(jnp.int32)
    tile_expert = jnp.minimum(
        jnp.sum((tile_end[None, :] <= jnp.arange(n_tiles, dtype=jnp.int32)[:, None]).astype(jnp.int32), axis=1),
        N_EXPERTS - 1)
    pad_tok = -1 - (jnp.arange(n_tiles * tm, dtype=jnp.int32) % n)
    src_tok = pad_tok.at[dest.reshape(a)].max(jnp.arange(a, dtype=jnp.int32) // TOP_K, unique_indices=True)
    src_tok = jnp.where(src_tok < 0, -1 - src_tok, src_tok)
    return tile_expert, n_used, src_tok, dest


def _combine_kernel(x_ref, y_ref, g_ref, *rest):
    o_ref = rest[-1]
    acc = x_ref[...]
    g = g_ref[...]
    for j in range(TOP_K):
        acc = acc + g[:, j:j + 1] * y_ref[j].astype(F32)
    o_ref[...] = acc


COMBINE_CHUNKS = 8


def _combine(x2, ysg, tg, row0, tm, prior):
    n = ysg.shape[1]
    off = row0 // tm
    return pl.pallas_call(
        _combine_kernel,
        grid=(n // tm,),
        in_specs=[pl.BlockSpec((tm, D_MODEL), lambda i: (i + off, 0)),
                  pl.BlockSpec((TOP_K, tm, D_MODEL), lambda i: (0, i, 0)),
                  pl.BlockSpec((tm, LANES), lambda i: (i + off, 0))] + [pl.BlockSpec(memory_space=pl.ANY)] * len(prior),
        out_specs=pl.BlockSpec((tm, D_MODEL), lambda i: (i + off, 0)),
        out_shape=jax.ShapeDtypeStruct(x2.shape, F32),
        input_output_aliases={3: 0} if prior else {},
        compiler_params=_cparams("parallel"),
        name="combine",
    )(x2, ysg, tg, *prior)


@compute_on("tpu_sparsecore")
@jax.jit
def _sc_take(x, idx):
    return jnp.take(x, idx, axis=0, mode="clip")


def _mixer_inputs(x2d, w, tm, classes, seq_len=0, win_len=0):
    return _in_proj(x2d, w["gmix"], w["w_in_p"], w["wa2_p"], w["ba"], w["gsq"], w["gsk"], w["bd"], tm, classes,
                    seq_len, win_len)


def kernel(x_prompt, x_sample, mem_prompt, state_gla, cache_swa_k, cache_swa_v, cache_mem_k, cache_mem_v, g_mix, w_in, w_gla_a2, b_gla_a, g_gla_out, g_swa_q, g_swa_k, w_out, g_mem, w_mk, w_mv, g_xk, g_xattn, w_xq, g_xq, w_xo, g_ffn, w_router, b_router, w_gate, b_gate, w_up, b_up, w_down, b_down):
    B, T, D = x_prompt.shape
    Bs, Ts, _ = x_sample.shape
    P = cache_swa_k.shape[2]
    l = 0

    wi = w_in[l]
    segs = np.cumsum((0, GLA_QK_W, GLA_QK_W, GLA_V_W, GLA_V_W, GLA_RANK, SWA_W, SWA_W, SWA_W))
    gq_c, gk_c, gv_c, gr_c, ga_c, sq_c, sk_c, sv_c = [wi[:, segs[j]:segs[j + 1]] for j in range(8)]
    w_in_p = jnp.concatenate(
        [gq_c, gk_c, gv_c, gr_c, sq_c, sk_c, sv_c, ga_c, jnp.zeros((D, D_IN_PAD - _C_GA - GLA_RANK), F32)],
        axis=1).astype(BF16)
    heads = np.arange(SWA_W) // SWA_HD
    w = dict(
        gmix=g_mix[l][None], w_in_p=w_in_p,
        wa2_p=jnp.concatenate([w_gla_a2[l], jnp.zeros((LANES - GLA_RANK, GLA_QK_W), F32)], axis=0).astype(BF16),
        ba=b_gla_a[l][None],
        gsq=jnp.tile(g_swa_q[l], SWA_HEADS)[None], gsk=jnp.tile(g_swa_k[l], SWA_HEADS)[None],
        bd=jnp.asarray(heads[:, None] == heads[None, :], BF16),
    )
    gout = g_gla_out[l][None]
    wout = w_out[l].astype(BF16)
    wxq, wxo = w_xq[l].astype(BF16), w_xo[l].astype(BF16)
    wr = jnp.concatenate([w_router[l], jnp.zeros((D, LANES - N_EXPERTS), F32)], axis=1).astype(BF16)
    br = jnp.concatenate([b_router[l], jnp.full((LANES - N_EXPERTS,), NEG, F32)])[None]
    post_w = (wout, g_xattn[l][None], wxq, g_xq[l][None])
    post_w2 = (wxo, g_ffn[l][None], wr, br)

    w_buf = min(SWA_WINDOW, T)
    gq, gk, gv, gr, la, sq, sk, sv, *cls, kT_p, vT_p = _mixer_inputs(x_prompt.reshape(B * T, D), w, IN_PROJ_TM, True,
                                                                     T, w_buf)
    r3 = lambda a: a.reshape(B, T, a.shape[-1])
    og_p, st_p = _gla(r3(gq), r3(gk), r3(gv), r3(la), r3(gr),
                      jnp.zeros((B, GLA_QK_W, GLA_DV), F32), gout, GLA_TB)
    o_pat, l_pat = [], []
    for _, dil in DILATED_PATTERNS:
        if dil == 1:
            qkv = [r3(sq), r3(sk), r3(sv)]
        else:
            c = CLASS_DILS.index(dil)
            qkv = [cls[j * len(CLASS_DILS) + c].reshape(B, T // dil, dil * SWA_W) for j in range(3)]
        o_c, l_c = _band_attn(*qkv, dil)
        o_pat.append(o_c)
        l_pat.append(l_c)
    mk_p, mv_p = _mem_kv(mem_prompt.reshape(B * MEM_LEN, D), g_mem[l][None], w_mk[l].astype(BF16),
                         w_mv[l].astype(BF16), g_xk[l][None])
    n_p, n_all = B * T, B * T + Bs * Ts
    x2_p, h3, ti_p, tg_p, cnt_p = _post(x_prompt.reshape(B * T, D), og_p.reshape(B * T, GLA_V_W), o_pat + l_pat,
                                        *post_w, mk_p.reshape(B, MEM_LEN * X_HEADS, X_HD),
                                        mv_p.reshape(B, MEM_LEN * X_HEADS, X_HD),
                                        *post_w2, jnp.zeros((1, LANES), F32), tm=POST_TM, n_seg=1, rows_per_mem=T,
                                        h3_rows=n_all, h3_row0=0, h3_prior=[])

    gq, gk, gv, gr, la, sq, sk, sv = _mixer_inputs(x_sample.reshape(Bs * Ts, D), w, Bs * Ts, False)
    pad = lambda a: jnp.pad(a.reshape(Bs, Ts, a.shape[-1]), ((0, 0), (0, GLA_CHUNK - Ts), (0, 0)))
    og_s, st_s = _gla(pad(gq), pad(gk), pad(gv), pad(la), pad(gr),
                      state_gla[l].reshape(Bs, GLA_QK_W, GLA_DV), gout, GLA_CHUNK)
    og_s = og_s[:, :Ts].reshape(Bs * Ts, GLA_V_W)
    rows_minor = lambda a: jnp.transpose(a.reshape(Bs, -1, SWA_HEADS, SWA_HD), (0, 2, 3, 1))
    o_swa_s, kb_s, vb_s = _samp_attn(sq.reshape(Bs, Ts, SWA_W), rows_minor(sk), rows_minor(sv),
                                     rows_minor(cache_swa_k[l]), rows_minor(cache_swa_v[l]))
    kb_s, vb_s = jnp.transpose(kb_s, (0, 3, 1, 2)), jnp.transpose(vb_s, (0, 3, 1, 2))
    x2_s, h3, ti_s, tg_s, cnt = _post(x_sample.reshape(Bs * Ts, D), og_s, [o_swa_s.reshape(Bs * Ts, SWA_W)],
                                      *post_w, cache_mem_k[l].reshape(Bs, MEM_LEN * X_HEADS, X_HD),
                                      cache_mem_v[l].reshape(Bs, MEM_LEN * X_HEADS, X_HD),
                                      *post_w2, cnt_p, tm=SAMPLE_SEQS * Ts, n_seg=SAMPLE_SEQS, rows_per_mem=Ts,
                                      h3_rows=n_all, h3_row0=n_p, h3_prior=[h3])

    ti = jnp.concatenate([ti_p[:, :2 * TOP_K], ti_s[:, :2 * TOP_K]], axis=0)
    tile_expert, n_used, src_tok, dest = _route(ti[:, :TOP_K], ti[:, TOP_K:], cnt[0, :N_EXPERTS].astype(jnp.int32),
                                                MOE_TM)
    ys = _moe(tile_expert, n_used, h3, src_tok, w_gate[l], b_gate[l][:, None, :], w_up[l],
              b_up[l][:, None, :], w_down[l], b_down[l][:, None, :])
    def combined(x2, tg, row0, first, n, tm, prior):
        ysg = _sc_take(ys, dest[first:first + n].T.reshape(-1)).reshape(TOP_K, n, D)
        return [_combine(x2, ysg, tg, row0, tm, prior)]

    y_p, n_c = [], n_p // COMBINE_CHUNKS
    for c in range(COMBINE_CHUNKS):
        y_p = combined(x2_p, tg_p, c * n_c, c * n_c, n_c, COMBINE_TM, y_p)
    y_p, y_s = y_p[0], combined(x2_s, tg_s, 0, n_p, Bs * Ts, Bs * Ts, [])[0]

    kv_p = lambda a: jnp.transpose(a.reshape(B, SWA_HEADS, SWA_HD, w_buf), (0, 3, 1, 2))[None]
    return (y_p.reshape(B, T, D), y_s.reshape(Bs, Ts, D),
            st_p.reshape(1, B, GLA_HEADS, GLA_DK, GLA_DV), kv_p(kT_p), kv_p(vT_p),
            mk_p.reshape(1, B, MEM_LEN, X_HEADS, X_HD), mv_p.reshape(1, B, MEM_LEN, X_HEADS, X_HD),
            st_s.reshape(1, Bs, GLA_HEADS, GLA_DK, GLA_DV),
            kb_s.reshape(1, Bs, P, SWA_HEADS, SWA_HD), vb_s.reshape(1, Bs, P, SWA_HEADS, SWA_HD))
```

```python
import functools

import numpy as np
import jax
import jax.numpy as jnp
from jax import lax
from jax.experimental import pallas as pl
from jax.experimental.compute_on import compute_on
from jax.experimental.pallas import tpu as pltpu

F32 = jnp.float32
BF16 = jnp.bfloat16

D_MODEL = 1024
GLA_HEADS = 4
GLA_DK = 64
GLA_DV = 128
GLA_RANK = 16
GLA_TAU = 16.0
GLA_CHUNK = 128
GLA_SUB = 32
SWA_HEADS = 8
SWA_HD = 64
DILATED_PATTERNS = ((128, 1), (512, 4), (2048, 16))
BAND = 128
SWA_WINDOW = 2048
MEM_LEN = 256
X_HEADS = 4
X_HD = 128
N_EXPERTS = 32
TOP_K = 4
SWIGLU_ALPHA = 1.702
SWIGLU_LIMIT = 7.0
EPS = 1e-6

GLA_QK_W = GLA_HEADS * GLA_DK
GLA_V_W = GLA_HEADS * GLA_DV
SWA_W = SWA_HEADS * SWA_HD
X_W = X_HEADS * X_HD
LANES = 128
NEG = -1e30
VMEM_LIMIT = 56 * 1024 * 1024

_C_GQ, _C_GK, _C_GV, _C_GR, _C_SQ, _C_SK, _C_SV, _C_GA = 0, 256, 512, 1024, 1536, 2048, 2560, 3072
D_IN_PAD = 3200

IN_PROJ_TM = 512
GLA_TB = 512
POST_TM = 512
COMBINE_TM = 512
MOE_TM = 512
SAMPLE_SEQS = 8


def _cparams(*sem):
    return pltpu.CompilerParams(dimension_semantics=sem, vmem_limit_bytes=VMEM_LIMIT)


def _dot(a, b):
    return jnp.dot(a, b, preferred_element_type=F32)


def _dot_nt(a, b):
    return lax.dot_general(a, b, (((1,), (1,)), ((), ())), preferred_element_type=F32)


def _rms_rows(x, g):
    return x * lax.rsqrt(jnp.mean(x * x, axis=-1, keepdims=True) + EPS) * g


def _split_bf16(x, n):
    out = []
    for _ in range(n - 1):
        hi = x.astype(BF16)
        out.append(hi)
        x = x - hi.astype(F32)
    out.append(x.astype(BF16))
    return out


def _group_rms(z, g, ones_bd, group):
    hi, lo = _split_bf16(z * z, 2)
    ss = _dot(hi, ones_bd) + _dot(lo, ones_bd)
    return z * lax.rsqrt(ss * (1.0 / group) + EPS) * g


def _log_sigmoid(x):
    return jnp.minimum(x, 0.0) - jnp.log1p(jnp.exp(-jnp.abs(x)))


def _sigmoid(x):
    return 1.0 / (1.0 + jnp.exp(-x))


CLASS_DILS = tuple(d for _, d in DILATED_PATTERNS if d > 1)


def _in_proj_kernel(x_ref, gmix_ref, w_ref, wa2_ref, ba_ref, gsq_ref, gsk_ref, bd_ref,
                    oq_ref, ok_ref, ov_ref, or_ref, ola_ref, osq_ref, osk_ref, osv_ref, *rest, classes, seq_tiles,
                    win_tiles):
    h = _rms_rows(x_ref[...], gmix_ref[...]).astype(BF16)

    def proj(lo, width):
        return _dot(h, w_ref[:, lo:lo + width])

    oq_ref[...] = proj(_C_GQ, GLA_QK_W) * (GLA_DK ** -0.5)
    ok_ref[...] = proj(_C_GK, GLA_QK_W)
    ov_ref[...] = proj(_C_GV, GLA_V_W)
    or_ref[...] = proj(_C_GR, GLA_V_W)
    osv_ref[...] = proj(_C_SV, SWA_W)
    ga = proj(_C_GA, LANES)
    xa = _dot(ga.astype(BF16), wa2_ref[...]) + ba_ref[...]
    ola_ref[...] = _log_sigmoid(xa) * (1.0 / GLA_TAU)
    bd = bd_ref[...]
    osq_ref[...] = _group_rms(proj(_C_SQ, SWA_W), gsq_ref[...], bd, SWA_HD)
    osk_ref[...] = _group_rms(proj(_C_SK, SWA_W), gsk_ref[...], bd, SWA_HD)

    if classes:
        kT_ref, vT_ref = rest[-3], rest[-2]
        rest = rest[:-3] + rest[-1:]

        @pl.when(pl.program_id(0) % seq_tiles >= seq_tiles - win_tiles)
        def _():
            kT_ref[...] = osk_ref[...].T
            vT_ref[...] = osv_ref[...].T

        cls_refs, col_scr = rest[:-1], rest[-1]
        tm = x_ref.shape[0]
        n_col = SWA_W // LANES
        for j, src in enumerate((osq_ref, osk_ref, osv_ref)):
            for g in range(n_col):
                col_scr[g] = src[:, g * LANES:(g + 1) * LANES]
            for c, d in enumerate(CLASS_DILS):
                dst = cls_refs[j * len(CLASS_DILS) + c]
                for r in range(d):
                    for g in range(n_col):
                        lo = r * SWA_W + g * LANES
                        dst[:, lo:lo + LANES] = col_scr[g, pl.ds(r, tm // d, stride=d), :].astype(BF16)


def _in_proj(x, gmix, w_in_p, wa2_p, ba, gsq, gsk, bd, tm, classes, seq_len=0, win_len=0):
    n = x.shape[0]
    row = lambda w: pl.BlockSpec((tm, w), lambda i: (i, 0))
    full = lambda a: pl.BlockSpec(a.shape, lambda i: (0,) * a.ndim)
    widths = (GLA_QK_W, GLA_QK_W, GLA_V_W, GLA_V_W, GLA_QK_W, SWA_W, SWA_W, SWA_W)
    out_specs = [row(w) for w in widths]
    out_shape = [jax.ShapeDtypeStruct((n, w), F32) for w in widths]
    seq_tiles, win_tiles = seq_len // tm, win_len // tm
    if classes:
        for _ in range(3):
            for d in CLASS_DILS:
                out_specs.append(pl.BlockSpec((tm // d, d * SWA_W), lambda i: (i, 0)))
                out_shape.append(jax.ShapeDtypeStruct((n // d, d * SWA_W), BF16))
        win = pl.BlockSpec((None, SWA_W, tm), lambda i: (i // seq_tiles, 0,
                                                         jnp.maximum(i % seq_tiles - (seq_tiles - win_tiles), 0)))
        out_specs += [win, win]
        out_shape += [jax.ShapeDtypeStruct((n // seq_len, SWA_W, win_len), F32)] * 2
    return pl.pallas_call(
        functools.partial(_in_proj_kernel, classes=classes, seq_tiles=seq_tiles, win_tiles=win_tiles),
        grid=(n // tm,),
        in_specs=[row(D_MODEL), full(gmix), full(w_in_p), full(wa2_p), full(ba), full(gsq), full(gsk), full(bd)],
        out_specs=out_specs,
        out_shape=out_shape,
        scratch_shapes=[pltpu.VMEM((SWA_W // LANES, tm, LANES), F32)] if classes else [],
        compiler_params=_cparams("arbitrary"),
        name="in_proj",
    )(x, gmix, w_in_p, wa2_p, ba, gsq, gsk, bd)


GLA_NB = 2


def _gla_kernel(q_ref, k_ref, v_ref, la_ref, r_ref, s0_ref, gout_ref, lcat_ref,
                o_ref, s_ref, s_scr, knt_scr, rhs_scr, *, n_chunks):
    tb = pl.program_id(1)
    C, S, NS, H, DK, DV = GLA_CHUNK, GLA_SUB, GLA_CHUNK // GLA_SUB, GLA_HEADS, GLA_DK, GLA_DV

    @pl.when(tb == 0)
    def _():
        s_scr[...] = s0_ref[...]
        knt_scr[...] = jnp.zeros(knt_scr.shape, BF16)
        rhs_scr[...] = jnp.zeros(rhs_scr.shape, BF16)

    lcat = lcat_ref[...]
    gout = gout_ref[...]
    row = lax.broadcasted_iota(jnp.int32, (C, H * DK), 0)
    tril = (lax.broadcasted_iota(jnp.int32, (C, H * C), 1) & (C - 1)) <= lax.broadcasted_iota(jnp.int32, (C, H * C), 0)
    ones_k = jnp.ones((3 * C, LANES), BF16)

    def chunk(c, carry):
        r0 = pl.multiple_of(c * C, C)
        rows = pl.ds(r0, C)
        for n in range(GLA_NB):
            g = la_ref[n, rows, :]
            b_r = _dot(lcat, jnp.concatenate(_split_bf16(g, 3), axis=1))
            b_r = b_r[:, :H * DK] + b_r[:, H * DK:2 * H * DK] + b_r[:, 2 * H * DK:]
            b = b_r[:C]
            ref = b_r[C:]
            b_end = b[C - 1:C, :]
            b_col = _dot(jnp.concatenate(_split_bf16(g.T, 3), axis=1), ones_k)
            q = q_ref[n, rows, :]
            k = k_ref[n, rows, :]
            v = v_ref[n, rows, :]
            q_hat = q * jnp.exp(b - ref)
            q_til = q * jnp.exp(b)
            k_dec = k * jnp.exp(b_end - b)
            for i in range(NS):
                e = ref[S * i:S * i + 1, :] - b
                piece = (k * jnp.exp(jnp.where(row < S * (i + 1), e, NEG))).astype(BF16)
                for h in range(H):
                    lo = i * H * DK + h * DK
                    knt_scr[n, h * C:(h + 1) * C, lo:lo + DK] = piece[:, h * DK:(h + 1) * DK]
            q_cat = jnp.concatenate([jnp.where((row >= S * i) & (row < S * (i + 1)), q_hat, 0.0) for i in range(NS)],
                                    axis=1).astype(BF16)
            sc = jnp.where(tril, _dot_nt(q_cat, knt_scr[n]), 0.0)
            st = s_scr[n]
            for h in range(H):
                vl = slice(h * DV, (h + 1) * DV)
                rhs_scr[n, h * C:(h + 1) * C, vl] = v[:, vl].astype(BF16)
                rhs_scr[n, H * C + h * DK:H * C + (h + 1) * DK, vl] = st[h * DK:(h + 1) * DK, :].astype(BF16)
            o = _dot(jnp.concatenate([sc, q_til], axis=1).astype(BF16), rhs_scr[n])
            kv = _dot(k_dec.T.astype(BF16), v.astype(BF16))
            s_scr[n] = st * jnp.exp(b_col) + jnp.concatenate(
                [kv[h * DK:(h + 1) * DK, h * DV:(h + 1) * DV] for h in range(H)], axis=0)
            gate = r_ref[n, rows, :]
            for h in range(H):
                vl = slice(h * DV, (h + 1) * DV)
                o_ref[n, rows, vl] = (_rms_rows(o[:, vl], gout) * (gate[:, vl] * _sigmoid(gate[:, vl]))).astype(o_ref.dtype)
        return carry

    lax.fori_loop(0, n_chunks, chunk, 0)

    @pl.when(tb == pl.num_programs(1) - 1)
    def _():
        s_ref[...] = s_scr[...]


def _gla_consts():
    C, S = GLA_CHUNK, GLA_SUB
    t = np.arange(C)
    incl = (t[None, :] <= t[:, None]).astype(np.float32)
    upto = (t[None, :] < (t[:, None] // S) * S).astype(np.float32)
    return jnp.asarray(np.concatenate([incl, upto], axis=0), BF16)


def _gla(q, k, v, la, r, s0, gout, tb):
    B, T, _ = q.shape
    H, C, DK, DV, NS = GLA_HEADS, GLA_CHUNK, GLA_DK, GLA_DV, GLA_CHUNK // GLA_SUB
    lcat = _gla_consts()
    blk = lambda w: pl.BlockSpec((GLA_NB, tb, w), lambda b, t: (b, t, 0))
    full = lambda a: pl.BlockSpec(a.shape, lambda b, t: (0,) * a.ndim)
    st_spec = pl.BlockSpec((GLA_NB, H * DK, DV), lambda b, t: (b, 0, 0))
    return pl.pallas_call(
        functools.partial(_gla_kernel, n_chunks=tb // C),
        grid=(B // GLA_NB, T // tb),
        in_specs=[blk(GLA_QK_W), blk(GLA_QK_W), blk(GLA_V_W), blk(GLA_QK_W), blk(GLA_V_W), st_spec,
                  full(gout), full(lcat)],
        out_specs=[blk(GLA_V_W), st_spec],
        out_shape=[jax.ShapeDtypeStruct((B, T, GLA_V_W), BF16),
                   jax.ShapeDtypeStruct((B, H * DK, DV), F32)],
        scratch_shapes=[pltpu.VMEM((GLA_NB, H * DK, DV), F32),
                        pltpu.VMEM((GLA_NB, H * C, NS * H * DK), BF16),
                        pltpu.VMEM((GLA_NB, H * C + H * DK, H * DV), BF16)],
        compiler_params=_cparams("parallel", "arbitrary"),
        name="gla",
    )(q, k, v, la, r, s0, gout, lcat)


def _band_attn_kernel(q_ref, kp_ref, kc_ref, vp_ref, vc_ref, o_ref, lse_ref, k_scr, v_scr, *, dil, n_sub, qs):
    i = pl.program_id(2)
    k_scr[0:BAND, :] = kp_ref[...].astype(BF16)
    k_scr[BAND:, :] = kc_ref[...].astype(BF16)
    v_scr[0:BAND, :] = vp_ref[...].astype(BF16)
    v_scr[BAND:, :] = vc_ref[...].astype(BF16)
    nk = qs + BAND
    t = lax.broadcasted_iota(jnp.int32, (qs, nk), 0)
    c = lax.broadcasted_iota(jnp.int32, (qs, nk), 1)
    dist = BAND + t - c
    in_band = (dist >= 0) & (dist <= BAND)
    distf = (dist * dil).astype(F32)
    first = lax.broadcasted_iota(jnp.int32, (nk, LANES), 1) < SWA_HD
    first_q = lax.broadcasted_iota(jnp.int32, (qs, LANES), 1) < SWA_HD
    zero = jnp.zeros((nk, LANES), BF16)
    ones_bd = jnp.concatenate([first, jnp.logical_not(first)], axis=0).astype(BF16)

    def sub_block(j, carry):
        r0 = pl.multiple_of(j * qs, qs)
        rows = pl.ds(r0, qs)
        valid = in_band & ((c >= BAND) | (i > 0) | (j > 0))
        for pr in range(SWA_HEADS // 2):
            cols = slice(pr * LANES, (pr + 1) * LANES)
            q2 = (q_ref[rows, cols].astype(F32) * (SWA_HD ** -0.5)).astype(BF16)
            k2 = k_scr[pl.ds(r0, nk), cols]
            v2 = v_scr[pl.ds(r0, nk), cols]
            k_bd = jnp.concatenate([jnp.where(first, k2, zero), jnp.where(first, zero, k2)], axis=0)
            v_bd = jnp.concatenate([jnp.where(first, v2, zero), jnp.where(first, zero, v2)], axis=0)
            s2 = _dot_nt(q2, k_bd)
            ps, ms = [], []
            for u in range(2):
                s = s2[:, u * nk:(u + 1) * nk] - (2.0 ** -(2 * pr + u + 1)) * distf
                s = jnp.where(valid, s, NEG)
                m = jnp.max(s, axis=-1, keepdims=True)
                ps.append(jnp.exp(s - m).astype(BF16))
                ms.append(m)
            od = _dot(jnp.concatenate(ps, axis=1), jnp.concatenate([v_bd, ones_bd], axis=1))
            den = od[:, LANES:]
            o_ref[rows, cols] = od[:, :LANES] / den
            lse_ref[rows, cols] = jnp.where(first_q, ms[0], ms[1]) + jnp.log(den)
        return carry

    lax.fori_loop(0, n_sub, sub_block, 0)


def _band_attn(q, k, v, dil):
    B, Tc, dw = q.shape
    W = dw // dil
    qb = min(4 * BAND, Tc)
    qs = min(2 * BAND, Tc)
    n_sub = qb // qs
    cur = pl.BlockSpec((None, qb, W), lambda b, r, i: (b, i, r))
    prev = pl.BlockSpec((None, BAND, W), lambda b, r, i: (b, jnp.maximum(i * (qb // BAND) - 1, 0), r))
    o, lse = pl.pallas_call(
        functools.partial(_band_attn_kernel, dil=dil, n_sub=n_sub, qs=qs),
        grid=(B, dil, Tc // qb),
        in_specs=[cur, prev, cur, prev, cur],
        out_specs=[cur, cur],
        out_shape=[jax.ShapeDtypeStruct((B, Tc, dw), F32)] * 2,
        scratch_shapes=[pltpu.VMEM((BAND + qb, W), BF16)] * 2,
        compiler_params=_cparams("parallel", "parallel", "parallel"),
        name="band_attn_d%d" % dil,
    )(q, k, k, v, v)
    return o.reshape(B * Tc, dw), lse.reshape(B * Tc, dw)


def _pattern_count(dist):
    cnt = jnp.zeros(dist.shape, F32)
    for window, dil in DILATED_PATTERNS:
        hit = (dist >= 0) & (dist <= window)
        if dil > 1:
            hit = hit & ((dist & (dil - 1)) == 0)
        cnt = cnt + hit.astype(F32)
    return cnt


def _samp_attn_kernel(q_ref, kn_ref, vn_ref, kp_ref, vp_ref, o_ref, ko_ref, vo_ref):
    P, Tn = kp_ref.shape[2], kn_ref.shape[2]
    qi = lax.broadcasted_iota(jnp.int32, (Tn, P), 0)
    dist_p = P + qi - lax.broadcasted_iota(jnp.int32, (Tn, P), 1)
    dist_n = lax.broadcasted_iota(jnp.int32, (Tn, Tn), 0) - lax.broadcasted_iota(jnp.int32, (Tn, Tn), 1)
    cnt_p, cnt_n = _pattern_count(dist_p), _pattern_count(dist_n)
    dpf, dnf = dist_p.astype(F32), dist_n.astype(F32)
    q = (q_ref[...] * (SWA_HD ** -0.5)).astype(BF16)
    outs = []
    for h in range(SWA_HEADS):
        kp, vp, kn, vn = kp_ref[h], vp_ref[h], kn_ref[h], vn_ref[h]
        ko_ref[h] = pltpu.roll(kp, P - Tn, axis=1)
        ko_ref[h, :, P - Tn:] = kn
        vo_ref[h] = pltpu.roll(vp, P - Tn, axis=1)
        vo_ref[h, :, P - Tn:] = vn
        qh = q[:, h * SWA_HD:(h + 1) * SWA_HD]
        slope = 2.0 ** -(h + 1)
        lp = jnp.where(cnt_p > 0, _dot(qh, kp.astype(BF16)) - slope * dpf, NEG)
        ln = jnp.where(cnt_n > 0, _dot(qh, kn.astype(BF16)) - slope * dnf, NEG)
        m = jnp.maximum(jnp.max(lp, axis=-1, keepdims=True), jnp.max(ln, axis=-1, keepdims=True))
        pp = cnt_p * jnp.exp(lp - m)
        pn = cnt_n * jnp.exp(ln - m)
        den = jnp.sum(pp, axis=-1, keepdims=True) + jnp.sum(pn, axis=-1, keepdims=True)
        outs.append((_dot_nt(pp.astype(BF16), vp.astype(BF16)) + _dot_nt(pn.astype(BF16), vn.astype(BF16))) / den)
    o_ref[...] = jnp.concatenate(outs, axis=1)


def _samp_attn(q, knT, vnT, kpT, vpT):
    B, Tn, W = q.shape
    P = kpT.shape[-1]
    qs = pl.BlockSpec((None, Tn, W), lambda b: (b, 0, 0))
    new = pl.BlockSpec((None, SWA_HEADS, SWA_HD, Tn), lambda b: (b, 0, 0, 0))
    past = pl.BlockSpec((None, SWA_HEADS, SWA_HD, P), lambda b: (b, 0, 0, 0))
    return pl.pallas_call(
        _samp_attn_kernel,
        grid=(B,),
        in_specs=[qs, new, new, past, past],
        out_specs=[qs, past, past],
        out_shape=[jax.ShapeDtypeStruct((B, Tn, W), F32), jax.ShapeDtypeStruct(kpT.shape, F32),
                   jax.ShapeDtypeStruct(kpT.shape, F32)],
        compiler_params=_cparams("parallel"),
        name="samp_attn",
    )(q, knT, vnT, kpT, vpT)


def _mem_kv_kernel(mem_ref, gmem_ref, wk_ref, wv_ref, gxk_ref, mk_ref, mv_ref):
    hm = _rms_rows(mem_ref[...], gmem_ref[...]).astype(BF16)
    mk = _dot(hm, wk_ref[...])
    gxk = gxk_ref[...]
    mv = _dot(hm, wv_ref[...])
    n = mem_ref.shape[0]
    for h in range(X_HEADS):
        sl = slice(h * X_HD, (h + 1) * X_HD)
        mk_ref[pl.ds(h, n, stride=X_HEADS), :] = _rms_rows(mk[:, sl], gxk)
        mv_ref[pl.ds(h, n, stride=X_HEADS), :] = mv[:, sl]


def _mem_kv(mem, gmem, wk, wv, gxk):
    n = mem.shape[0]
    return pl.pallas_call(
        _mem_kv_kernel,
        out_shape=[jax.ShapeDtypeStruct((n * X_HEADS, X_HD), F32)] * 2,
        compiler_params=pltpu.CompilerParams(vmem_limit_bytes=VMEM_LIMIT),
        name="mem_kv",
    )(mem, gmem, wk, wv, gxk)


def _post_kernel(*refs, n_pat, n_seg):
    n_swa = 2 * n_pat if n_pat > 1 else 1
    x_ref, og_ref = refs[0], refs[1]
    swa_refs = refs[2:2 + n_swa]
    (wout_ref, gx_ref, wxq_ref, gxq_ref, mk_ref, mv_ref, wxo_ref, gffn_ref, wr_ref, br_ref, tri_ref,
     cnt0_ref) = refs[2 + n_swa:14 + n_swa]
    n_scr = 3 if n_pat > 1 else 2
    x2_ref, h3_ref, ti_ref, tg_ref, cnt_ref = refs[len(refs) - n_scr - 5:len(refs) - n_scr]
    ox_scr, run_scr = refs[len(refs) - n_scr:len(refs) - n_scr + 2]

    @pl.when(pl.program_id(0) == 0)
    def _():
        run_scr[...] = cnt0_ref[...]

    if n_pat > 1:
        dei_scr = refs[-1]
        tm = x_ref.shape[0]
        vals = []
        for a, r in enumerate(swa_refs):
            d = DILATED_PATTERNS[a % n_pat][1]
            if d == 1:
                vals.append(r[...])
            else:
                slot, n_col = len(vals), SWA_W // LANES
                for c in range(d):
                    for g in range(n_col):
                        lo = c * SWA_W + g * LANES
                        dei_scr[slot, g, pl.ds(c, tm // d, stride=d), :] = r[:, lo:lo + LANES]
                vals.append(jnp.concatenate([dei_scr[slot, g] for g in range(n_col)], axis=1))
        o_p, l_p = vals[:n_pat], vals[n_pat:]
        lmax = functools.reduce(jnp.maximum, l_p)
        w_p = [jnp.exp(l - lmax) for l in l_p]
        o_swa = sum(w * o for w, o in zip(w_p, o_p)) / sum(w_p)
    else:
        o_swa = swa_refs[0][...]

    x1 = x_ref[...] + _dot(og_ref[...], wout_ref[0:GLA_V_W, :]) + _dot(o_swa.astype(BF16), wout_ref[GLA_V_W:, :])

    q = _dot(_rms_rows(x1, gx_ref[...]).astype(BF16), wxq_ref[...])
    gxq = gxq_ref[...]
    rows_t, seg = x1.shape[0], x1.shape[0] // n_seg
    if n_seg > 1:
        own = (lax.broadcasted_iota(jnp.int32, (rows_t, n_seg * MEM_LEN), 0) // seg
               == lax.broadcasted_iota(jnp.int32, (rows_t, n_seg * MEM_LEN), 1) // MEM_LEN)
    for h in range(X_HEADS):
        sl = slice(h * X_HD, (h + 1) * X_HD)
        qn = _rms_rows(q[:, sl], gxq).astype(BF16)
        mem_rows = pl.ds(h, MEM_LEN, stride=X_HEADS)
        mk_h = jnp.concatenate([mk_ref[s, mem_rows, :] for s in range(n_seg)], axis=0).astype(BF16)
        mv_h = jnp.concatenate([mv_ref[s, mem_rows, :] for s in range(n_seg)], axis=0).astype(BF16)
        sc = _dot_nt(qn, mk_h) * (X_HD ** -0.5)
        if n_seg > 1:
            sc = jnp.where(own, sc, NEG)
        p = jnp.exp(sc - jnp.max(sc, axis=-1, keepdims=True))
        p = p / jnp.sum(p, axis=-1, keepdims=True)
        ox_scr[:, sl] = _dot(p.astype(BF16), mv_h)
    x2 = x1 + _dot(ox_scr[...].astype(BF16), wxo_ref[...])
    x2_ref[...] = x2

    h3 = _rms_rows(x2, gffn_ref[...]).astype(BF16)
    h3_ref[...] = h3
    work = _dot(h3, wr_ref[...]) + br_ref[...]
    lane = lax.broadcasted_iota(jnp.int32, work.shape, 1)
    vals, idxs = [], []
    for _ in range(TOP_K):
        m = jnp.max(work, axis=-1, keepdims=True)
        idx = jnp.min(jnp.where(work == m, lane, LANES), axis=-1, keepdims=True)
        vals.append(m)
        idxs.append(idx)
        work = jnp.where(lane == idx, -jnp.inf, work)
    es = [jnp.exp(v - vals[0]) for v in vals]
    den = sum(es)
    sel = [lane == idx for idx in idxs]
    onehot = functools.reduce(jnp.logical_or, sel).astype(BF16)
    run = run_scr[...]
    rank = _dot(tri_ref[...], onehot) + run
    run_scr[...] = run + jnp.sum(onehot.astype(F32), axis=0, keepdims=True)
    cnt_ref[...] = run_scr[...]
    ti = jnp.zeros(work.shape, jnp.int32)
    tg = jnp.zeros(work.shape, F32)
    for j in range(TOP_K):
        pos = jnp.sum(jnp.where(sel[j], rank, 0.0), axis=-1, keepdims=True).astype(jnp.int32)
        ti = jnp.where(lane == j, idxs[j], ti)
        ti = jnp.where(lane == TOP_K + j, pos, ti)
        tg = jnp.where(lane == j, es[j] / den, tg)
    ti_ref[...] = ti
    tg_ref[...] = tg


def _post(x, og, swa, wout, gx, wxq, gxq, mk, mv, wxo, gffn, wr, br, cnt0, tm, n_seg, rows_per_mem, h3_rows, h3_row0,
          h3_prior):
    n = x.shape[0]
    n_pat = len(swa) // 2 if len(swa) > 1 else 1
    tri = jnp.asarray(np.tril(np.ones((tm, tm), np.float32), -1), BF16)
    row = lambda w: pl.BlockSpec((tm, w), lambda i: (i, 0))
    full = lambda a: pl.BlockSpec(a.shape, lambda i: (0,) * a.ndim)
    mem = pl.BlockSpec((n_seg, MEM_LEN * X_HEADS, X_HD), lambda i: ((i * tm) // (rows_per_mem * n_seg), 0, 0))
    swa_specs = [pl.BlockSpec((tm * SWA_W // a.shape[1], a.shape[1]), lambda i: (i, 0)) for a in swa]
    scratch = [pltpu.VMEM((tm, X_W), F32), pltpu.VMEM((1, LANES), F32)]
    if n_pat > 1:
        scratch.append(pltpu.VMEM((len(swa), SWA_W // LANES, tm, LANES), F32))
    return pl.pallas_call(
        functools.partial(_post_kernel, n_pat=n_pat, n_seg=n_seg),
        grid=(n // tm,),
        in_specs=[row(D_MODEL), row(GLA_V_W)] + swa_specs
        + [full(wout), full(gx), full(wxq), full(gxq), mem, mem, full(wxo), full(gffn), full(wr), full(br),
           full(tri), full(cnt0)] + [pl.BlockSpec(memory_space=pl.ANY)] * len(h3_prior),
        out_specs=[row(D_MODEL), pl.BlockSpec((tm, D_MODEL), lambda i: (i + h3_row0 // tm, 0)), row(LANES), row(LANES),
                   full(cnt0)],
        out_shape=[jax.ShapeDtypeStruct((n, D_MODEL), F32), jax.ShapeDtypeStruct((h3_rows, D_MODEL), BF16),
                   jax.ShapeDtypeStruct((n, LANES), jnp.int32), jax.ShapeDtypeStruct((n, LANES), F32),
                   jax.ShapeDtypeStruct((1, LANES), F32)],
        scratch_shapes=scratch,
        compiler_params=_cparams("arbitrary"),
        name="post",
        input_output_aliases={14 + len(swa): 1} if h3_prior else {},
    )(x, og, *swa, wout, gx, wxq, gxq, mk, mv, wxo, gffn, wr, br, tri, cnt0, *h3_prior)


def _moe_kernel(te_ref, nu_ref, x_ref, wg_ref, bg_ref, wu_ref, bu_ref, wd_ref, bd_ref, *rest):
    y_ref, wg_scr, wu_scr, wd_scr = rest[-4:]
    i = pl.program_id(0)
    live = i < nu_ref[0]

    @pl.when(live & ((i == 0) | (te_ref[i] != te_ref[jnp.maximum(i - 1, 0)])))
    def _():
        wg_scr[...] = wg_ref[...].astype(BF16)
        wu_scr[...] = wu_ref[...].astype(BF16)
        wd_scr[...] = wd_ref[...].astype(BF16)

    @pl.when(live)
    def _():
        x = x_ref[...]
        g = jnp.minimum(_dot(x, wg_scr[...]) + bg_ref[...], SWIGLU_LIMIT)
        u = jnp.clip(_dot(x, wu_scr[...]) + bu_ref[...], -SWIGLU_LIMIT, SWIGLU_LIMIT)
        a = g * _sigmoid(SWIGLU_ALPHA * g) * (u + 1.0)
        y_ref[...] = (_dot(a.astype(BF16), wd_scr[...]) + bd_ref[...]).astype(y_ref.dtype)

    @pl.when(jnp.logical_not(live))
    def _():
        y_ref[...] = jnp.zeros(y_ref.shape, y_ref.dtype)


MOE_CHUNK_ENDS = (0.05, 0.25, 1.0)


def _moe(tile_expert, n_used, h3, src_tok, wg, bg, wu, bu, wd, bd):
    P = src_tok.shape[0]
    tm = MOE_TM
    n_tiles = P // tm
    ys = None
    ends = [max(1, round(n_tiles * f)) for f in MOE_CHUNK_ENDS]
    for lo, hi in zip([0] + ends[:-1], ends):
        w_spec = pl.BlockSpec((None, D_MODEL, D_MODEL), lambda i, te, *_: (te[i], 0, 0))
        b_spec = pl.BlockSpec((None, 1, D_MODEL), lambda i, te, *_: (te[i], 0, 0))
        prior = [] if ys is None else [ys]
        ys = pl.pallas_call(
            _moe_kernel,
            grid_spec=pltpu.PrefetchScalarGridSpec(
                num_scalar_prefetch=2,
                grid=(hi - lo,),
                in_specs=[pl.BlockSpec((tm, D_MODEL), lambda i, *_: (i, 0)),
                          w_spec, b_spec, w_spec, b_spec, w_spec, b_spec]
                + [pl.BlockSpec(memory_space=pl.ANY)] * len(prior),
                out_specs=pl.BlockSpec((tm, D_MODEL), lambda i, *_, lo=lo: (i + lo, 0)),
                scratch_shapes=[pltpu.VMEM((D_MODEL, D_MODEL), BF16)] * 3,
            ),
            out_shape=jax.ShapeDtypeStruct((P, D_MODEL), BF16),
            input_output_aliases={9: 0} if prior else {},
            compiler_params=_cparams("arbitrary"),
            name="moe",
        )(tile_expert[lo:hi], n_used - lo, _sc_take(h3, src_tok[lo * tm:hi * tm]), wg, bg, wu, bu, wd, bd, *prior)
    return ys


def _route(top_i, rank, counts, tm):
    n = top_i.shape[0]
    a = n * TOP_K
    n_tiles = a // tm + N_EXPERTS
    tiles_e = (counts + tm - 1) // tm
    tile_end = jnp.cumsum(tiles_e)
    slot0 = (tile_end - tiles_e) * tm
    experts = jnp.arange(N_EXPERTS, dtype=jnp.int32)
    dest = rank + jnp.sum(jnp.where(top_i[..., None] == experts, slot0, 0), axis=-1)
    n_used = tile_end[-1:].astype(jnp.int32)
    tile_expert = jnp.minimum(
        jnp.sum((tile_end[None, :] <= jnp.arange(n_tiles, dtype=jnp.int32)[:, None]).astype(jnp.int32), axis=1),
        N_EXPERTS - 1)
    pad_tok = -1 - (jnp.arange(n_tiles * tm, dtype=jnp.int32) % n)
    src_tok = pad_tok.at[dest.reshape(a)].max(jnp.arange(a, dtype=jnp.int32) // TOP_K, unique_indices=True)
    src_tok = jnp.where(src_tok < 0, -1 - src_tok, src_tok)
    return tile_expert, n_used, src_tok, dest


def _combine_kernel(x_ref, y_ref, g_ref, *rest):
    o_ref = rest[-1]
    acc = x_ref[...]
    g = g_ref[...]
    for j in range(TOP_K):
        acc = acc + g[:, j:j + 1] * y_ref[j].astype(F32)
    o_ref[...] = acc


COMBINE_CHUNKS = 4


def _combine(x2, ysg, tg, row0, tm, prior):
    n = ysg.shape[1]
    off = row0 // tm
    return pl.pallas_call(
        _combine_kernel,
        grid=(n // tm,),
        in_specs=[pl.BlockSpec((tm, D_MODEL), lambda i: (i + off, 0)),
                  pl.BlockSpec((TOP_K, tm, D_MODEL), lambda i: (0, i, 0)),
                  pl.BlockSpec((tm, LANES), lambda i: (i + off, 0))] + [pl.BlockSpec(memory_space=pl.ANY)] * len(prior),
        out_specs=pl.BlockSpec((tm, D_MODEL), lambda i: (i + off, 0)),
        out_shape=jax.ShapeDtypeStruct(x2.shape, F32),
        input_output_aliases={3: 0} if prior else {},
        compiler_params=_cparams("parallel"),
        name="combine",
    )(x2, ysg, tg, *prior)


@compute_on("tpu_sparsecore")
@jax.jit
def _sc_take(x, idx):
    return jnp.take(x, idx, axis=0, mode="clip")


def _mixer_inputs(x2d, w, tm, classes, seq_len=0, win_len=0):
    return _in_proj(x2d, w["gmix"], w["w_in_p"], w["wa2_p"], w["ba"], w["gsq"], w["gsk"], w["bd"], tm, classes,
                    seq_len, win_len)


def kernel(x_prompt, x_sample, mem_prompt, state_gla, cache_swa_k, cache_swa_v, cache_mem_k, cache_mem_v, g_mix, w_in, w_gla_a2, b_gla_a, g_gla_out, g_swa_q, g_swa_k, w_out, g_mem, w_mk, w_mv, g_xk, g_xattn, w_xq, g_xq, w_xo, g_ffn, w_router, b_router, w_gate, b_gate, w_up, b_up, w_down, b_down):
    B, T, D = x_prompt.shape
    Bs, Ts, _ = x_sample.shape
    P = cache_swa_k.shape[2]
    l = 0

    wi = w_in[l]
    segs = np.cumsum((0, GLA_QK_W, GLA_QK_W, GLA_V_W, GLA_V_W, GLA_RANK, SWA_W, SWA_W, SWA_W))
    gq_c, gk_c, gv_c, gr_c, ga_c, sq_c, sk_c, sv_c = [wi[:, segs[j]:segs[j + 1]] for j in range(8)]
    w_in_p = jnp.concatenate(
        [gq_c, gk_c, gv_c, gr_c, sq_c, sk_c, sv_c, ga_c, jnp.zeros((D, D_IN_PAD - _C_GA - GLA_RANK), F32)],
        axis=1).astype(BF16)
    heads = np.arange(SWA_W) // SWA_HD
    w = dict(
        gmix=g_mix[l][None], w_in_p=w_in_p,
        wa2_p=jnp.concatenate([w_gla_a2[l], jnp.zeros((LANES - GLA_RANK, GLA_QK_W), F32)], axis=0).astype(BF16),
        ba=b_gla_a[l][None],
        gsq=jnp.tile(g_swa_q[l], SWA_HEADS)[None], gsk=jnp.tile(g_swa_k[l], SWA_HEADS)[None],
        bd=jnp.asarray(heads[:, None] == heads[None, :], BF16),
    )
    gout = g_gla_out[l][None]
    wout = w_out[l].astype(BF16)
    wxq, wxo = w_xq[l].astype(BF16), w_xo[l].astype(BF16)
    wr = jnp.concatenate([w_router[l], jnp.zeros((D, LANES - N_EXPERTS), F32)], axis=1).astype(BF16)
    br = jnp.concatenate([b_router[l], jnp.full((LANES - N_EXPERTS,), NEG, F32)])[None]
    post_w = (wout, g_xattn[l][None], wxq, g_xq[l][None])
    post_w2 = (wxo, g_ffn[l][None], wr, br)

    w_buf = min(SWA_WINDOW, T)
    gq, gk, gv, gr, la, sq, sk, sv, *cls, kT_p, vT_p = _mixer_inputs(x_prompt.reshape(B * T, D), w, IN_PROJ_TM, True,
                                                                     T, w_buf)
    r3 = lambda a: a.reshape(B, T, a.shape[-1])
    og_p, st_p = _gla(r3(gq), r3(gk), r3(gv), r3(la), r3(gr),
                      jnp.zeros((B, GLA_QK_W, GLA_DV), F32), gout, GLA_TB)
    o_pat, l_pat = [], []
    for _, dil in DILATED_PATTERNS:
        if dil == 1:
            qkv = [r3(sq), r3(sk), r3(sv)]
        else:
            c = CLASS_DILS.index(dil)
            qkv = [cls[j * len(CLASS_DILS) + c].reshape(B, T // dil, dil * SWA_W) for j in range(3)]
        o_c, l_c = _band_attn(*qkv, dil)
        o_pat.append(o_c)
        l_pat.append(l_c)
    mk_p, mv_p = _mem_kv(mem_prompt.reshape(B * MEM_LEN, D), g_mem[l][None], w_mk[l].astype(BF16),
                         w_mv[l].astype(BF16), g_xk[l][None])
    n_p, n_all = B * T, B * T + Bs * Ts
    x2_p, h3, ti_p, tg_p, cnt_p = _post(x_prompt.reshape(B * T, D), og_p.reshape(B * T, GLA_V_W), o_pat + l_pat,
                                        *post_w, mk_p.reshape(B, MEM_LEN * X_HEADS, X_HD),
                                        mv_p.reshape(B, MEM_LEN * X_HEADS, X_HD),
                                        *post_w2, jnp.zeros((1, LANES), F32), tm=POST_TM, n_seg=1, rows_per_mem=T,
                                        h3_rows=n_all, h3_row0=0, h3_prior=[])

    gq, gk, gv, gr, la, sq, sk, sv = _mixer_inputs(x_sample.reshape(Bs * Ts, D), w, Bs * Ts, False)
    pad = lambda a: jnp.pad(a.reshape(Bs, Ts, a.shape[-1]), ((0, 0), (0, GLA_CHUNK - Ts), (0, 0)))
    og_s, st_s = _gla(pad(gq), pad(gk), pad(gv), pad(la), pad(gr),
                      state_gla[l].reshape(Bs, GLA_QK_W, GLA_DV), gout, GLA_CHUNK)
    og_s = og_s[:, :Ts].reshape(Bs * Ts, GLA_V_W)
    rows_minor = lambda a: jnp.transpose(a.reshape(Bs, -1, SWA_HEADS, SWA_HD), (0, 2, 3, 1))
    o_swa_s, kb_s, vb_s = _samp_attn(sq.reshape(Bs, Ts, SWA_W), rows_minor(sk), rows_minor(sv),
                                     rows_minor(cache_swa_k[l]), rows_minor(cache_swa_v[l]))
    kb_s, vb_s = jnp.transpose(kb_s, (0, 3, 1, 2)), jnp.transpose(vb_s, (0, 3, 1, 2))
    x2_s, h3, ti_s, tg_s, cnt = _post(x_sample.reshape(Bs * Ts, D), og_s, [o_swa_s.reshape(Bs * Ts, SWA_W)],
                                      *post_w, cache_mem_k[l].reshape(Bs, MEM_LEN * X_HEADS, X_HD),
                                      cache_mem_v[l].reshape(Bs, MEM_LEN * X_HEADS, X_HD),
                                      *post_w2, cnt_p, tm=SAMPLE_SEQS * Ts, n_seg=SAMPLE_SEQS, rows_per_mem=Ts,
                                      h3_rows=n_all, h3_row0=n_p, h3_prior=[h3])

    ti = jnp.concatenate([ti_p[:, :2 * TOP_K], ti_s[:, :2 * TOP_K]], axis=0)
    tile_expert, n_used, src_tok, dest = _route(ti[:, :TOP_K], ti[:, TOP_K:], cnt[0, :N_EXPERTS].astype(jnp.int32),
                                                MOE_TM)
    ys = _moe(tile_expert, n_used, h3, src_tok, w_gate[l], b_gate[l][:, None, :], w_up[l],
              b_up[l][:, None, :], w_down[l], b_down[l][:, None, :])
    def combined(x2, tg, row0, first, n, tm, prior):
        ysg = _sc_take(ys, dest[first:first + n].T.reshape(-1)).reshape(TOP_K, n, D)
        return [_combine(x2, ysg, tg, row0, tm, prior)]

    y_p, n_c = [], n_p // COMBINE_CHUNKS
    for c in range(COMBINE_CHUNKS):
        y_p = combined(x2_p, tg_p, c * n_c, c * n_c, n_c, COMBINE_TM, y_p)
    y_p, y_s = y_p[0], combined(x2_s, tg_s, 0, n_p, Bs * Ts, Bs * Ts, [])[0]

    kv_p = lambda a: jnp.transpose(a.reshape(B, SWA_HEADS, SWA_HD, w_buf), (0, 3, 1, 2))[None]
    return (y_p.reshape(B, T, D), y_s.reshape(Bs, Ts, D),
            st_p.reshape(1, B, GLA_HEADS, GLA_DK, GLA_DV), kv_p(kT_p), kv_p(vT_p),
            mk_p.reshape(1, B, MEM_LEN, X_HEADS, X_HD), mv_p.reshape(1, B, MEM_LEN, X_HEADS, X_HD),
            st_s.reshape(1, Bs, GLA_HEADS, GLA_DK, GLA_DV),
            kb_s.reshape(1, Bs, P, SWA_HEADS, SWA_HD), vb_s.reshape(1, Bs, P, SWA_HEADS, SWA_HD))
```

```python
import functools

import numpy as np
import jax
import jax.numpy as jnp
from jax import lax
from jax.experimental import pallas as pl
from jax.experimental.compute_on import compute_on
from jax.experimental.pallas import tpu as pltpu

F32 = jnp.float32
BF16 = jnp.bfloat16

D_MODEL = 1024
GLA_HEADS = 4
GLA_DK = 64
GLA_DV = 128
GLA_RANK = 16
GLA_TAU = 16.0
GLA_CHUNK = 128
GLA_SUB = 32
SWA_HEADS = 8
SWA_HD = 64
DILATED_PATTERNS = ((128, 1), (512, 4), (2048, 16))
BAND = 128
SWA_WINDOW = 2048
MEM_LEN = 256
X_HEADS = 4
X_HD = 128
N_EXPERTS = 32
TOP_K = 4
SWIGLU_ALPHA = 1.702
SWIGLU_LIMIT = 7.0
EPS = 1e-6

GLA_QK_W = GLA_HEADS * GLA_DK
GLA_V_W = GLA_HEADS * GLA_DV
SWA_W = SWA_HEADS * SWA_HD
X_W = X_HEADS * X_HD
LANES = 128
NEG = -1e30
VMEM_LIMIT = 56 * 1024 * 1024

_C_GQ, _C_GK, _C_GV, _C_GR, _C_SQ, _C_SK, _C_SV, _C_GA = 0, 256, 512, 1024, 1536, 2048, 2560, 3072
D_IN_PAD = 3200

IN_PROJ_TM = 512
GLA_TB = 512
POST_TM = 512
COMBINE_TM = 512
MOE_TM = 512
SAMPLE_SEQS = 8


def _cparams(*sem):
    return pltpu.CompilerParams(dimension_semantics=sem, vmem_limit_bytes=VMEM_LIMIT)


def _dot(a, b):
    return jnp.dot(a, b, preferred_element_type=F32)


def _dot_nt(a, b):
    return lax.dot_general(a, b, (((1,), (1,)), ((), ())), preferred_element_type=F32)


def _rms_rows(x, g):
    return x * lax.rsqrt(jnp.mean(x * x, axis=-1, keepdims=True) + EPS) * g


def _split_bf16(x, n):
    out = []
    for _ in range(n - 1):
        hi = x.astype(BF16)
        out.append(hi)
        x = x - hi.astype(F32)
    out.append(x.astype(BF16))
    return out


def _group_rms(z, g, ones_bd, group):
    hi, lo = _split_bf16(z * z, 2)
    ss = _dot(hi, ones_bd) + _dot(lo, ones_bd)
    return z * lax.rsqrt(ss * (1.0 / group) + EPS) * g


def _log_sigmoid(x):
    return jnp.minimum(x, 0.0) - jnp.log1p(jnp.exp(-jnp.abs(x)))


def _sigmoid(x):
    return 1.0 / (1.0 + jnp.exp(-x))


CLASS_DILS = tuple(d for _, d in DILATED_PATTERNS if d > 1)


def _in_proj_kernel(x_ref, gmix_ref, w_ref, wa2_ref, ba_ref, gsq_ref, gsk_ref, bd_ref,
                    oq_ref, ok_ref, ov_ref, or_ref, ola_ref, osq_ref, osk_ref, osv_ref, *rest, classes, seq_tiles,
                    win_tiles):
    h = _rms_rows(x_ref[...], gmix_ref[...]).astype(BF16)

    def proj(lo, width):
        return _dot(h, w_ref[:, lo:lo + width])

    oq_ref[...] = proj(_C_GQ, GLA_QK_W) * (GLA_DK ** -0.5)
    ok_ref[...] = proj(_C_GK, GLA_QK_W)
    ov_ref[...] = proj(_C_GV, GLA_V_W)
    or_ref[...] = proj(_C_GR, GLA_V_W)
    osv_ref[...] = proj(_C_SV, SWA_W)
    ga = proj(_C_GA, LANES)
    xa = _dot(ga.astype(BF16), wa2_ref[...]) + ba_ref[...]
    ola_ref[...] = _log_sigmoid(xa) * (1.0 / GLA_TAU)
    bd = bd_ref[...]
    osq_ref[...] = _group_rms(proj(_C_SQ, SWA_W), gsq_ref[...], bd, SWA_HD)
    osk_ref[...] = _group_rms(proj(_C_SK, SWA_W), gsk_ref[...], bd, SWA_HD)

    if classes:
        kT_ref, vT_ref = rest[-3], rest[-2]
        rest = rest[:-3] + rest[-1:]

        @pl.when(pl.program_id(0) % seq_tiles >= seq_tiles - win_tiles)
        def _():
            kT_ref[...] = osk_ref[...].T
            vT_ref[...] = osv_ref[...].T

        cls_refs, col_scr = rest[:-1], rest[-1]
        tm = x_ref.shape[0]
        n_col = SWA_W // LANES
        for j, src in enumerate((osq_ref, osk_ref, osv_ref)):
            for g in range(n_col):
                col_scr[g] = src[:, g * LANES:(g + 1) * LANES]
            for c, d in enumerate(CLASS_DILS):
                dst = cls_refs[j * len(CLASS_DILS) + c]
                for r in range(d):
                    for g in range(n_col):
                        lo = r * SWA_W + g * LANES
                        dst[:, lo:lo + LANES] = col_scr[g, pl.ds(r, tm // d, stride=d), :].astype(BF16)


def _in_proj(x, gmix, w_in_p, wa2_p, ba, gsq, gsk, bd, tm, classes, seq_len=0, win_len=0):
    n = x.shape[0]
    row = lambda w: pl.BlockSpec((tm, w), lambda i: (i, 0))
    full = lambda a: pl.BlockSpec(a.shape, lambda i: (0,) * a.ndim)
    widths = (GLA_QK_W, GLA_QK_W, GLA_V_W, GLA_V_W, GLA_QK_W, SWA_W, SWA_W, SWA_W)
    out_specs = [row(w) for w in widths]
    out_shape = [jax.ShapeDtypeStruct((n, w), F32) for w in widths]
    seq_tiles, win_tiles = seq_len // tm, win_len // tm
    if classes:
        for _ in range(3):
            for d in CLASS_DILS:
                out_specs.append(pl.BlockSpec((tm // d, d * SWA_W), lambda i: (i, 0)))
                out_shape.append(jax.ShapeDtypeStruct((n // d, d * SWA_W), BF16))
        win = pl.BlockSpec((None, SWA_W, tm), lambda i: (i // seq_tiles, 0,
                                                         jnp.maximum(i % seq_tiles - (seq_tiles - win_tiles), 0)))
        out_specs += [win, win]
        out_shape += [jax.ShapeDtypeStruct((n // seq_len, SWA_W, win_len), F32)] * 2
    return pl.pallas_call(
        functools.partial(_in_proj_kernel, classes=classes, seq_tiles=seq_tiles, win_tiles=win_tiles),
        grid=(n // tm,),
        in_specs=[row(D_MODEL), full(gmix), full(w_in_p), full(wa2_p), full(ba), full(gsq), full(gsk), full(bd)],
        out_specs=out_specs,
        out_shape=out_shape,
        scratch_shapes=[pltpu.VMEM((SWA_W // LANES, tm, LANES), F32)] if classes else [],
        compiler_params=_cparams("arbitrary"),
        name="in_proj",
    )(x, gmix, w_in_p, wa2_p, ba, gsq, gsk, bd)


GLA_NB = 2


def _gla_kernel(q_ref, k_ref, v_ref, la_ref, r_ref, s0_ref, gout_ref, lcat_ref,
                o_ref, s_ref, s_scr, knt_scr, rhs_scr, *, n_chunks):
    tb = pl.program_id(1)
    C, S, NS, H, DK, DV = GLA_CHUNK, GLA_SUB, GLA_CHUNK // GLA_SUB, GLA_HEADS, GLA_DK, GLA_DV

    @pl.when(tb == 0)
    def _():
        s_scr[...] = s0_ref[...]
        knt_scr[...] = jnp.zeros(knt_scr.shape, BF16)
        rhs_scr[...] = jnp.zeros(rhs_scr.shape, BF16)

    lcat = lcat_ref[...]
    gout = gout_ref[...]
    row = lax.broadcasted_iota(jnp.int32, (C, H * DK), 0)
    tril = (lax.broadcasted_iota(jnp.int32, (C, H * C), 1) & (C - 1)) <= lax.broadcasted_iota(jnp.int32, (C, H * C), 0)
    ones_k = jnp.ones((3 * C, LANES), BF16)

    def chunk(c, carry):
        r0 = pl.multiple_of(c * C, C)
        rows = pl.ds(r0, C)
        for n in range(GLA_NB):
            g = la_ref[n, rows, :]
            b_r = _dot(lcat, jnp.concatenate(_split_bf16(g, 3), axis=1))
            b_r = b_r[:, :H * DK] + b_r[:, H * DK:2 * H * DK] + b_r[:, 2 * H * DK:]
            b = b_r[:C]
            ref = b_r[C:]
            b_end = b[C - 1:C, :]
            b_col = _dot(jnp.concatenate(_split_bf16(g.T, 3), axis=1), ones_k)
            q = q_ref[n, rows, :]
            k = k_ref[n, rows, :]
            v = v_ref[n, rows, :]
            q_hat = q * jnp.exp(b - ref)
            q_til = q * jnp.exp(b)
            k_dec = k * jnp.exp(b_end - b)
            for i in range(NS):
                e = ref[S * i:S * i + 1, :] - b
                piece = (k * jnp.exp(jnp.where(row < S * (i + 1), e, NEG))).astype(BF16)
                for h in range(H):
                    lo = i * H * DK + h * DK
                    knt_scr[n, h * C:(h + 1) * C, lo:lo + DK] = piece[:, h * DK:(h + 1) * DK]
            q_cat = jnp.concatenate([jnp.where((row >= S * i) & (row < S * (i + 1)), q_hat, 0.0) for i in range(NS)],
                                    axis=1).astype(BF16)
            sc = jnp.where(tril, _dot_nt(q_cat, knt_scr[n]), 0.0)
            st = s_scr[n]
            for h in range(H):
                vl = slice(h * DV, (h + 1) * DV)
                rhs_scr[n, h * C:(h + 1) * C, vl] = v[:, vl].astype(BF16)
                rhs_scr[n, H * C + h * DK:H * C + (h + 1) * DK, vl] = st[h * DK:(h + 1) * DK, :].astype(BF16)
            o = _dot(jnp.concatenate([sc, q_til], axis=1).astype(BF16), rhs_scr[n])
            kv = _dot(k_dec.T.astype(BF16), v.astype(BF16))
            s_scr[n] = st * jnp.exp(b_col) + jnp.concatenate(
                [kv[h * DK:(h + 1) * DK, h * DV:(h + 1) * DV] for h in range(H)], axis=0)
            gate = r_ref[n, rows, :]
            for h in range(H):
                vl = slice(h * DV, (h + 1) * DV)
                o_ref[n, rows, vl] = (_rms_rows(o[:, vl], gout) * (gate[:, vl] * _sigmoid(gate[:, vl]))).astype(o_ref.dtype)
        return carry

    lax.fori_loop(0, n_chunks, chunk, 0)

    @pl.when(tb == pl.num_programs(1) - 1)
    def _():
        s_ref[...] = s_scr[...]


def _gla_consts():
    C, S = GLA_CHUNK, GLA_SUB
    t = np.arange(C)
    incl = (t[None, :] <= t[:, None]).astype(np.float32)
    upto = (t[None, :] < (t[:, None] // S) * S).astype(np.float32)
    return jnp.asarray(np.concatenate([incl, upto], axis=0), BF16)


def _gla(q, k, v, la, r, s0, gout, tb):
    B, T, _ = q.shape
    H, C, DK, DV, NS = GLA_HEADS, GLA_CHUNK, GLA_DK, GLA_DV, GLA_CHUNK // GLA_SUB
    lcat = _gla_consts()
    blk = lambda w: pl.BlockSpec((GLA_NB, tb, w), lambda b, t: (b, t, 0))
    full = lambda a: pl.BlockSpec(a.shape, lambda b, t: (0,) * a.ndim)
    st_spec = pl.BlockSpec((GLA_NB, H * DK, DV), lambda b, t: (b, 0, 0))
    return pl.pallas_call(
        functools.partial(_gla_kernel, n_chunks=tb // C),
        grid=(B // GLA_NB, T // tb),
        in_specs=[blk(GLA_QK_W), blk(GLA_QK_W), blk(GLA_V_W), blk(GLA_QK_W), blk(GLA_V_W), st_spec,
                  full(gout), full(lcat)],
        out_specs=[blk(GLA_V_W), st_spec],
        out_shape=[jax.ShapeDtypeStruct((B, T, GLA_V_W), BF16),
                   jax.ShapeDtypeStruct((B, H * DK, DV), F32)],
        scratch_shapes=[pltpu.VMEM((GLA_NB, H * DK, DV), F32),
                        pltpu.VMEM((GLA_NB, H * C, NS * H * DK), BF16),
                        pltpu.VMEM((GLA_NB, H * C + H * DK, H * DV), BF16)],
        compiler_params=_cparams("parallel", "arbitrary"),
        name="gla",
    )(q, k, v, la, r, s0, gout, lcat)


def _band_attn_kernel(q_ref, kp_ref, kc_ref, vp_ref, vc_ref, o_ref, lse_ref, k_scr, v_scr, *, dil, n_sub, qs):
    i = pl.program_id(2)
    k_scr[0:BAND, :] = kp_ref[...].astype(BF16)
    k_scr[BAND:, :] = kc_ref[...].astype(BF16)
    v_scr[0:BAND, :] = vp_ref[...].astype(BF16)
    v_scr[BAND:, :] = vc_ref[...].astype(BF16)
    nk = qs + BAND
    t = lax.broadcasted_iota(jnp.int32, (qs, nk), 0)
    c = lax.broadcasted_iota(jnp.int32, (qs, nk), 1)
    dist = BAND + t - c
    in_band = (dist >= 0) & (dist <= BAND)
    distf = (dist * dil).astype(F32)
    first = lax.broadcasted_iota(jnp.int32, (nk, LANES), 1) < SWA_HD
    first_q = lax.broadcasted_iota(jnp.int32, (qs, LANES), 1) < SWA_HD
    zero = jnp.zeros((nk, LANES), BF16)
    ones_bd = jnp.concatenate([first, jnp.logical_not(first)], axis=0).astype(BF16)

    def sub_block(j, carry):
        r0 = pl.multiple_of(j * qs, qs)
        rows = pl.ds(r0, qs)
        valid = in_band & ((c >= BAND) | (i > 0) | (j > 0))
        for pr in range(SWA_HEADS // 2):
            cols = slice(pr * LANES, (pr + 1) * LANES)
            q2 = (q_ref[rows, cols].astype(F32) * (SWA_HD ** -0.5)).astype(BF16)
            k2 = k_scr[pl.ds(r0, nk), cols]
            v2 = v_scr[pl.ds(r0, nk), cols]
            k_bd = jnp.concatenate([jnp.where(first, k2, zero), jnp.where(first, zero, k2)], axis=0)
            v_bd = jnp.concatenate([jnp.where(first, v2, zero), jnp.where(first, zero, v2)], axis=0)
            s2 = _dot_nt(q2, k_bd)
            ps, ms = [], []
            for u in range(2):
                s = s2[:, u * nk:(u + 1) * nk] - (2.0 ** -(2 * pr + u + 1)) * distf
                s = jnp.where(valid, s, NEG)
                m = jnp.max(s, axis=-1, keepdims=True)
                ps.append(jnp.exp(s - m).astype(BF16))
                ms.append(m)
            od = _dot(jnp.concatenate(ps, axis=1), jnp.concatenate([v_bd, ones_bd], axis=1))
            den = od[:, LANES:]
            o_ref[rows, cols] = od[:, :LANES] / den
            lse_ref[rows, cols] = jnp.where(first_q, ms[0], ms[1]) + jnp.log(den)
        return carry

    lax.fori_loop(0, n_sub, sub_block, 0)


def _band_attn(q, k, v, dil):
    B, Tc, dw = q.shape
    W = dw // dil
    qb = min(8 * BAND, Tc)
    qs = min(2 * BAND, Tc)
    n_sub = qb // qs
    cur = pl.BlockSpec((None, qb, W), lambda b, r, i: (b, i, r))
    prev = pl.BlockSpec((None, BAND, W), lambda b, r, i: (b, jnp.maximum(i * (qb // BAND) - 1, 0), r))
    o, lse = pl.pallas_call(
        functools.partial(_band_attn_kernel, dil=dil, n_sub=n_sub, qs=qs),
        grid=(B, dil, Tc // qb),
        in_specs=[cur, prev, cur, prev, cur],
        out_specs=[cur, cur],
        out_shape=[jax.ShapeDtypeStruct((B, Tc, dw), F32)] * 2,
        scratch_shapes=[pltpu.VMEM((BAND + qb, W), BF16)] * 2,
        compiler_params=_cparams("parallel", "parallel", "parallel"),
        name="band_attn_d%d" % dil,
    )(q, k, k, v, v)
    return o.reshape(B * Tc, dw), lse.reshape(B * Tc, dw)


def _pattern_count(dist):
    cnt = jnp.zeros(dist.shape, F32)
    for window, dil in DILATED_PATTERNS:
        hit = (dist >= 0) & (dist <= window)
        if dil > 1:
            hit = hit & ((dist & (dil - 1)) == 0)
        cnt = cnt + hit.astype(F32)
    return cnt


def _samp_attn_kernel(q_ref, kn_ref, vn_ref, kp_ref, vp_ref, o_ref, ko_ref, vo_ref):
    P, Tn = kp_ref.shape[2], kn_ref.shape[2]
    qi = lax.broadcasted_iota(jnp.int32, (Tn, P), 0)
    dist_p = P + qi - lax.broadcasted_iota(jnp.int32, (Tn, P), 1)
    dist_n = lax.broadcasted_iota(jnp.int32, (Tn, Tn), 0) - lax.broadcasted_iota(jnp.int32, (Tn, Tn), 1)
    cnt_p, cnt_n = _pattern_count(dist_p), _pattern_count(dist_n)
    dpf, dnf = dist_p.astype(F32), dist_n.astype(F32)
    q = (q_ref[...] * (SWA_HD ** -0.5)).astype(BF16)
    outs = []
    for h in range(SWA_HEADS):
        kp, vp, kn, vn = kp_ref[h], vp_ref[h], kn_ref[h], vn_ref[h]
        ko_ref[h] = pltpu.roll(kp, P - Tn, axis=1)
        ko_ref[h, :, P - Tn:] = kn
        vo_ref[h] = pltpu.roll(vp, P - Tn, axis=1)
        vo_ref[h, :, P - Tn:] = vn
        qh = q[:, h * SWA_HD:(h + 1) * SWA_HD]
        slope = 2.0 ** -(h + 1)
        lp = jnp.where(cnt_p > 0, _dot(qh, kp.astype(BF16)) - slope * dpf, NEG)
        ln = jnp.where(cnt_n > 0, _dot(qh, kn.astype(BF16)) - slope * dnf, NEG)
        m = jnp.maximum(jnp.max(lp, axis=-1, keepdims=True), jnp.max(ln, axis=-1, keepdims=True))
        pp = cnt_p * jnp.exp(lp - m)
        pn = cnt_n * jnp.exp(ln - m)
        den = jnp.sum(pp, axis=-1, keepdims=True) + jnp.sum(pn, axis=-1, keepdims=True)
        outs.append((_dot_nt(pp.astype(BF16), vp.astype(BF16)) + _dot_nt(pn.astype(BF16), vn.astype(BF16))) / den)
    o_ref[...] = jnp.concatenate(outs, axis=1)


def _samp_attn(q, knT, vnT, kpT, vpT):
    B, Tn, W = q.shape
    P = kpT.shape[-1]
    qs = pl.BlockSpec((None, Tn, W), lambda b: (b, 0, 0))
    new = pl.BlockSpec((None, SWA_HEADS, SWA_HD, Tn), lambda b: (b, 0, 0, 0))
    past = pl.BlockSpec((None, SWA_HEADS, SWA_HD, P), lambda b: (b, 0, 0, 0))
    return pl.pallas_call(
        _samp_attn_kernel,
        grid=(B,),
        in_specs=[qs, new, new, past, past],
        out_specs=[qs, past, past],
        out_shape=[jax.ShapeDtypeStruct((B, Tn, W), F32), jax.ShapeDtypeStruct(kpT.shape, F32),
                   jax.ShapeDtypeStruct(kpT.shape, F32)],
        compiler_params=_cparams("parallel"),
        name="samp_attn",
    )(q, knT, vnT, kpT, vpT)


def _mem_kv_kernel(mem_ref, gmem_ref, wk_ref, wv_ref, gxk_ref, mk_ref, mv_ref):
    hm = _rms_rows(mem_ref[...], gmem_ref[...]).astype(BF16)
    mk = _dot(hm, wk_ref[...])
    gxk = gxk_ref[...]
    mv = _dot(hm, wv_ref[...])
    n = mem_ref.shape[0]
    for h in range(X_HEADS):
        sl = slice(h * X_HD, (h + 1) * X_HD)
        mk_ref[pl.ds(h, n, stride=X_HEADS), :] = _rms_rows(mk[:, sl], gxk)
        mv_ref[pl.ds(h, n, stride=X_HEADS), :] = mv[:, sl]


def _mem_kv(mem, gmem, wk, wv, gxk):
    n = mem.shape[0]
    return pl.pallas_call(
        _mem_kv_kernel,
        out_shape=[jax.ShapeDtypeStruct((n * X_HEADS, X_HD), F32)] * 2,
        compiler_params=pltpu.CompilerParams(vmem_limit_bytes=VMEM_LIMIT),
        name="mem_kv",
    )(mem, gmem, wk, wv, gxk)


def _post_kernel(*refs, n_pat, n_seg):
    n_swa = 2 * n_pat if n_pat > 1 else 1
    x_ref, og_ref = refs[0], refs[1]
    swa_refs = refs[2:2 + n_swa]
    (wout_ref, gx_ref, wxq_ref, gxq_ref, mk_ref, mv_ref, wxo_ref, gffn_ref, wr_ref, br_ref, tri_ref,
     cnt0_ref) = refs[2 + n_swa:14 + n_swa]
    n_scr = 3 if n_pat > 1 else 2
    x2_ref, h3_ref, ti_ref, tg_ref, cnt_ref = refs[len(refs) - n_scr - 5:len(refs) - n_scr]
    ox_scr, run_scr = refs[len(refs) - n_scr:len(refs) - n_scr + 2]

    @pl.when(pl.program_id(0) == 0)
    def _():
        run_scr[...] = cnt0_ref[...]

    if n_pat > 1:
        dei_scr = refs[-1]
        tm = x_ref.shape[0]
        vals = []
        for a, r in enumerate(swa_refs):
            d = DILATED_PATTERNS[a % n_pat][1]
            if d == 1:
                vals.append(r[...])
            else:
                slot, n_col = len(vals), SWA_W // LANES
                for c in range(d):
                    for g in range(n_col):
                        lo = c * SWA_W + g * LANES
                        dei_scr[slot, g, pl.ds(c, tm // d, stride=d), :] = r[:, lo:lo + LANES]
                vals.append(jnp.concatenate([dei_scr[slot, g] for g in range(n_col)], axis=1))
        o_p, l_p = vals[:n_pat], vals[n_pat:]
        lmax = functools.reduce(jnp.maximum, l_p)
        w_p = [jnp.exp(l - lmax) for l in l_p]
        o_swa = sum(w * o for w, o in zip(w_p, o_p)) / sum(w_p)
    else:
        o_swa = swa_refs[0][...]

    x1 = x_ref[...] + _dot(og_ref[...], wout_ref[0:GLA_V_W, :]) + _dot(o_swa.astype(BF16), wout_ref[GLA_V_W:, :])

    q = _dot(_rms_rows(x1, gx_ref[...]).astype(BF16), wxq_ref[...])
    gxq = gxq_ref[...]
    rows_t, seg = x1.shape[0], x1.shape[0] // n_seg
    if n_seg > 1:
        own = (lax.broadcasted_iota(jnp.int32, (rows_t, n_seg * MEM_LEN), 0) // seg
               == lax.broadcasted_iota(jnp.int32, (rows_t, n_seg * MEM_LEN), 1) // MEM_LEN)
    for h in range(X_HEADS):
        sl = slice(h * X_HD, (h + 1) * X_HD)
        qn = _rms_rows(q[:, sl], gxq).astype(BF16)
        mem_rows = pl.ds(h, MEM_LEN, stride=X_HEADS)
        mk_h = jnp.concatenate([mk_ref[s, mem_rows, :] for s in range(n_seg)], axis=0).astype(BF16)
        mv_h = jnp.concatenate([mv_ref[s, mem_rows, :] for s in range(n_seg)], axis=0).astype(BF16)
        sc = _dot_nt(qn, mk_h) * (X_HD ** -0.5)
        if n_seg > 1:
            sc = jnp.where(own, sc, NEG)
        p = jnp.exp(sc - jnp.max(sc, axis=-1, keepdims=True))
        p = p / jnp.sum(p, axis=-1, keepdims=True)
        ox_scr[:, sl] = _dot(p.astype(BF16), mv_h)
    x2 = x1 + _dot(ox_scr[...].astype(BF16), wxo_ref[...])
    x2_ref[...] = x2

    h3 = _rms_rows(x2, gffn_ref[...]).astype(BF16)
    h3_ref[...] = h3
    work = _dot(h3, wr_ref[...]) + br_ref[...]
    lane = lax.broadcasted_iota(jnp.int32, work.shape, 1)
    vals, idxs = [], []
    for _ in range(TOP_K):
        m = jnp.max(work, axis=-1, keepdims=True)
        idx = jnp.min(jnp.where(work == m, lane, LANES), axis=-1, keepdims=True)
        vals.append(m)
        idxs.append(idx)
        work = jnp.where(lane == idx, -jnp.inf, work)
    es = [jnp.exp(v - vals[0]) for v in vals]
    den = sum(es)
    sel = [lane == idx for idx in idxs]
    onehot = functools.reduce(jnp.logical_or, sel).astype(BF16)
    run = run_scr[...]
    rank = _dot(tri_ref[...], onehot) + run
    run_scr[...] = run + jnp.sum(onehot.astype(F32), axis=0, keepdims=True)
    cnt_ref[...] = run_scr[...]
    ti = jnp.zeros(work.shape, jnp.int32)
    tg = jnp.zeros(work.shape, F32)
    for j in range(TOP_K):
        pos = jnp.sum(jnp.where(sel[j], rank, 0.0), axis=-1, keepdims=True).astype(jnp.int32)
        ti = jnp.where(lane == j, idxs[j], ti)
        ti = jnp.where(lane == TOP_K + j, pos, ti)
        tg = jnp.where(lane == j, es[j] / den, tg)
    ti_ref[...] = ti
    tg_ref[...] = tg


def _post(x, og, swa, wout, gx, wxq, gxq, mk, mv, wxo, gffn, wr, br, cnt0, tm, n_seg, rows_per_mem, h3_rows, h3_row0,
          h3_prior):
    n = x.shape[0]
    n_pat = len(swa) // 2 if len(swa) > 1 else 1
    tri = jnp.asarray(np.tril(np.ones((tm, tm), np.float32), -1), BF16)
    row = lambda w: pl.BlockSpec((tm, w), lambda i: (i, 0))
    full = lambda a: pl.BlockSpec(a.shape, lambda i: (0,) * a.ndim)
    mem = pl.BlockSpec((n_seg, MEM_LEN * X_HEADS, X_HD), lambda i: ((i * tm) // (rows_per_mem * n_seg), 0, 0))
    swa_specs = [pl.BlockSpec((tm * SWA_W // a.shape[1], a.shape[1]), lambda i: (i, 0)) for a in swa]
    scratch = [pltpu.VMEM((tm, X_W), F32), pltpu.VMEM((1, LANES), F32)]
    if n_pat > 1:
        scratch.append(pltpu.VMEM((len(swa), SWA_W // LANES, tm, LANES), F32))
    return pl.pallas_call(
        functools.partial(_post_kernel, n_pat=n_pat, n_seg=n_seg),
        grid=(n // tm,),
        in_specs=[row(D_MODEL), row(GLA_V_W)] + swa_specs
        + [full(wout), full(gx), full(wxq), full(gxq), mem, mem, full(wxo), full(gffn), full(wr), full(br),
           full(tri), full(cnt0)] + [pl.BlockSpec(memory_space=pl.ANY)] * len(h3_prior),
        out_specs=[row(D_MODEL), pl.BlockSpec((tm, D_MODEL), lambda i: (i + h3_row0 // tm, 0)), row(LANES), row(LANES),
                   full(cnt0)],
        out_shape=[jax.ShapeDtypeStruct((n, D_MODEL), F32), jax.ShapeDtypeStruct((h3_rows, D_MODEL), BF16),
                   jax.ShapeDtypeStruct((n, LANES), jnp.int32), jax.ShapeDtypeStruct((n, LANES), F32),
                   jax.ShapeDtypeStruct((1, LANES), F32)],
        scratch_shapes=scratch,
        compiler_params=_cparams("arbitrary"),
        name="post",
        input_output_aliases={14 + len(swa): 1} if h3_prior else {},
    )(x, og, *swa, wout, gx, wxq, gxq, mk, mv, wxo, gffn, wr, br, tri, cnt0, *h3_prior)


def _moe_kernel(te_ref, nu_ref, x_ref, wg_ref, bg_ref, wu_ref, bu_ref, wd_ref, bd_ref, *rest):
    y_ref, wg_scr, wu_scr, wd_scr = rest[-4:]
    i = pl.program_id(0)
    live = i < nu_ref[0]

    @pl.when(live & ((i == 0) | (te_ref[i] != te_ref[jnp.maximum(i - 1, 0)])))
    def _():
        wg_scr[...] = wg_ref[...].astype(BF16)
        wu_scr[...] = wu_ref[...].astype(BF16)
        wd_scr[...] = wd_ref[...].astype(BF16)

    @pl.when(live)
    def _():
        x = x_ref[...]
        g = jnp.minimum(_dot(x, wg_scr[...]) + bg_ref[...], SWIGLU_LIMIT)
        u = jnp.clip(_dot(x, wu_scr[...]) + bu_ref[...], -SWIGLU_LIMIT, SWIGLU_LIMIT)
        a = g * _sigmoid(SWIGLU_ALPHA * g) * (u + 1.0)
        y_ref[...] = (_dot(a.astype(BF16), wd_scr[...]) + bd_ref[...]).astype(y_ref.dtype)

    @pl.when(jnp.logical_not(live))
    def _():
        y_ref[...] = jnp.zeros(y_ref.shape, y_ref.dtype)


MOE_CHUNK_ENDS = (0.05, 0.25, 1.0)


def _moe(tile_expert, n_used, h3, src_tok, wg, bg, wu, bu, wd, bd):
    P = src_tok.shape[0]
    tm = MOE_TM
    n_tiles = P // tm
    ys = None
    ends = [max(1, round(n_tiles * f)) for f in MOE_CHUNK_ENDS]
    for lo, hi in zip([0] + ends[:-1], ends):
        w_spec = pl.BlockSpec((None, D_MODEL, D_MODEL), lambda i, te, *_: (te[i], 0, 0))
        b_spec = pl.BlockSpec((None, 1, D_MODEL), lambda i, te, *_: (te[i], 0, 0))
        prior = [] if ys is None else [ys]
        ys = pl.pallas_call(
            _moe_kernel,
            grid_spec=pltpu.PrefetchScalarGridSpec(
                num_scalar_prefetch=2,
                grid=(hi - lo,),
                in_specs=[pl.BlockSpec((tm, D_MODEL), lambda i, *_: (i, 0)),
                          w_spec, b_spec, w_spec, b_spec, w_spec, b_spec]
                + [pl.BlockSpec(memory_space=pl.ANY)] * len(prior),
                out_specs=pl.BlockSpec((tm, D_MODEL), lambda i, *_, lo=lo: (i + lo, 0)),
                scratch_shapes=[pltpu.VMEM((D_MODEL, D_MODEL), BF16)] * 3,
            ),
            out_shape=jax.ShapeDtypeStruct((P, D_MODEL), BF16),
            input_output_aliases={9: 0} if prior else {},
            compiler_params=_cparams("arbitrary"),
            name="moe",
        )(tile_expert[lo:hi], n_used - lo, _sc_take(h3, src_tok[lo * tm:hi * tm]), wg, bg, wu, bu, wd, bd, *prior)
    return ys


def _route(top_i, rank, counts, tm):
    n = top_i.shape[0]
    a = n * TOP_K
    n_tiles = a // tm + N_EXPERTS
    tiles_e = (counts + tm - 1) // tm
    tile_end = jnp.cumsum(tiles_e)
    slot0 = (tile_end - tiles_e) * tm
    experts = jnp.arange(N_EXPERTS, dtype=jnp.int32)
    dest = rank + jnp.sum(jnp.where(top_i[..., None] == experts, slot0, 0), axis=-1)
    n_used = tile_end[-1:].astype(jnp.int32)
    tile_expert = jnp.minimum(
        jnp.sum((tile_end[None, :] <= jnp.arange(n_tiles, dtype=jnp.int32)[:, None]).astype(jnp.int32), axis=1),
        N_EXPERTS - 1)
    pad_tok = -1 - (jnp.arange(n_tiles * tm, dtype=jnp.int32) % n)
    src_tok = pad_tok.at[dest.reshape(a)].max(jnp.arange(a, dtype=jnp.int32) // TOP_K, unique_indices=True)
    src_tok = jnp.where(src_tok < 0, -1 - src_tok, src_tok)
    return tile_expert, n_used, src_tok, dest


def _combine_kernel(x_ref, y_ref, g_ref, *rest):
    o_ref = rest[-1]
    acc = x_ref[...]
    g = g_ref[...]
    for j in range(TOP_K):
        acc = acc + g[:, j:j + 1] * y_ref[j].astype(F32)
    o_ref[...] = acc


COMBINE_CHUNKS = 4


def _combine(x2, ysg, tg, row0, tm, prior):
    n = ysg.shape[1]
    off = row0 // tm
    return pl.pallas_call(
        _combine_kernel,
        grid=(n // tm,),
        in_specs=[pl.BlockSpec((tm, D_MODEL), lambda i: (i + off, 0)),
                  pl.BlockSpec((TOP_K, tm, D_MODEL), lambda i: (0, i, 0)),
                  pl.BlockSpec((tm, LANES), lambda i: (i + off, 0))] + [pl.BlockSpec(memory_space=pl.ANY)] * len(prior),
        out_specs=pl.BlockSpec((tm, D_MODEL), lambda i: (i + off, 0)),
        out_shape=jax.ShapeDtypeStruct(x2.shape, F32),
        input_output_aliases={3: 0} if prior else {},
        compiler_params=_cparams("parallel"),
        name="combine",
    )(x2, ysg, tg, *prior)


@compute_on("tpu_sparsecore")
@jax.jit
def _sc_take(x, idx):
    return jnp.take(x, idx, axis=0, mode="clip")


def _mixer_inputs(x2d, w, tm, classes, seq_len=0, win_len=0):
    return _in_proj(x2d, w["gmix"], w["w_in_p"], w["wa2_p"], w["ba"], w["gsq"], w["gsk"], w["bd"], tm, classes,
                    seq_len, win_len)


def kernel(x_prompt, x_sample, mem_prompt, state_gla, cache_swa_k, cache_swa_v, cache_mem_k, cache_mem_v, g_mix, w_in, w_gla_a2, b_gla_a, g_gla_out, g_swa_q, g_swa_k, w_out, g_mem, w_mk, w_mv, g_xk, g_xattn, w_xq, g_xq, w_xo, g_ffn, w_router, b_router, w_gate, b_gate, w_up, b_up, w_down, b_down):
    B, T, D = x_prompt.shape
    Bs, Ts, _ = x_sample.shape
    P = cache_swa_k.shape[2]
    l = 0

    wi = w_in[l]
    segs = np.cumsum((0, GLA_QK_W, GLA_QK_W, GLA_V_W, GLA_V_W, GLA_RANK, SWA_W, SWA_W, SWA_W))
    gq_c, gk_c, gv_c, gr_c, ga_c, sq_c, sk_c, sv_c = [wi[:, segs[j]:segs[j + 1]] for j in range(8)]
    w_in_p = jnp.concatenate(
        [gq_c, gk_c, gv_c, gr_c, sq_c, sk_c, sv_c, ga_c, jnp.zeros((D, D_IN_PAD - _C_GA - GLA_RANK), F32)],
        axis=1).astype(BF16)
    heads = np.arange(SWA_W) // SWA_HD
    w = dict(
        gmix=g_mix[l][None], w_in_p=w_in_p,
        wa2_p=jnp.concatenate([w_gla_a2[l], jnp.zeros((LANES - GLA_RANK, GLA_QK_W), F32)], axis=0).astype(BF16),
        ba=b_gla_a[l][None],
        gsq=jnp.tile(g_swa_q[l], SWA_HEADS)[None], gsk=jnp.tile(g_swa_k[l], SWA_HEADS)[None],
        bd=jnp.asarray(heads[:, None] == heads[None, :], BF16),
    )
    gout = g_gla_out[l][None]
    wout = w_out[l].astype(BF16)
    wxq, wxo = w_xq[l].astype(BF16), w_xo[l].astype(BF16)
    wr = jnp.concatenate([w_router[l], jnp.zeros((D, LANES - N_EXPERTS), F32)], axis=1).astype(BF16)
    br = jnp.concatenate([b_router[l], jnp.full((LANES - N_EXPERTS,), NEG, F32)])[None]
    post_w = (wout, g_xattn[l][None], wxq, g_xq[l][None])
    post_w2 = (wxo, g_ffn[l][None], wr, br)

    w_buf = min(SWA_WINDOW, T)
    gq, gk, gv, gr, la, sq, sk, sv, *cls, kT_p, vT_p = _mixer_inputs(x_prompt.reshape(B * T, D), w, IN_PROJ_TM, True,
                                                                     T, w_buf)
    r3 = lambda a: a.reshape(B, T, a.shape[-1])
    og_p, st_p = _gla(r3(gq), r3(gk), r3(gv), r3(la), r3(gr),
                      jnp.zeros((B, GLA_QK_W, GLA_DV), F32), gout, GLA_TB)
    o_pat, l_pat = [], []
    for _, dil in DILATED_PATTERNS:
        if dil == 1:
            qkv = [r3(sq), r3(sk), r3(sv)]
        else:
            c = CLASS_DILS.index(dil)
            qkv = [cls[j * len(CLASS_DILS) + c].reshape(B, T // dil, dil * SWA_W) for j in range(3)]
        o_c, l_c = _band_attn(*qkv, dil)
        o_pat.append(o_c)
        l_pat.append(l_c)
    mk_p, mv_p = _mem_kv(mem_prompt.reshape(B * MEM_LEN, D), g_mem[l][None], w_mk[l].astype(BF16),
                         w_mv[l].astype(BF16), g_xk[l][None])
    n_p, n_all = B * T, B * T + Bs * Ts
    x2_p, h3, ti_p, tg_p, cnt_p = _post(x_prompt.reshape(B * T, D), og_p.reshape(B * T, GLA_V_W), o_pat + l_pat,
                                        *post_w, mk_p.reshape(B, MEM_LEN * X_HEADS, X_HD),
                                        mv_p.reshape(B, MEM_LEN * X_HEADS, X_HD),
                                        *post_w2, jnp.zeros((1, LANES), F32), tm=POST_TM, n_seg=1, rows_per_mem=T,
                                        h3_rows=n_all, h3_row0=0, h3_prior=[])

    gq, gk, gv, gr, la, sq, sk, sv = _mixer_inputs(x_sample.reshape(Bs * Ts, D), w, Bs * Ts, False)
    pad = lambda a: jnp.pad(a.reshape(Bs, Ts, a.shape[-1]), ((0, 0), (0, GLA_CHUNK - Ts), (0, 0)))
    og_s, st_s = _gla(pad(gq), pad(gk), pad(gv), pad(la), pad(gr),
                      state_gla[l].reshape(Bs, GLA_QK_W, GLA_DV), gout, GLA_CHUNK)
    og_s = og_s[:, :Ts].reshape(Bs * Ts, GLA_V_W)
    rows_minor = lambda a: jnp.transpose(a.reshape(Bs, -1, SWA_HEADS, SWA_HD), (0, 2, 3, 1))
    o_swa_s, kb_s, vb_s = _samp_attn(sq.reshape(Bs, Ts, SWA_W), rows_minor(sk), rows_minor(sv),
                                     rows_minor(cache_swa_k[l]), rows_minor(cache_swa_v[l]))
    kb_s, vb_s = jnp.transpose(kb_s, (0, 3, 1, 2)), jnp.transpose(vb_s, (0, 3, 1, 2))
    x2_s, h3, ti_s, tg_s, cnt = _post(x_sample.reshape(Bs * Ts, D), og_s, [o_swa_s.reshape(Bs * Ts, SWA_W)],
                                      *post_w, cache_mem_k[l].reshape(Bs, MEM_LEN * X_HEADS, X_HD),
                                      cache_mem_v[l].reshape(Bs, MEM_LEN * X_HEADS, X_HD),
                                      *post_w2, cnt_p, tm=SAMPLE_SEQS * Ts, n_seg=SAMPLE_SEQS, rows_per_mem=Ts,
                                      h3_rows=n_all, h3_row0=n_p, h3_prior=[h3])

    ti = jnp.concatenate([ti_p[:, :2 * TOP_K], ti_s[:, :2 * TOP_K]], axis=0)
    tile_expert, n_used, src_tok, dest = _route(ti[:, :TOP_K], ti[:, TOP_K:], cnt[0, :N_EXPERTS].astype(jnp.int32),
                                                MOE_TM)
    ys = _moe(tile_expert, n_used, h3, src_tok, w_gate[l], b_gate[l][:, None, :], w_up[l],
              b_up[l][:, None, :], w_down[l], b_down[l][:, None, :])
    def combined(x2, tg, row0, first, n, tm, prior):
        ysg = _sc_take(ys, dest[first:first + n].T.reshape(-1)).reshape(TOP_K, n, D)
        return [_combine(x2, ysg, tg, row0, tm, prior)]

    y_p, n_c = [], n_p // COMBINE_CHUNKS
    for c in range(COMBINE_CHUNKS):
        y_p = combined(x2_p, tg_p, c * n_c, c * n_c, n_c, COMBINE_TM, y_p)
    y_p, y_s = y_p[0], combined(x2_s, tg_s, 0, n_p, Bs * Ts, Bs * Ts, [])[0]

    kv_p = lambda a: jnp.transpose(a.reshape(B, SWA_HEADS, SWA_HD, w_buf), (0, 3, 1, 2))[None]
    return (y_p.reshape(B, T, D), y_s.reshape(Bs, Ts, D),
            st_p.reshape(1, B, GLA_HEADS, GLA_DK, GLA_DV), kv_p(kT_p), kv_p(vT_p),
            mk_p.reshape(1, B, MEM_LEN, X_HEADS, X_HD), mv_p.reshape(1, B, MEM_LEN, X_HEADS, X_HD),
            st_s.reshape(1, Bs, GLA_HEADS, GLA_DK, GLA_DV),
            kb_s.reshape(1, Bs, P, SWA_HEADS, SWA_HD), vb_s.reshape(1, Bs, P, SWA_HEADS, SWA_HD))
```
